```python
import math
import jax, jax.numpy as jnp
from jax import lax
import numpy as np

D_MODEL = 1024
BATCH = 4
SEQ = 4096
DEPTH = 2

GRID_W = 64
CTX_LEN = 256
N_BRANCH = 3
EPS = 1e-6

HY_WIDTH = D_MODEL
HY_ORDER = 2
HY_SHORT_CONV = 3
HY_BANDS = 16
HY_EMB = 2 * HY_BANDS + 1
HY_FFN = 64
HY_WINDOW_SHIFT = 0.05
HY_DECAY_SHORT = 0.3
HY_DECAY_LONG = 1.5
HY_DECAY_TARGET = 1e-2

SSD_INNER = D_MODEL
SSD_HEADDIM = 64
SSD_HEADS = SSD_INNER // SSD_HEADDIM
SSD_GROUPS = 2
SSD_STATE = 128
SSD_SHORT_CONV = 3
SSD_CHUNK = 128

DA_HEAD_DIM = 64
DA_V_DIM = 2 * DA_HEAD_DIM
DA_HEADS = D_MODEL // DA_V_DIM
DA_WIDTH = DA_HEADS * DA_V_DIM
ROPE_BASE = 10000.0
ROPE_FREQS = DA_HEAD_DIM // 4
Q_BLOCK = 128

DENSE_FF = 4 * D_MODEL
N_EXPERTS = 8
TOP_K = 2
EXPERT_FF = 2 * D_MODEL

HY_COLS = (HY_ORDER + 1) * HY_WIDTH
SSD_XBC = SSD_INNER + 2 * SSD_GROUPS * SSD_STATE
SSD_COLS = SSD_INNER + SSD_XBC + 2 * SSD_HEADS
DA_QK = DA_HEADS * 2 * DA_HEAD_DIM
DA_COLS = 2 * DA_QK + DA_WIDTH
GATE_COLS = N_BRANCH * D_MODEL
IN_COLS = HY_COLS + SSD_COLS + DA_COLS + GATE_COLS

kernel_name = 'hybrid_hyena_ssd_diffattn_moe_block'


def rmsnorm(x, g):
    xf = x.astype(jnp.float32)
    y = xf * lax.rsqrt(jnp.mean(xf * xf, axis=-1, keepdims=True) + EPS)
    return (y * g.astype(jnp.float32)).astype(x.dtype)


def modulate(h, shift, scale):
    return h * (1 + scale) + shift


def dwconv(u, w, b):
    k = w.shape[0]
    y = lax.conv_general_dilated(u, w[:, None, :], window_strides=(1,), padding=[((k - 1) // 2, k // 2)],
                                 dimension_numbers=('NWC', 'WIO', 'NWC'), feature_group_count=u.shape[-1])
    return y + b


def hyena_filters(L, w1, b1, w2, b2, w3, b3, w4, freq):
    f32 = jnp.float32
    t = jnp.linspace(0.0, 1.0, L, dtype=f32)[:, None]
    w = 2.0 * math.pi * jnp.arange(L, dtype=f32)[:, None] / L
    f = jnp.linspace(1e-4, HY_BANDS - 1, HY_BANDS, dtype=f32)[None, :]
    feats = jnp.concatenate([t, jnp.cos(f * w), -jnp.sin(f * w)], axis=-1)
    h = jnp.sin(freq[0] * (feats @ w1 + b1))
    h = jnp.sin(freq[1] * (h @ w2 + b2))
    h = jnp.sin(freq[2] * (h @ w3 + b3))
    k = (h @ w4).astype(f32).reshape(L, HY_ORDER, 2, HY_WIDTH)
    max_decay = math.log(HY_DECAY_TARGET) / HY_DECAY_SHORT
    min_decay = math.log(HY_DECAY_TARGET) / HY_DECAY_LONG
    deltas = jnp.abs(jnp.linspace(min_decay, max_decay, HY_WIDTH, dtype=f32))
    window = jnp.exp(-t * deltas) + HY_WINDOW_SHIFT
    return k * window[:, None, None, :]


def bidir_fftconv(u, kf, kb, bias):
    L, C = u.shape[1], u.shape[2]
    k = jnp.concatenate([kf[:1] + kb[:1], kf[1:], jnp.zeros((1, C), kf.dtype), kb[:0:-1]], axis=0)
    uf = u.astype(jnp.float32)
    y = jnp.fft.irfft(jnp.fft.rfft(uf, n=2 * L, axis=1) * jnp.fft.rfft(k, n=2 * L, axis=0)[None], n=2 * L, axis=1)[:, :L]
    return (y + uf * bias.astype(jnp.float32)).astype(u.dtype)


def hyena_seq(p, conv_w, conv_b, w1, b1, w2, b2, w3, b3, w4, freq, bias):
    filt = hyena_filters(p.shape[1], w1, b1, w2, b2, w3, b3, w4, freq)
    u = dwconv(p, conv_w, conv_b)
    v, x1, x2 = jnp.split(u, 3, axis=-1)
    z = x1 * bidir_fftconv(v, filt[:, 0, 0], filt[:, 0, 1], bias[0])
    return x2 * bidir_fftconv(z, filt[:, 1, 0], filt[:, 1, 1], bias[1])


def segsum(a):
    T = a.shape[-1]
    x = jnp.broadcast_to(a[..., :, None], a.shape + (T,))
    x = jnp.cumsum(jnp.where(jnp.tril(jnp.ones((T, T), bool), -1), x, 0.0), axis=-2)
    return jnp.where(jnp.tril(jnp.ones((T, T), bool)), x, -jnp.inf)


def ssd_scan(X, A, Bm, Cm, init, need_y):
    b, l, h, p = X.shape
    g, n = Bm.shape[2], Bm.shape[3]
    r = h // g
    q = SSD_CHUNK
    c = l // q
    X = X.reshape(b, c, q, g, r, p)
    A = A.reshape(b, c, q, g, r).transpose(0, 3, 4, 1, 2)
    Bm = Bm.reshape(b, c, q, g, n)
    Cm = Cm.reshape(b, c, q, g, n)
    A_cs = jnp.cumsum(A, axis=-1)
    decay_states = jnp.exp(A_cs[..., -1:] - A_cs)
    states = jnp.einsum('bclgn,bgrcl,bclgrp->bcgrpn', Bm, decay_states, X)
    states = jnp.concatenate([init.reshape(b, 1, g, r, p, n), states], axis=1)
    decay_chunk = jnp.exp(segsum(jnp.pad(A_cs[..., -1], ((0, 0), (0, 0), (0, 0), (1, 0)))))
    new_states = jnp.einsum('bgrzc,bcgrpn->bzgrpn', decay_chunk, states)
    final = new_states[:, -1].reshape(b, h, p, n)
    if not need_y:
        return None, final
    cb = jnp.einsum('bclgn,bcsgn->bgcls', Cm, Bm)
    m = cb[:, :, None] * jnp.exp(segsum(A))
    y_diag = jnp.einsum('bgrcls,bcsgrp->bclgrp', m, X)
    y_off = jnp.einsum('bclgn,bcgrpn,bgrcl->bclgrp', Cm, new_states[:, :-1], jnp.exp(A_cs))
    return (y_diag + y_off).reshape(b, l, h, p), final


def ssd_prepare(p, conv_w, conv_b, dt_bias):
    f32 = jnp.float32
    b, l = p.shape[:2]
    z, xbc, dt_raw = jnp.split(p, [SSD_INNER, SSD_INNER + SSD_XBC], axis=-1)
    xbc = jax.nn.silu(dwconv(xbc, conv_w, conv_b)).astype(f32)
    xs, bm, cm = jnp.split(xbc, [SSD_INNER, SSD_INNER + SSD_GROUPS * SSD_STATE], axis=-1)
    xs = xs.reshape(b, l, SSD_HEADS, SSD_HEADDIM)
    bm = bm.reshape(b, l, SSD_GROUPS, SSD_STATE)
    cm = cm.reshape(b, l, SSD_GROUPS, SSD_STATE)
    dt = jax.nn.softplus(dt_raw.astype(f32).reshape(b, l, 2, SSD_HEADS) + dt_bias.astype(f32))
    return z, xs, bm, cm, dt


def ssd_direction(xs, bm, cm, dt, a, init, reverse, need_y):
    if reverse:
        xs, bm, cm, dt = (jnp.flip(t, 1) for t in (xs, bm, cm, dt))
    y, final = ssd_scan(xs * dt[..., None], dt * a, bm, cm, init, need_y)
    if reverse and need_y:
        y = jnp.flip(y, 1)
    return y, final


def ssd_finish(y, xs, z, d_skip, norm_g):
    f32 = jnp.float32
    b, l = z.shape[:2]
    y = (y + xs * d_skip.astype(f32)[:, None]).reshape(b, l, SSD_INNER) * jax.nn.silu(z.astype(f32))
    yg = y.reshape(b, l, SSD_GROUPS, SSD_INNER // SSD_GROUPS)
    yg = yg * lax.rsqrt(jnp.mean(yg * yg, axis=-1, keepdims=True) + EPS)
    return (yg.reshape(b, l, SSD_INNER) * norm_g.astype(f32)).astype(z.dtype)


def axial_rope(L):
    rows = L // GRID_W
    row = jnp.repeat(jnp.arange(rows), GRID_W)
    col = jnp.tile(jnp.arange(GRID_W), rows)
    inv = ROPE_BASE ** (-jnp.arange(ROPE_FREQS, dtype=jnp.float32) / ROPE_FREQS)
    ang = jnp.stack([row, col], axis=-1).astype(jnp.float32)[:, :, None] * inv
    return jnp.cos(ang), jnp.sin(ang)


def apply_rope(x, cos, sin):
    b, L, H, c, d = x.shape
    xr = x.reshape(b, L, H, c, 2, 2, d // 4)
    x1, x2 = xr[..., 0, :], xr[..., 1, :]
    cs, sn = cos[:, None, None], sin[:, None, None]
    out = jnp.stack([x1 * cs - x2 * sn, x2 * cs + x1 * sn], axis=-2)
    return out.reshape(x.shape).astype(x.dtype)


def qk_heads(t, gain):
    b, l = t.shape[:2]
    return rmsnorm(t.reshape(b, l, DA_HEADS, 2, DA_HEAD_DIM), gain)


def v_heads(t):
    b, l = t.shape[:2]
    return t.reshape(b, l, DA_HEADS, DA_V_DIM)


def diff_attend(q, k, v, lam):
    f32 = jnp.float32
    s = jnp.einsum('bqhcd,bkhcd->bhcqk', q.astype(f32), k.astype(f32)) * (DA_HEAD_DIM ** -0.5)
    p = jax.nn.softmax(s, axis=-1)
    a = p[:, :, 0] - lam * p[:, :, 1]
    return jnp.einsum('bhqk,bkhe->bqhe', a, v.astype(f32))


def latent_attention(q, k, v, lam):
    b, l = q.shape[:2]
    nb = l // Q_BLOCK
    qb = q.reshape(b, nb, Q_BLOCK, DA_HEADS, 2, DA_HEAD_DIM).transpose(1, 0, 2, 3, 4, 5)
    out = lax.map(lambda blk: diff_attend(blk, k, v, lam), qb)
    return out.transpose(1, 0, 2, 3, 4).reshape(b, l, DA_HEADS, DA_V_DIM)


def da_finish(o, subln_g, lam_init, dtype):
    b, l = o.shape[:2]
    o = rmsnorm(o, subln_g) * (1.0 - lam_init)
    return o.reshape(b, l, DA_WIDTH).astype(dtype)


def merge_branches(ys, gate_logits, w_branch, w_out):
    b, l = gate_logits.shape[:2]
    proj = jnp.einsum('blie,ied->blid', jnp.stack(ys, axis=2), w_branch)
    gates = jax.nn.sigmoid(gate_logits.reshape(b, l, N_BRANCH, D_MODEL).astype(jnp.float32)).astype(proj.dtype)
    return jnp.sum(gates * proj, axis=2) @ w_out


def swiglu(x, w1, w3, w2):
    return (jax.nn.silu(x @ w1) * (x @ w3)) @ w2


def moe_swiglu(x, router_w, w1, w3, w2):
    f32 = jnp.float32
    logits = (x @ router_w).astype(f32)
    top_vals, top_idx = lax.top_k(logits, TOP_K)
    weights = jax.nn.softmax(top_vals, axis=-1)
    gate = jnp.sum(jax.nn.one_hot(top_idx, N_EXPERTS, dtype=f32) * weights[..., None], axis=-2)
    out = jnp.zeros(x.shape, f32)
    for e in range(N_EXPERTS):
        out = out + gate[..., e:e + 1] * swiglu(x, w1[e], w3[e], w2[e]).astype(f32)
    return out.astype(x.dtype)


def token_mixer(p_lat, p_ctx, hy_conv_w, hy_conv_b, hy_w1, hy_b1, hy_w2, hy_b2, hy_w3, hy_b3, hy_w4, hy_freq, hy_bias,
                ssd_conv_w, ssd_conv_b, ssd_dt_bias, ssd_a_log, ssd_d, ssd_norm_g,
                da_q_norm, da_k_norm, da_lambda, da_subln_g, w_branch, w_out, layer_num, need_ctx):
    f32 = jnp.float32
    cuts = [HY_COLS, HY_COLS + SSD_COLS, HY_COLS + SSD_COLS + DA_COLS]
    hy_l, ssd_l, da_l, gate_l = jnp.split(p_lat, cuts, axis=-1)
    hy_c, ssd_c, da_c, gate_c = jnp.split(p_ctx, cuts, axis=-1)
    hy_args = (hy_conv_w, hy_conv_b, hy_w1, hy_b1, hy_w2, hy_b2, hy_w3, hy_b3, hy_w4, hy_freq, hy_bias)

    y_hy_l = hyena_seq(hy_l, *hy_args)

    a = -jnp.exp(ssd_a_log.astype(f32))
    z_l, xs_l, b_l, c_l, dt_l = ssd_prepare(ssd_l, ssd_conv_w, ssd_conv_b, ssd_dt_bias)
    z_c, xs_c, b_c, c_c, dt_c = ssd_prepare(ssd_c, ssd_conv_w, ssd_conv_b, ssd_dt_bias)
    zero = jnp.zeros((p_ctx.shape[0], SSD_HEADS, SSD_HEADDIM, SSD_STATE), f32)
    yl_dirs, yc_dirs = [], []
    for d in range(2):
        rev = d == 1
        yc, state = ssd_direction(xs_c, b_c, c_c, dt_c[:, :, d], a[d], zero, rev, need_ctx)
        yl, _ = ssd_direction(xs_l, b_l, c_l, dt_l[:, :, d], a[d], state, rev, True)
        yl_dirs.append(yl)
        yc_dirs.append(yc)
    y_ssd_l = ssd_finish(yl_dirs[0] + yl_dirs[1], xs_l, z_l, ssd_d, ssd_norm_g)

    lam_init = 0.8 - 0.6 * math.exp(-0.3 * layer_num)
    lv = da_lambda.astype(f32)
    lam = jnp.exp(jnp.sum(lv[0] * lv[1])) - jnp.exp(jnp.sum(lv[2] * lv[3])) + lam_init
    q_l, k_l, v_l = jnp.split(da_l, [DA_QK, 2 * DA_QK], axis=-1)
    k_c, v_c = jnp.split(da_c[..., DA_QK:], [DA_QK], axis=-1)
    cos, sin = axial_rope(p_lat.shape[1])
    q_l = apply_rope(qk_heads(q_l, da_q_norm), cos, sin)
    k_l = apply_rope(qk_heads(k_l, da_k_norm), cos, sin)
    k_c = qk_heads(k_c, da_k_norm)
    v_c = v_heads(v_c)
    k_all = jnp.concatenate([k_l, k_c], axis=1)
    v_all = jnp.concatenate([v_heads(v_l), v_c], axis=1)
    y_da_l = da_finish(latent_attention(q_l, k_all, v_all, lam), da_subln_g, lam_init, p_lat.dtype)

    out_l = merge_branches((y_hy_l, y_ssd_l, y_da_l), gate_l, w_branch, w_out)
    if not need_ctx:
        return out_l, None
    y_hy_c = hyena_seq(hy_c, *hy_args)
    y_ssd_c = ssd_finish(yc_dirs[0] + yc_dirs[1], xs_c, z_c, ssd_d, ssd_norm_g)
    q_c = qk_heads(da_c[..., :DA_QK], da_q_norm)
    y_da_c = da_finish(diff_attend(q_c, k_c, v_c, lam), da_subln_g, lam_init, p_ctx.dtype)
    out_c = merge_branches((y_hy_c, y_ssd_c, y_da_c), gate_c, w_branch, w_out)
    return out_l, out_c


def setup_inputs(seed: int = 0) -> dict:
    key = jax.random.key(seed)
    ks = iter(jax.random.split(key, 64))
    f32 = jnp.float32

    def nrm(shape, scale):
        return jax.random.normal(next(ks), shape, f32) * scale

    def gain(shape):
        return 1.0 + nrm(shape, 0.02)

    L, D = DEPTH, D_MODEL
    n_dense = (DEPTH + 1) // 2
    n_moe = DEPTH // 2
    dt0 = jnp.exp(jax.random.uniform(next(ks), (L, 2, SSD_HEADS), f32, math.log(1e-3), math.log(1e-1)))
    dt_bias = dt0 + jnp.log(-jnp.expm1(-dt0))
    a_log = jnp.log(jax.random.uniform(next(ks), (L, 2, SSD_HEADS), f32, 1.0, 16.0))
    return {
        'x': nrm((BATCH, SEQ, D), 1.0),
        'c': nrm((BATCH, D), 1.0),
        'ctx': nrm((BATCH, CTX_LEN, D), 1.0),
        'c_ctx': nrm((D,), 1.0),
        'w_mod': nrm((L, D, 6 * D), D ** -0.5),
        'b_mod': nrm((L, 6 * D), 0.01),
        'norm1_g': gain((L, D)),
        'norm2_g': gain((L, D)),
        'w_in': nrm((L, D, IN_COLS), D ** -0.5),
        'hy_conv_w': nrm((L, HY_SHORT_CONV, HY_COLS), HY_SHORT_CONV ** -0.5),
        'hy_conv_b': nrm((L, HY_COLS), 0.01),
        'hy_w1': nrm((L, HY_EMB, HY_FFN), HY_EMB ** -0.5),
        'hy_b1': nrm((L, HY_FFN), 0.1),
        'hy_w2': nrm((L, HY_FFN, HY_FFN), HY_FFN ** -0.5),
        'hy_b2': nrm((L, HY_FFN), 0.1),
        'hy_w3': nrm((L, HY_FFN, HY_FFN), HY_FFN ** -0.5),
        'hy_b3': nrm((L, HY_FFN), 0.1),
        'hy_w4': nrm((L, HY_FFN, HY_ORDER * 2 * HY_WIDTH), 0.1 * HY_FFN ** -0.5),
        'hy_freq': gain((L, 3, HY_FFN)),
        'hy_bias': nrm((L, HY_ORDER, HY_WIDTH), 0.1),
        'ssd_conv_w': nrm((L, SSD_SHORT_CONV, SSD_XBC), SSD_SHORT_CONV ** -0.5),
        'ssd_conv_b': nrm((L, SSD_XBC), 0.01),
        'ssd_dt_bias': dt_bias,
        'ssd_a_log': a_log,
        'ssd_d': gain((L, SSD_HEADS)),
        'ssd_norm_g': gain((L, SSD_INNER)),
        'da_q_norm': gain((L, DA_HEAD_DIM)),
        'da_k_norm': gain((L, DA_HEAD_DIM)),
        'da_lambda': nrm((L, 4, DA_HEAD_DIM), 0.1),
        'da_subln_g': gain((L, DA_V_DIM)),
        'w_branch': nrm((L, N_BRANCH, D, D), D ** -0.5),
        'w_out': nrm((L, D, D), D ** -0.5),
        'ffn_w1': nrm((n_dense, D, DENSE_FF), D ** -0.5),
        'ffn_w3': nrm((n_dense, D, DENSE_FF), D ** -0.5),
        'ffn_w2': nrm((n_dense, DENSE_FF, D), DENSE_FF ** -0.5),
        'router_w': nrm((n_moe, D, N_EXPERTS), D ** -0.5),
        'moe_w1': nrm((n_moe, N_EXPERTS, D, EXPERT_FF), D ** -0.5),
        'moe_w3': nrm((n_moe, N_EXPERTS, D, EXPERT_FF), D ** -0.5),
        'moe_w2': nrm((n_moe, N_EXPERTS, EXPERT_FF, D), EXPERT_FF ** -0.5),
    }


def reference(x, c, ctx, c_ctx, w_mod, b_mod, norm1_g, norm2_g, w_in, hy_conv_w, hy_conv_b, hy_w1, hy_b1, hy_w2,
              hy_b2, hy_w3, hy_b3, hy_w4, hy_freq, hy_bias, ssd_conv_w, ssd_conv_b, ssd_dt_bias, ssd_a_log, ssd_d,
              ssd_norm_g, da_q_norm, da_k_norm, da_lambda, da_subln_g, w_branch, w_out, ffn_w1, ffn_w3, ffn_w2,
              router_w, moe_w1, moe_w3, moe_w2):
    h_lat, h_ctx = x, ctx
    for i in range(DEPTH):
        need_ctx = i < DEPTH - 1
        mod_l = (jax.nn.silu(c) @ w_mod[i] + b_mod[i])[:, None, :]
        mod_c = jax.nn.silu(c_ctx) @ w_mod[i] + b_mod[i]
        sh1_l, sc1_l, g1_l, sh2_l, sc2_l, g2_l = jnp.split(mod_l, 6, axis=-1)
        sh1_c, sc1_c, g1_c, sh2_c, sc2_c, g2_c = jnp.split(mod_c, 6, axis=-1)

        p_lat = modulate(rmsnorm(h_lat, norm1_g[i]), sh1_l, sc1_l) @ w_in[i]
        p_ctx = modulate(rmsnorm(h_ctx, norm1_g[i]), sh1_c, sc1_c) @ w_in[i]
        m_lat, m_ctx = token_mixer(p_lat, p_ctx, hy_conv_w[i], hy_conv_b[i], hy_w1[i], hy_b1[i], hy_w2[i], hy_b2[i],
                                   hy_w3[i], hy_b3[i], hy_w4[i], hy_freq[i], hy_bias[i], ssd_conv_w[i], ssd_conv_b[i],
                                   ssd_dt_bias[i], ssd_a_log[i], ssd_d[i], ssd_norm_g[i], da_q_norm[i], da_k_norm[i],
                                   da_lambda[i], da_subln_g[i], w_branch[i], w_out[i], i + 1, need_ctx)
        h_lat = h_lat + g1_l * m_lat
        f_in = modulate(rmsnorm(h_lat, norm2_g[i]), sh2_l, sc2_l)
        if need_ctx:
            h_ctx = h_ctx + g1_c * m_ctx
            f_in = jnp.concatenate([modulate(rmsnorm(h_ctx, norm2_g[i]), sh2_c, sc2_c), f_in], axis=1)
        j = i // 2
        if i % 2 == 0:
            f_out = swiglu(f_in, ffn_w1[j], ffn_w3[j], ffn_w2[j])
        else:
            f_out = moe_swiglu(f_in, router_w[j], moe_w1[j], moe_w3[j], moe_w2[j])
        n_c = f_in.shape[1] - h_lat.shape[1]
        h_lat = h_lat + g2_l * f_out[:, n_c:]
        if need_ctx:
            h_ctx = h_ctx + g2_c * f_out[:, :n_c]
    return h_lat
```

```python
import math
from functools import partial

import numpy as np
import jax
import jax.numpy as jnp
from jax import lax
from jax.experimental import pallas as pl
from jax.experimental.pallas import tpu as pltpu

F32, BF16 = jnp.float32, jnp.bfloat16
HIGHEST = lax.Precision.HIGHEST

D = 1024
EPS = 1e-6
GRID_W = 64
ROW_TILE = 256
LANES = 128
SSD_CHUNK = 128
SSD_HEADS = 16
SSD_HEADDIM = 64
DA_HEADS = 8
DA_HEAD_DIM = 64
N_EXPERTS = 8
HY_BANDS = 16
HY_EMB = 2 * HY_BANDS + 1
HY_FFN = 64

_PCOL_HY = 0
_PCOL_GATE = 3072
_PCOL_DA = 6144
_PCOL_Z = 9216
_PCOL_XBC = 10240
_PCOL_DT = 11776
_PCOLS = 12288
_VMEM_LIMIT = 56 * 1024 * 1024


def _cp(sem, vmem=None):
    return pltpu.CompilerParams(dimension_semantics=sem, vmem_limit_bytes=vmem)


def _dot(a, b, prec=None):
    return jnp.dot(a, b, preferred_element_type=F32, precision=prec)


def _silu(x):
    return x * jax.nn.sigmoid(x)


def _mod_kernel(c_ref, w_ref, b_ref, o_ref):
    o_ref[...] = _dot(_silu(c_ref[...]), w_ref[...], HIGHEST) + b_ref[...]


def _mod(cvec, w, b):
    return pl.pallas_call(
        _mod_kernel, grid=(6,),
        in_specs=[pl.BlockSpec((8, D), lambda j: (0, 0)), pl.BlockSpec((D, D), lambda j: (0, j)),
                  pl.BlockSpec((1, D), lambda j: (0, j))],
        out_specs=pl.BlockSpec((8, D), lambda j: (0, j)),
        out_shape=jax.ShapeDtypeStruct((8, 6 * D), F32), name="mod")(cvec, w, b)


def _modrow(t, tpb):
    return jnp.where(t % tpb == 0, 0, 1 + t // tpb)


def _mod_spec(k, tpb):
    return pl.BlockSpec((None, 1, D), lambda t: (_modrow(t, tpb) * 6 + k, 0, 0))


def _normmod_kernel(h_ref, g_ref, sh_ref, sc_ref, o_ref):
    x = h_ref[...]
    y = x * lax.rsqrt(jnp.mean(x * x, axis=-1, keepdims=True) + EPS) * g_ref[...]
    o_ref[...] = (y * (1.0 + sc_ref[...]) + sh_ref[...]).astype(o_ref.dtype)


def _normmod(h, g, mod3, tpb):
    rows = h.shape[0]
    return pl.pallas_call(
        _normmod_kernel, grid=(rows // ROW_TILE,),
        in_specs=[pl.BlockSpec((ROW_TILE, D), lambda t: (t, 0)), pl.BlockSpec((1, D), lambda t: (0, 0)),
                  _mod_spec(0, tpb), _mod_spec(1, tpb)],
        out_specs=pl.BlockSpec((ROW_TILE, D), lambda t: (t, 0)),
        out_shape=jax.ShapeDtypeStruct((rows, D), BF16), name="normmod",
        compiler_params=_cp(("parallel",)))(h, g, mod3, mod3)


def _mm_kernel(a_ref, b_ref, o_ref):
    o_ref[...] = _dot(a_ref[...], b_ref[...]).astype(o_ref.dtype)


def _matmul(a, b, tm, tn, out_dtype):
    m, k = a.shape
    n = b.shape[1]
    return pl.pallas_call(
        _mm_kernel, grid=(n // tn, m // tm),
        in_specs=[pl.BlockSpec((tm, k), lambda j, i: (i, 0)), pl.BlockSpec((k, tn), lambda j, i: (0, j))],
        out_specs=pl.BlockSpec((tm, tn), lambda j, i: (i, j)),
        out_shape=jax.ShapeDtypeStruct((m, n), out_dtype), name="inproj",
        compiler_params=_cp(("parallel", "parallel"), _VMEM_LIMIT))(a, b)


def _hyfilt_kernel(f_ref, w1, b1, w2, b2, w3, b3, w4, fr_ref, dl_ref, o_ref, *, shift):
    f = f_ref[...]
    fr = fr_ref[...]
    h = jnp.sin(fr[0:1] * (_dot(f, w1[...], HIGHEST) + b1[...]))
    h = jnp.sin(fr[1:2] * (_dot(h, w2[...], HIGHEST) + b2[...]))
    h = jnp.sin(fr[2:3] * (_dot(h, w3[...], HIGHEST) + b3[...]))
    k = _dot(h, w4[...], HIGHEST)
    o_ref[...] = k * (jnp.exp(-f[:, 0:1] * dl_ref[...]) + shift)


def _hy_filters(L, w1, b1, w2, b2, w3, b3, w4, freq):
    t = jnp.linspace(0.0, 1.0, L, dtype=F32)[:, None]
    w = 2.0 * math.pi * jnp.arange(L, dtype=F32)[:, None] / L
    f = jnp.linspace(1e-4, HY_BANDS - 1, HY_BANDS, dtype=F32)[None, :]
    feats = jnp.concatenate([t, jnp.cos(f * w), -jnp.sin(f * w)], axis=-1)
    feats = jnp.pad(feats, ((0, 0), (0, HY_FFN - HY_EMB)))
    w1p = jnp.pad(w1, ((0, HY_FFN - HY_EMB), (0, 0)))
    max_decay = math.log(1e-2) / 0.3
    min_decay = math.log(1e-2) / 1.5
    deltas = jnp.abs(jnp.linspace(min_decay, max_decay, D, dtype=F32))[None, :]
    tl = min(L, 512)
    full = lambda r, c: pl.BlockSpec((r, c), lambda i, j: (0, 0))
    return pl.pallas_call(
        partial(_hyfilt_kernel, shift=0.05), grid=(L // tl, 4),
        in_specs=[pl.BlockSpec((tl, HY_FFN), lambda i, j: (i, 0)), full(HY_FFN, HY_FFN), full(1, HY_FFN),
                  full(HY_FFN, HY_FFN), full(1, HY_FFN), full(HY_FFN, HY_FFN), full(1, HY_FFN),
                  pl.BlockSpec((HY_FFN, D), lambda i, j: (0, j)), full(3, HY_FFN), full(1, D)],
        out_specs=pl.BlockSpec((tl, D), lambda i, j: (i, j)),
        out_shape=jax.ShapeDtypeStruct((L, 4 * D), F32), name="hyfilt",
        compiler_params=_cp(("parallel", "parallel")))(
            feats, w1p, b1[None], w2, b2[None], w3, b3[None], w4, freq, deltas)


def _fft_split(L):
    n = 2 * L
    n2 = 128 if L >= 2048 else n // 32
    return n // n2, n2


def _fft_tables(L):
    n1_, n2_ = _fft_split(L)
    n, h1 = 2 * L, n1_ // 2
    n2 = np.arange(n2_)[:, None, None]
    k1 = np.arange(n1_)[None, :, None]
    n1 = np.arange(h1)[None, None, :]
    g = np.exp(-2j * np.pi * (n2 * k1 / n + n1 * k1 / n1_))
    ma = np.concatenate([np.concatenate([g.real, -g.imag], 2), np.concatenate([g.imag, g.real], 2)], 1)
    kk = np.arange(n2_)
    f = np.exp(-2j * np.pi * np.outer(kk, kk) / n2_)
    mc = np.block([[f.real, -f.imag], [f.imag, f.real]])
    mci = np.block([[f.real, f.imag], [-f.imag, f.real]])
    gi = np.exp(2j * np.pi * (n2 * np.arange(n1_)[None, None, :] / n
                              + np.arange(h1)[None, :, None] * np.arange(n1_)[None, None, :] / n1_)) / n
    mai = np.concatenate([np.concatenate([gi.real, -gi.imag], 2), np.concatenate([gi.imag, gi.real], 2)], 1)
    cast = lambda a: jnp.asarray(a, F32).astype(BF16)
    return cast(ma), cast(ma[:, :, :h1]), cast(mc), cast(mci), cast(mai)


def _kfft_kernel(kf_ref, kb_ref, bias_ref, mar_ref, mc_ref, o_ref, af_ref, ab_ref, *, N1, N2):
    H1, PA = N1 // 2, 2 * N1 + 8

    def stage_a(n2, c):
        for src, dst in ((kf_ref, af_ref), (kb_ref, ab_ref)):
            slab = src[pl.ds(n2, H1, stride=N2), :].astype(BF16)
            dst[pl.ds(pl.multiple_of(n2 * PA, 8), 2 * N1), :] = _dot(mar_ref[n2], slab)
        return c

    lax.fori_loop(0, N2, stage_a, 0)
    bias = bias_ref[...]

    def stage_c(k1, c):
        def spec(ref):
            st = jnp.concatenate([ref[pl.ds(k1, N2, stride=PA), :], ref[pl.ds(N1 + k1, N2, stride=PA), :]], 0)
            return _dot(mc_ref[...], st.astype(BF16))
        xf, xb = spec(af_ref), spec(ab_ref)
        o_ref[k1] = jnp.concatenate([xf[:N2] + xb[:N2] + bias, xf[N2:] - xb[N2:]], 0).astype(o_ref.dtype)
        return c

    lax.fori_loop(0, N1, stage_c, 0)


def _hy_kfft(k4, bias, tabs, L):
    N1, N2 = _fft_split(L)
    _, mar, mc, _, _ = tabs
    nb = D // LANES
    return pl.pallas_call(
        partial(_kfft_kernel, N1=N1, N2=N2), grid=(2, nb),
        in_specs=[pl.BlockSpec((L, LANES), lambda o, j: (0, o * 2 * nb + j)),
                  pl.BlockSpec((L, LANES), lambda o, j: (0, o * 2 * nb + nb + j)),
                  pl.BlockSpec((None, 1, LANES), lambda o, j: (o, 0, j)),
                  pl.BlockSpec(mar.shape, lambda o, j: (0, 0, 0)), pl.BlockSpec(mc.shape, lambda o, j: (0, 0))],
        out_specs=pl.BlockSpec((None, N1, 2 * N2, LANES), lambda o, j: (o, 0, 0, j)),
        out_shape=jax.ShapeDtypeStruct((2, N1, 2 * N2, D), BF16),
        scratch_shapes=[pltpu.VMEM((N2 * (2 * N1 + 8), LANES), F32)] * 2, name="hykfft",
        compiler_params=_cp(("parallel", "parallel"), _VMEM_LIMIT))(k4, k4, bias[:, None, :], mar, mc)


def _hyconv_kernel(sig_ref, gate_ref, cws_ref, cbs_ref, cwg_ref, cbg_ref, kh_ref, ma_ref, mc_ref, mci_ref,
                   mai_ref, *rest, L, N1, N2, in_row0, out_row0, sig_conv, aliased):
    if aliased:
        rest = rest[1:]
    o_ref, nat0, nat1, s_ref, g_ref, a_ref = rest
    H1, PA = N1 // 2, 2 * N1 + 8
    nats = (nat0, nat1)
    CH = min(L, 512)

    def stage(ref, bb, r0):
        nat = nats[bb]
        nat[0:8, :] = jnp.zeros((8, LANES), F32)
        nat[8 + L:16 + L, :] = jnp.zeros((8, LANES), F32)
        for i in range(L // CH):
            nat[8 + i * CH:8 + (i + 1) * CH, :] = ref[bb, r0 + i * CH:r0 + (i + 1) * CH, :].astype(F32)

    def permute(bb, dest, w_ref, b_ref):
        nat = nats[bb]
        if w_ref is not None:
            w = w_ref[...]
            b = b_ref[...]

        def body(n2, c):
            u = nat[pl.ds(8 + n2, H1, stride=N2), :]
            if w_ref is not None:
                u = (w[0:1] * nat[pl.ds(7 + n2, H1, stride=N2), :] + w[1:2] * u
                     + w[2:3] * nat[pl.ds(9 + n2, H1, stride=N2), :] + b)
            dest[n2, bb * H1:(bb + 1) * H1, :] = u.astype(dest.dtype)
            return c

        lax.fori_loop(0, N2, body, 0)

    for bb in range(2):
        stage(sig_ref, bb, in_row0 if sig_conv else 0)
        permute(bb, s_ref, cws_ref if sig_conv else None, cbs_ref)
    for bb in range(2):
        stage(gate_ref, bb, in_row0)
        permute(bb, g_ref, cwg_ref, cbg_ref)

    def stage_a(n2, c):
        a_ref[pl.ds(pl.multiple_of(n2 * PA, 8), 2 * N1), :] = _dot(ma_ref[n2], s_ref[n2])
        return c

    lax.fori_loop(0, N2, stage_a, 0)

    def stage_c(k1, c):
        st = jnp.concatenate([a_ref[pl.ds(k1, N2, stride=PA), :], a_ref[pl.ds(N1 + k1, N2, stride=PA), :]], 0)
        x = _dot(mc_ref[...], st.astype(BF16))
        kh = kh_ref[k1].astype(F32)
        xr, xi, kr, ki = x[:N2], x[N2:], kh[:N2], kh[N2:]
        y = jnp.concatenate([xr * kr - xi * ki, xr * ki + xi * kr], 0)
        cc = _dot(mci_ref[...], y.astype(BF16))
        a_ref[pl.ds(k1, N2, stride=PA), :] = cc[:N2]
        a_ref[pl.ds(N1 + k1, N2, stride=PA), :] = cc[N2:]
        return c

    lax.fori_loop(0, N1, stage_c, 0)

    def stage_ai(n2, c):
        st = a_ref[pl.ds(pl.multiple_of(n2 * PA, 8), 2 * N1), :]
        r = _dot(mai_ref[n2], st.astype(BF16)) * g_ref[n2].astype(F32)
        nat0[pl.ds(8 + n2, H1, stride=N2), :] = r[:H1]
        nat1[pl.ds(8 + n2, H1, stride=N2), :] = r[H1:]
        return c

    lax.fori_loop(0, N2, stage_ai, 0)

    for bb in range(2):
        if out_row0:
            o_ref[bb, 0:out_row0, :] = jnp.zeros((out_row0, LANES), o_ref.dtype)
        for i in range(L // CH):
            o_ref[bb, out_row0 + i * CH:out_row0 + (i + 1) * CH, :] = (
                nats[bb][8 + i * CH:8 + (i + 1) * CH, :].astype(o_ref.dtype))


def _hy_conv(sig, sig_col, gate, gate_col, cw, cb, kh, order, tabs, L, in_row0, out_rows, out_row0, sig_conv,
             alias_into=None):
    N1, N2 = _fft_split(L)
    ma, _, mc, mci, mai = tabs
    H1 = N1 // 2
    B = gate.shape[0]
    nb = D // LANES
    scol = sig_col if sig_conv else 0
    cspec = lambda r, c0: pl.BlockSpec((r, LANES), lambda j, q: (0, c0 + j))
    const = lambda a: pl.BlockSpec(a.shape, lambda j, q: (0,) * a.ndim)
    in_specs = [pl.BlockSpec((2, sig.shape[1], LANES), lambda j, q: (q, 0, sig_col + j)),
                pl.BlockSpec((2, gate.shape[1], LANES), lambda j, q: (q, 0, gate_col + j)),
                cspec(3, scol), cspec(1, scol), cspec(3, gate_col), cspec(1, gate_col),
                pl.BlockSpec((None, N1, 2 * N2, LANES), lambda j, q: (order, 0, 0, j)),
                const(ma), const(mc), const(mci), const(mai)]
    args = [sig, gate, cw, cb, cw, cb, kh, ma, mc, mci, mai]
    aliases = {}
    if alias_into is not None:
        in_specs.append(pl.BlockSpec(memory_space=pl.ANY))
        args.append(alias_into)
        aliases = {len(args) - 1: 0}
        out_total = alias_into.shape[1]
    else:
        out_total = out_row0 + out_rows
    return pl.pallas_call(
        partial(_hyconv_kernel, L=L, N1=N1, N2=N2, in_row0=in_row0, out_row0=out_row0, sig_conv=sig_conv,
                aliased=alias_into is not None),
        grid=(nb, B // 2), in_specs=in_specs,
        out_specs=pl.BlockSpec((2, out_row0 + out_rows, LANES), lambda j, q: (q, 0, j)),
        out_shape=jax.ShapeDtypeStruct((B, out_total, D), BF16),
        scratch_shapes=[pltpu.VMEM((L + 16, LANES), F32), pltpu.VMEM((L + 16, LANES), F32),
                        pltpu.VMEM((N2, 2 * H1, LANES), BF16), pltpu.VMEM((N2, 2 * H1, LANES), BF16),
                        pltpu.VMEM((N2 * (2 * N1 + 8), LANES), F32)],
        input_output_aliases=aliases, name="hyconv",
        compiler_params=_cp(("parallel", "parallel"), _VMEM_LIMIT))(*args)


def _ssdpre_kernel(x_ref, w_ref, b_ref, o_ref, nat, *, T, CTX):
    W = x_ref.shape[1]
    nat[0:8, :] = jnp.zeros((8, W), F32)
    nat[8 + T:16 + T, :] = jnp.zeros((8, W), F32)
    for i in range(T // ROW_TILE):
        nat[8 + i * ROW_TILE:8 + (i + 1) * ROW_TILE, :] = x_ref[i * ROW_TILE:(i + 1) * ROW_TILE, :].astype(F32)
    w = w_ref[...]
    b = b_ref[...]
    row = lax.broadcasted_iota(jnp.int32, (ROW_TILE, 1), 0)
    for i in range(T // ROW_TILE):
        r0 = 8 + i * ROW_TILE
        xm = nat[r0 - 1:r0 - 1 + ROW_TILE, :]
        xp = nat[r0 + 1:r0 + 1 + ROW_TILE, :]
        if i * ROW_TILE == CTX:
            xm = jnp.where(row == 0, 0.0, xm)
        if (i + 1) * ROW_TILE == CTX:
            xp = jnp.where(row == ROW_TILE - 1, 0.0, xp)
        u = w[0:1] * xm + w[1:2] * nat[r0:r0 + ROW_TILE, :] + w[2:3] * xp + b
        o_ref[i * ROW_TILE:(i + 1) * ROW_TILE, :] = _silu(u).astype(o_ref.dtype)


def _ssd_pre(p3, cw, cb, CTX):
    B, T, _ = p3.shape
    W = 256
    nblk = cw.shape[1] // W
    return pl.pallas_call(
        partial(_ssdpre_kernel, T=T, CTX=CTX), grid=(B, nblk),
        in_specs=[pl.BlockSpec((None, T, W), lambda b, j: (b, 0, _PCOL_XBC // W + j)),
                  pl.BlockSpec((3, W), lambda b, j: (0, j)), pl.BlockSpec((1, W), lambda b, j: (0, j))],
        out_specs=pl.BlockSpec((None, T, W), lambda b, j: (b, 0, j)),
        out_shape=jax.ShapeDtypeStruct((B, T, cw.shape[1]), BF16),
        scratch_shapes=[pltpu.VMEM((T + 16, W), F32)], name="ssdpre",
        compiler_params=_cp(("parallel", "parallel")))(p3, cw, cb)


def _ssdscan_kernel(xbc_ref, dt_ref, dtb_ref, a_ref, tri_ref, *rest, rev):
    if rev:
        o_ref, s_ref = rest
    else:
        z_ref, yr_ref, dsk_ref, ng_ref, o_ref, s_ref, ybuf = rest
    Q = SSD_CHUNK

    @pl.when(pl.program_id(1) == 0)
    def _():
        s_ref[...] = jnp.zeros(s_ref.shape, F32)

    x = xbc_ref[...]
    xs, Bm, Cm = x[:, :D], x[:, D:D + 2 * LANES], x[:, D + 2 * LANES:]
    dtr = dt_ref[...].astype(F32) + dtb_ref[...]
    dt = jnp.maximum(dtr, 0.0) + jnp.log(1.0 + jnp.exp(-jnp.abs(dtr)))
    tri = tri_ref[...]
    mask = tri > 0.5
    cs = _dot(tri, dt * a_ref[...], HIGHEST)
    csT = cs.T
    dtT = dt.T
    last = 0 if rev else Q - 1
    col0 = SSD_HEADS if rev else 0
    lane_lo = lax.broadcasted_iota(jnp.int32, (1, LANES), 1) < SSD_HEADDIM
    for g in range(2):
        Bg = Bm[:, g * LANES:(g + 1) * LANES]
        Cg = Cm[:, g * LANES:(g + 1) * LANES]
        CB = lax.dot_general(Cg, Bg, (((1,), (1,)), ((), ())), preferred_element_type=F32)
        BgT = Bg.astype(F32).T
        Cgf = Cg.astype(F32)
        for pr in range(4):
            hp = g * 4 + pr
            xs_pair = xs[:, hp * LANES:(hp + 1) * LANES]
            Sp = s_ref[hp]
            rhs = jnp.concatenate([xs_pair, Sp.astype(BF16)], 0)
            ys, sts = [], []
            for hh in range(2):
                col = col0 + 2 * hp + hh
                csl = jnp.broadcast_to(cs[:, col:col + 1], (Q, Q))
                csr = csT[col:col + 1, :]
                dtr_row = dtT[col:col + 1, :]
                Lm = jnp.exp(jnp.where(mask, csl - csr, -1e30))
                lhs = jnp.concatenate([CB * Lm * dtr_row, Cgf * jnp.exp(csl)], 1).astype(BF16)
                ys.append(_dot(lhs, rhs))
                tot = csT[col:col + 1, last:last + 1]
                w_row = jnp.exp(tot - csr) * dtr_row
                st = _dot((BgT * w_row).astype(BF16), xs_pair)
                sts.append(jnp.exp(tot) * Sp + st)
            y_pair = jnp.where(lane_lo, ys[0], ys[1])
            s_ref[hp] = jnp.where(lane_lo, sts[0], sts[1])
            sl = slice(hp * LANES, (hp + 1) * LANES)
            if rev:
                o_ref[:, sl] = y_pair.astype(o_ref.dtype)
            else:
                ybuf[:, sl] = y_pair + yr_ref[:, sl].astype(F32) + xs_pair.astype(F32) * dsk_ref[:, sl]
    if not rev:
        y = ybuf[...] * _silu(z_ref[...].astype(F32))
        half = D // 2
        parts = []
        for g in range(2):
            yg = y[:, g * half:(g + 1) * half]
            parts.append(yg * lax.rsqrt(jnp.mean(yg * yg, axis=-1, keepdims=True) + EPS))
        o_ref[...] = (jnp.concatenate(parts, 1) * ng_ref[...]).astype(o_ref.dtype)


def _ssd_scan(xbc, p3, dtb, arow, rev, CTX, fin=None):
    B, T, _ = xbc.shape
    Q = SSD_CHUNK
    nch, nc = T // Q, CTX // Q
    if rev:
        chunk = lambda s: jnp.where(s < nc, nc - 1 - s, nch - 1 + nc - s)
        tri = jnp.asarray(np.triu(np.ones((Q, Q), np.float32)))
    else:
        chunk = lambda s: s
        tri = jnp.asarray(np.tril(np.ones((Q, Q), np.float32)))
    row = lambda w: pl.BlockSpec((1, w), lambda b, s: (0, 0))
    in_specs = [pl.BlockSpec((None, Q, xbc.shape[2]), lambda b, s: (b, chunk(s), 0)),
                pl.BlockSpec((None, Q, LANES), lambda b, s: (b, chunk(s), _PCOL_DT // LANES)),
                row(LANES), row(LANES), pl.BlockSpec((Q, Q), lambda b, s: (0, 0))]
    args = [xbc, p3, dtb, arow, tri]
    scratch = [pltpu.VMEM((SSD_HEADS // 2, LANES, LANES), F32)]
    if not rev:
        yrev, dskip, ng = fin
        in_specs += [pl.BlockSpec((None, Q, D), lambda b, s: (b, s, _PCOL_Z // D)),
                     pl.BlockSpec((None, Q, D), lambda b, s: (b, s, 0)), row(D), row(D)]
        args += [p3, yrev, dskip, ng]
        scratch.append(pltpu.VMEM((Q, D), F32))
    return pl.pallas_call(
        partial(_ssdscan_kernel, rev=rev), grid=(B, nch), in_specs=in_specs,
        out_specs=pl.BlockSpec((None, Q, D), lambda b, s: (b, chunk(s), 0)),
        out_shape=jax.ShapeDtypeStruct((B, T, D), BF16), scratch_shapes=scratch,
        name="ssdrev" if rev else "ssdfwd",
        compiler_params=_cp(("parallel", "arbitrary")))(*args)


def _dapre_kernel(q_ref, k_ref, gq_ref, gk_ref, cos_ref, sin_ref, ones_ref, qo_ref, ko_ref):
    ones = ones_ref[...]
    cosf = jnp.concatenate([cos_ref[...]] * (D // LANES), 1)
    sinf = jnp.concatenate([sin_ref[...]] * (D // LANES), 1)
    lane = lax.broadcasted_iota(jnp.int32, (1, D), 1)
    first_half = (lane % 32) < 16

    def norm_rope(x_ref, g_ref, scale):
        x = x_ref[...].astype(F32)
        sq = x * x
        hi = sq.astype(BF16)
        lo = (sq - hi.astype(F32)).astype(BF16)
        parts = []
        for blk in range(D // 256):
            sl = slice(blk * 256, (blk + 1) * 256)
            parts.append(_dot(hi[:, sl], ones) + _dot(lo[:, sl], ones))
        ss = jnp.concatenate(parts, 1)
        y = x * lax.rsqrt(ss * (1.0 / DA_HEAD_DIM) + EPS) * g_ref[...]
        partner = jnp.where(first_half, pltpu.roll(y, D - 16, 1), pltpu.roll(y, 16, 1))
        return ((y * cosf + partner * sinf) * scale).astype(BF16)

    qo_ref[...] = norm_rope(q_ref, gq_ref, DA_HEAD_DIM ** -0.5)
    ko_ref[...] = norm_rope(k_ref, gk_ref, 1.0)


def _rope_tables(L, CTX):
    rows = L // GRID_W
    r = np.repeat(np.arange(rows), GRID_W)
    c = np.tile(np.arange(GRID_W), rows)
    nf = DA_HEAD_DIM // 4
    inv = jnp.asarray(10000.0, F32) ** (-jnp.arange(nf, dtype=F32) / nf)
    ang = jnp.stack([jnp.asarray(r), jnp.asarray(c)], -1).astype(F32)[:, :, None] * inv
    cos, sin = jnp.cos(ang), jnp.sin(ang)
    cos64 = jnp.stack([cos, cos], 2).reshape(L, DA_HEAD_DIM)
    sin64 = jnp.stack([-sin, sin], 2).reshape(L, DA_HEAD_DIM)
    cos_t = jnp.concatenate([jnp.ones((CTX, DA_HEAD_DIM), F32), cos64], 0)
    sin_t = jnp.concatenate([jnp.zeros((CTX, DA_HEAD_DIM), F32), sin64], 0)
    return jnp.tile(cos_t, (1, 2)), jnp.tile(sin_t, (1, 2))


def _da_pre(p2, gq, gk, cos_t, sin_t, tpb):
    rows = p2.shape[0]
    ones = jnp.asarray(np.kron(np.eye(256 // DA_HEAD_DIM), np.ones((DA_HEAD_DIM, DA_HEAD_DIM))), BF16)
    tab = pl.BlockSpec((ROW_TILE, LANES), lambda t: (t % tpb, 0))
    row = pl.BlockSpec((1, D), lambda t: (0, 0))
    out = pl.BlockSpec((ROW_TILE, D), lambda t: (t, 0))
    return pl.pallas_call(
        _dapre_kernel, grid=(rows // ROW_TILE,),
        in_specs=[pl.BlockSpec((ROW_TILE, D), lambda t: (t, _PCOL_DA // D)),
                  pl.BlockSpec((ROW_TILE, D), lambda t: (t, _PCOL_DA // D + 1)), row, row, tab, tab,
                  pl.BlockSpec((256, 256), lambda t: (0, 0))],
        out_specs=[out, out],
        out_shape=[jax.ShapeDtypeStruct((rows, D), BF16)] * 2, name="dapre",
        compiler_params=_cp(("parallel",)))(p2, p2, gq, gk, cos_t, sin_t, ones)


def _attn_kernel(lam_ref, q_ref, k_ref, v_ref, g_ref, *rest, lam_init, aliased):
    o_ref = rest[-1]
    lv = lam_ref[...]
    lam = (jnp.exp(jnp.sum(lv[0:1] * lv[1:2], keepdims=True)) - jnp.exp(jnp.sum(lv[2:3] * lv[3:4], keepdims=True))
           + lam_init)
    q, k, v = q_ref[...], k_ref[...], v_ref[...]
    lane_lo = lax.broadcasted_iota(jnp.int32, (1, LANES), 1) < DA_HEAD_DIM
    outs = []
    for comp in range(2):
        qm = jnp.where(lane_lo if comp == 0 else jnp.logical_not(lane_lo), q, jnp.zeros_like(q))
        s = lax.dot_general(qm, k, (((1,), (1,)), ((), ())), preferred_element_type=F32)
        e = jnp.exp(s - jnp.max(s, axis=1, keepdims=True))
        outs.append(_dot(e.astype(BF16), v) / jnp.sum(e, axis=1, keepdims=True))
    o = outs[0] - lam * outs[1]
    y = o * lax.rsqrt(jnp.mean(o * o, axis=-1, keepdims=True) + EPS) * g_ref[...] * (1.0 - lam_init)
    o_ref[...] = y.astype(o_ref.dtype)


def _da_attn(lam4, qn, kn, p3, subg, lam_init, q_row0, nq, kv_rows, alias_into=None):
    B, T, _ = qn.shape
    tq = ROW_TILE
    in_specs = [pl.BlockSpec((4, DA_HEAD_DIM), lambda b, h, i: (0, 0)),
                pl.BlockSpec((None, tq, LANES), lambda b, h, i: (b, q_row0 + i, h)),
                pl.BlockSpec((None, kv_rows, LANES), lambda b, h, i: (b, 0, h)),
                pl.BlockSpec((None, kv_rows, LANES), lambda b, h, i: (b, 0, (_PCOL_DA + 2 * D) // LANES + h)),
                pl.BlockSpec((1, LANES), lambda b, h, i: (0, 0))]
    args = [lam4, qn, kn, p3, subg]
    aliases = {}
    if alias_into is not None:
        in_specs.append(pl.BlockSpec(memory_space=pl.ANY))
        args.append(alias_into)
        aliases = {len(args) - 1: 0}
    return pl.pallas_call(
        partial(_attn_kernel, lam_init=lam_init, aliased=alias_into is not None),
        grid=(B, DA_HEADS, nq), in_specs=in_specs,
        out_specs=pl.BlockSpec((None, tq, LANES), lambda b, h, i: (b, q_row0 + i, h)),
        out_shape=jax.ShapeDtypeStruct((B, T, D), BF16), input_output_aliases=aliases, name="daattn",
        compiler_params=_cp(("parallel", "parallel", "parallel"), _VMEM_LIMIT))(*args)


def _merge_kernel(yh_ref, ys_ref, yd_ref, g0_ref, g1_ref, g2_ref, h_ref, wb_ref, wo_ref, gate_ref, n2_ref, sh_ref,
                  sc_ref, *rest, route):
    acc = None
    for i, (y_ref, gl_ref) in enumerate(((yh_ref, g0_ref), (ys_ref, g1_ref), (yd_ref, g2_ref))):
        term = jax.nn.sigmoid(gl_ref[...].astype(F32)) * _dot(y_ref[...], wb_ref[i])
        acc = term if acc is None else acc + term
    hn = h_ref[...] + gate_ref[...] * _dot(acc.astype(BF16), wo_ref[...])
    xn = hn * lax.rsqrt(jnp.mean(hn * hn, axis=-1, keepdims=True) + EPS) * n2_ref[...]
    fin = xn * (1.0 + sc_ref[...]) + sh_ref[...]
    if not route:
        hout_ref, fin_ref = rest
    else:
        rw_ref, hout_ref, fin_ref, rg_ref = rest
        logits = _dot(fin, rw_ref[...], HIGHEST)
        lane = lax.broadcasted_iota(jnp.int32, logits.shape, 1)
        logits = jnp.where(lane < N_EXPERTS, logits, -jnp.inf)
        m1 = jnp.max(logits, axis=1, keepdims=True)
        i1 = jnp.min(jnp.where(logits == m1, lane, LANES), axis=1, keepdims=True)
        rest_l = jnp.where(lane == i1, -jnp.inf, logits)
        m2 = jnp.max(rest_l, axis=1, keepdims=True)
        i2 = jnp.min(jnp.where(rest_l == m2, lane, LANES), axis=1, keepdims=True)
        e2 = jnp.exp(m2 - m1)
        rg_ref[...] = jnp.where(lane == i1, 1.0 / (1.0 + e2), 0.0) + jnp.where(lane == i2, e2 / (1.0 + e2), 0.0)
    hout_ref[...] = hn
    fin_ref[...] = fin.astype(fin_ref.dtype)


def _merge(yh, ys, yd, p2, h, wb, wo, mod3, n2g, tpb, latent_only, router_w=None):
    rows = h.shape[0]
    if latent_only:
        nt = rows // ROW_TILE // tpb * (tpb - 1)
        src = lambda t: t // (tpb - 1) * tpb + 1 + t % (tpb - 1)
    else:
        nt = rows // ROW_TILE
        src = lambda t: t
    tile = lambda cb: pl.BlockSpec((ROW_TILE, D), lambda t: (src(t), cb))
    mspec = lambda k: pl.BlockSpec((None, 1, D), lambda t: (_modrow(src(t), tpb) * 6 + k, 0, 0))
    in_specs = [tile(0), tile(0), tile(0), tile(_PCOL_GATE // D), tile(_PCOL_GATE // D + 1),
                tile(_PCOL_GATE // D + 2), tile(0), pl.BlockSpec((3, D, D), lambda t: (0, 0, 0)),
                pl.BlockSpec((D, D), lambda t: (0, 0)), mspec(2), pl.BlockSpec((1, D), lambda t: (0, 0)),
                mspec(3), mspec(4)]
    args = [yh, ys, yd, p2, p2, p2, h, wb, wo, mod3, n2g, mod3, mod3]
    otile = pl.BlockSpec((ROW_TILE, D), lambda t: (t, 0))
    out_specs = [otile, otile]
    out_shape = [jax.ShapeDtypeStruct((nt * ROW_TILE, D), F32), jax.ShapeDtypeStruct((nt * ROW_TILE, D), BF16)]
    if router_w is not None:
        in_specs.append(pl.BlockSpec((D, LANES), lambda t: (0, 0)))
        args.append(router_w)
        out_specs.append(pl.BlockSpec((ROW_TILE, LANES), lambda t: (t, 0)))
        out_shape.append(jax.ShapeDtypeStruct((nt * ROW_TILE, LANES), F32))
    return pl.pallas_call(
        partial(_merge_kernel, route=router_w is not None), grid=(nt,), in_specs=in_specs, out_specs=out_specs,
        out_shape=out_shape, name="merge", compiler_params=_cp(("parallel",), _VMEM_LIMIT))(*args)


def _ffn_kernel(x_ref, w1_ref, w3_ref, w2_ref, h_ref, gc_ref, gl_ref, o_ref, acc, *, ctx_rows, tiles_per_batch):
    f = pl.program_id(1)
    x = x_ref[...]
    a = (_silu(_dot(x, w1_ref[...])) * _dot(x, w3_ref[...])).astype(BF16)
    part = _dot(a, w2_ref[...])

    @pl.when(f == 0)
    def _():
        acc[...] = part

    @pl.when(f > 0)
    def _():
        acc[...] += part

    @pl.when(f == pl.num_programs(1) - 1)
    def _():
        tm = x_ref.shape[0]
        r = (pl.program_id(0) % tiles_per_batch) * tm + lax.broadcasted_iota(jnp.int32, (tm, 1), 0)
        gate = jnp.where(r < ctx_rows, gc_ref[...], gl_ref[...])
        o_ref[...] = h_ref[...] + gate * acc[...]


def _ffn(fin, w1, w3, w2, h, mod3, T, CTX):
    rows = h.shape[0]
    tpb = 4
    tm = T // tpb
    ff = w1.shape[1]
    tf = 512
    return pl.pallas_call(
        partial(_ffn_kernel, ctx_rows=CTX, tiles_per_batch=tpb), grid=(rows // tm, ff // tf),
        in_specs=[pl.BlockSpec((tm, D), lambda i, f: (i, 0)), pl.BlockSpec((D, tf), lambda i, f: (0, f)),
                  pl.BlockSpec((D, tf), lambda i, f: (0, f)), pl.BlockSpec((tf, D), lambda i, f: (f, 0)),
                  pl.BlockSpec((tm, D), lambda i, f: (i, 0)),
                  pl.BlockSpec((None, 1, D), lambda i, f: (5, 0, 0)),
                  pl.BlockSpec((None, 1, D), lambda i, f: ((1 + i // tpb) * 6 + 5, 0, 0))],
        out_specs=pl.BlockSpec((tm, D), lambda i, f: (i, 0)),
        out_shape=jax.ShapeDtypeStruct((rows, D), F32), scratch_shapes=[pltpu.VMEM((tm, D), F32)], name="ffn",
        compiler_params=_cp(("parallel", "arbitrary"), _VMEM_LIMIT))(fin, w1, w3, w2, h, mod3, mod3)


def _moe_kernel(x_ref, rg_ref, w1_ref, w3_ref, w2_ref, h_ref, gl_ref, o_ref, acc):
    e, f = pl.program_id(1), pl.program_id(2)
    x = x_ref[...]
    a = _silu(_dot(x, w1_ref[...])) * _dot(x, w3_ref[...]) * rg_ref[...]
    part = _dot(a.astype(BF16), w2_ref[...])
    first = jnp.logical_and(e == 0, f == 0)

    @pl.when(first)
    def _():
        acc[...] = part

    @pl.when(jnp.logical_not(first))
    def _():
        acc[...] += part

    @pl.when(jnp.logical_and(e == pl.num_programs(1) - 1, f == pl.num_programs(2) - 1))
    def _():
        o_ref[...] = h_ref[...] + gl_ref[...] * acc[...]


def _moe(fin, rgate, w1, w3, w2, h, mod3, SEQ):
    rows = h.shape[0]
    tm = min(SEQ, 1024)
    ff = w1.shape[2]
    tf = 512
    tiles_per_batch = SEQ // tm
    rg = rgate[:, :N_EXPERTS].T[:, :, None]
    return pl.pallas_call(
        _moe_kernel, grid=(rows // tm, N_EXPERTS, ff // tf),
        in_specs=[pl.BlockSpec((tm, D), lambda i, e, f: (i, 0)),
                  pl.BlockSpec((None, tm, 1), lambda i, e, f: (e, i, 0)),
                  pl.BlockSpec((None, D, tf), lambda i, e, f: (e, 0, f)),
                  pl.BlockSpec((None, D, tf), lambda i, e, f: (e, 0, f)),
                  pl.BlockSpec((None, tf, D), lambda i, e, f: (e, f, 0)),
                  pl.BlockSpec((tm, D), lambda i, e, f: (i, 0)),
                  pl.BlockSpec((None, 1, D), lambda i, e, f: ((1 + i // tiles_per_batch) * 6 + 5, 0, 0))],
        out_specs=pl.BlockSpec((tm, D), lambda i, e, f: (i, 0)),
        out_shape=jax.ShapeDtypeStruct((rows, D), F32), scratch_shapes=[pltpu.VMEM((tm, D), F32)], name="moe",
        compiler_params=_cp(("parallel", "arbitrary", "arbitrary"), _VMEM_LIMIT))(fin, rg, w1, w3, w2, h, mod3)


def _regroup_cols(w):
    hy, ssd, da, gate = 3072, 2592, 3072, 3072
    parts = [w[..., :hy], w[..., hy + ssd + da:hy + ssd + da + gate], w[..., hy + ssd:hy + ssd + da],
             w[..., hy:hy + ssd]]
    pad = jnp.zeros(w.shape[:-1] + (_PCOLS - hy - ssd - da - gate,), w.dtype)
    return jnp.concatenate(parts + [pad], axis=-1)


def kernel(x, c, ctx, c_ctx, w_mod, b_mod, norm1_g, norm2_g, w_in, hy_conv_w, hy_conv_b, hy_w1, hy_b1, hy_w2, hy_b2,
           hy_w3, hy_b3, hy_w4, hy_freq, hy_bias, ssd_conv_w, ssd_conv_b, ssd_dt_bias, ssd_a_log, ssd_d, ssd_norm_g,
           da_q_norm, da_k_norm, da_lambda, da_subln_g, w_branch, w_out, ffn_w1, ffn_w3, ffn_w2, router_w, moe_w1,
           moe_w3, moe_w2):
    B, SEQ, _ = x.shape
    CTX = ctx.shape[1]
    assert CTX == ROW_TILE and SEQ % ROW_TILE == 0 and B % 2 == 0 and B + 1 <= 8
    T = CTX + SEQ
    tpb = T // ROW_TILE
    depth = w_in.shape[0]
    R = B * T

    h = jnp.concatenate([ctx, x], axis=1).reshape(R, D)
    cvec = jnp.zeros((8, D), F32).at[0].set(c_ctx).at[1:1 + B].set(c)
    cos_t, sin_t = _rope_tables(SEQ, CTX)
    tabs_l, tabs_c = _fft_tables(SEQ), _fft_tables(CTX)
    nb = D // LANES

    for i in range(depth):
        need_ctx = i < depth - 1
        mod = _mod(cvec, w_mod[i], b_mod[i][None])
        mod3 = mod.reshape(8 * 6, 1, D)
        xn = _normmod(h, norm1_g[i][None], mod3, tpb)
        p2 = _matmul(xn, _regroup_cols(w_in[i]).astype(BF16), T // 4, 1024, BF16)
        p3 = p2.reshape(B, T, _PCOLS)

        cw, cb = hy_conv_w[i], hy_conv_b[i][None]
        hy_args = (hy_w1[i], hy_b1[i], hy_w2[i], hy_b2[i], hy_w3[i], hy_b3[i], hy_w4[i], hy_freq[i])
        kh = _hy_kfft(_hy_filters(SEQ, *hy_args), hy_bias[i], tabs_l, SEQ)
        z1 = _hy_conv(p3, 0, p3, nb, cw, cb, kh, 0, tabs_l, SEQ, CTX, SEQ, 0, True)
        y_hy = _hy_conv(z1, 0, p3, 2 * nb, cw, cb, kh, 1, tabs_l, SEQ, CTX, SEQ, CTX, False)
        if need_ctx:
            khc = _hy_kfft(_hy_filters(CTX, *hy_args), hy_bias[i], tabs_c, CTX)
            z1c = _hy_conv(p3, 0, p3, nb, cw, cb, khc, 0, tabs_c, CTX, 0, CTX, 0, True)
            y_hy = _hy_conv(z1c, 0, p3, 2 * nb, cw, cb, khc, 1, tabs_c, CTX, 0, CTX, 0, False, alias_into=y_hy)

        xbc = _ssd_pre(p3, ssd_conv_w[i], ssd_conv_b[i][None], CTX)
        dtb = jnp.pad(ssd_dt_bias[i].reshape(1, 2 * SSD_HEADS), ((0, 0), (0, LANES - 2 * SSD_HEADS)))
        arow = jnp.pad(-jnp.exp(ssd_a_log[i].astype(F32)).reshape(1, 2 * SSD_HEADS),
                       ((0, 0), (0, LANES - 2 * SSD_HEADS)))
        y_rev = _ssd_scan(xbc, p3, dtb, arow, True, CTX)
        dskip = jnp.repeat(ssd_d[i].astype(F32), SSD_HEADDIM)[None]
        y_ssd = _ssd_scan(xbc, p3, dtb, arow, False, CTX, fin=(y_rev, dskip, ssd_norm_g[i][None]))

        lam_init = 0.8 - 0.6 * math.exp(-0.3 * (i + 1))
        gq = jnp.tile(da_q_norm[i], D // DA_HEAD_DIM)[None]
        gk = jnp.tile(da_k_norm[i], D // DA_HEAD_DIM)[None]
        qn, kn = _da_pre(p2, gq, gk, cos_t, sin_t, tpb)
        qn, kn = qn.reshape(B, T, D), kn.reshape(B, T, D)
        subg = da_subln_g[i][None]
        y_da = _da_attn(da_lambda[i], qn, kn, p3, subg, lam_init, 1, tpb - 1, T)
        if need_ctx:
            y_da = _da_attn(da_lambda[i], qn, kn, p3, subg, lam_init, 0, 1, CTX, alias_into=y_da)

        wb, wo = w_branch[i].astype(BF16), w_out[i].astype(BF16)
        flat = lambda a: a.reshape(R, D)
        j = i // 2
        if i % 2 == 0:
            h, fin = _merge(flat(y_hy), flat(y_ssd), flat(y_da), p2, h, wb, wo, mod3, norm2_g[i][None], tpb,
                            latent_only=not need_ctx)
            assert need_ctx, "dense FFN layers are expected to carry context rows"
            h = _ffn(fin, ffn_w1[j].astype(BF16), ffn_w3[j].astype(BF16), ffn_w2[j].astype(BF16), h, mod3, T, CTX)
        else:
            assert not need_ctx, "expert layers are expected to be latent only"
            rw = jnp.pad(router_w[j], ((0, 0), (0, LANES - N_EXPERTS)))
            h, fin, rgate = _merge(flat(y_hy), flat(y_ssd), flat(y_da), p2, h, wb, wo, mod3, norm2_g[i][None], tpb,
                                   latent_only=True, router_w=rw)
            h = _moe(fin, rgate, moe_w1[j].astype(BF16), moe_w3[j].astype(BF16), moe_w2[j].astype(BF16), h, mod3,
                     SEQ)
    return h.reshape(B, SEQ, D)
```

```python
import math
from functools import partial

import numpy as np
import jax
import jax.numpy as jnp
from jax import lax
from jax.experimental import pallas as pl
from jax.experimental.pallas import tpu as pltpu

F32, BF16 = jnp.float32, jnp.bfloat16
HIGHEST = lax.Precision.HIGHEST

D = 1024
EPS = 1e-6
GRID_W = 64
ROW_TILE = 256
LANES = 128
SSD_CHUNK = 128
SSD_HEADS = 16
SSD_HEADDIM = 64
DA_HEADS = 8
DA_HEAD_DIM = 64
N_EXPERTS = 8
HY_BANDS = 16
HY_EMB = 2 * HY_BANDS + 1
HY_FFN = 64

_PCOL_HY = 0
_PCOL_GATE = 3072
_PCOL_DA = 6144
_PCOL_Z = 9216
_PCOL_XBC = 10240
_PCOL_DT = 11776
_PCOLS = 12288
_VMEM_LIMIT = 56 * 1024 * 1024


def _cp(sem, vmem=None):
    return pltpu.CompilerParams(dimension_semantics=sem, vmem_limit_bytes=vmem)


def _dot(a, b, prec=None):
    return jnp.dot(a, b, preferred_element_type=F32, precision=prec)


def _silu(x):
    return x * jax.nn.sigmoid(x)


def _mod_kernel(c_ref, w_ref, b_ref, o_ref):
    o_ref[...] = _dot(_silu(c_ref[...]), w_ref[...], HIGHEST) + b_ref[...]


def _mod(cvec, w, b):
    return pl.pallas_call(
        _mod_kernel, grid=(6,),
        in_specs=[pl.BlockSpec((8, D), lambda j: (0, 0)), pl.BlockSpec((D, D), lambda j: (0, j)),
                  pl.BlockSpec((1, D), lambda j: (0, j))],
        out_specs=pl.BlockSpec((8, D), lambda j: (0, j)),
        out_shape=jax.ShapeDtypeStruct((8, 6 * D), F32), name="mod")(cvec, w, b)


def _modrow(t, tpb):
    return jnp.where(t % tpb == 0, 0, 1 + t // tpb)


def _mod_spec(k, tpb):
    return pl.BlockSpec((None, 1, D), lambda t: (_modrow(t, tpb) * 6 + k, 0, 0))


def _normmod_kernel(h_ref, g_ref, sh_ref, sc_ref, o_ref):
    x = h_ref[...]
    y = x * lax.rsqrt(jnp.mean(x * x, axis=-1, keepdims=True) + EPS) * g_ref[...]
    o_ref[...] = (y * (1.0 + sc_ref[...]) + sh_ref[...]).astype(o_ref.dtype)


def _normmod(h, g, mod3, tpb):
    rows = h.shape[0]
    return pl.pallas_call(
        _normmod_kernel, grid=(rows // ROW_TILE,),
        in_specs=[pl.BlockSpec((ROW_TILE, D), lambda t: (t, 0)), pl.BlockSpec((1, D), lambda t: (0, 0)),
                  _mod_spec(0, tpb), _mod_spec(1, tpb)],
        out_specs=pl.BlockSpec((ROW_TILE, D), lambda t: (t, 0)),
        out_shape=jax.ShapeDtypeStruct((rows, D), BF16), name="normmod",
        compiler_params=_cp(("parallel",)))(h, g, mod3, mod3)


def _mm_kernel(a_ref, b_ref, o_ref):
    o_ref[...] = _dot(a_ref[...], b_ref[...]).astype(o_ref.dtype)


def _matmul(a, b, tm, tn, out_dtype):
    m, k = a.shape
    n = b.shape[1]
    return pl.pallas_call(
        _mm_kernel, grid=(n // tn, m // tm),
        in_specs=[pl.BlockSpec((tm, k), lambda j, i: (i, 0)), pl.BlockSpec((k, tn), lambda j, i: (0, j))],
        out_specs=pl.BlockSpec((tm, tn), lambda j, i: (i, j)),
        out_shape=jax.ShapeDtypeStruct((m, n), out_dtype), name="inproj",
        compiler_params=_cp(("parallel", "parallel"), _VMEM_LIMIT))(a, b)


def _hyfilt_kernel(f_ref, w1, b1, w2, b2, w3, b3, w4, fr_ref, dl_ref, o_ref, *, shift):
    f = f_ref[...]
    fr = fr_ref[...]
    h = jnp.sin(fr[0:1] * (_dot(f, w1[...], HIGHEST) + b1[...]))
    h = jnp.sin(fr[1:2] * (_dot(h, w2[...], HIGHEST) + b2[...]))
    h = jnp.sin(fr[2:3] * (_dot(h, w3[...], HIGHEST) + b3[...]))
    k = _dot(h, w4[...], HIGHEST)
    o_ref[...] = k * (jnp.exp(-f[:, 0:1] * dl_ref[...]) + shift)


def _hy_filters(L, w1, b1, w2, b2, w3, b3, w4, freq):
    t = jnp.linspace(0.0, 1.0, L, dtype=F32)[:, None]
    w = 2.0 * math.pi * jnp.arange(L, dtype=F32)[:, None] / L
    f = jnp.linspace(1e-4, HY_BANDS - 1, HY_BANDS, dtype=F32)[None, :]
    feats = jnp.concatenate([t, jnp.cos(f * w), -jnp.sin(f * w)], axis=-1)
    feats = jnp.pad(feats, ((0, 0), (0, HY_FFN - HY_EMB)))
    w1p = jnp.pad(w1, ((0, HY_FFN - HY_EMB), (0, 0)))
    max_decay = math.log(1e-2) / 0.3
    min_decay = math.log(1e-2) / 1.5
    deltas = jnp.abs(jnp.linspace(min_decay, max_decay, D, dtype=F32))[None, :]
    tl = min(L, 512)
    full = lambda r, c: pl.BlockSpec((r, c), lambda i, j: (0, 0))
    return pl.pallas_call(
        partial(_hyfilt_kernel, shift=0.05), grid=(L // tl, 4),
        in_specs=[pl.BlockSpec((tl, HY_FFN), lambda i, j: (i, 0)), full(HY_FFN, HY_FFN), full(1, HY_FFN),
                  full(HY_FFN, HY_FFN), full(1, HY_FFN), full(HY_FFN, HY_FFN), full(1, HY_FFN),
                  pl.BlockSpec((HY_FFN, D), lambda i, j: (0, j)), full(3, HY_FFN), full(1, D)],
        out_specs=pl.BlockSpec((tl, D), lambda i, j: (i, j)),
        out_shape=jax.ShapeDtypeStruct((L, 4 * D), F32), name="hyfilt",
        compiler_params=_cp(("parallel", "parallel")))(
            feats, w1p, b1[None], w2, b2[None], w3, b3[None], w4, freq, deltas)


def _fft_split(L):
    n = 2 * L
    n2 = 128 if L >= 2048 else n // 32
    return n // n2, n2


def _fft_tables(L):
    n1_, n2_ = _fft_split(L)
    n, h1 = 2 * L, n1_ // 2
    n2 = np.arange(n2_)[:, None, None]
    k1 = np.arange(n1_)[None, :, None]
    n1 = np.arange(h1)[None, None, :]
    g = np.exp(-2j * np.pi * (n2 * k1 / n + n1 * k1 / n1_))
    ma = np.concatenate([np.concatenate([g.real, -g.imag], 2), np.concatenate([g.imag, g.real], 2)], 1)
    kk = np.arange(n2_)
    f = np.exp(-2j * np.pi * np.outer(kk, kk) / n2_)
    mc = np.block([[f.real, -f.imag], [f.imag, f.real]])
    mci = np.block([[f.real, f.imag], [-f.imag, f.real]])
    gi = np.exp(2j * np.pi * (n2 * np.arange(n1_)[None, None, :] / n
                              + np.arange(h1)[None, :, None] * np.arange(n1_)[None, None, :] / n1_)) / n
    mai = np.concatenate([np.concatenate([gi.real, -gi.imag], 2), np.concatenate([gi.imag, gi.real], 2)], 1)
    cast = lambda a: jnp.asarray(a, F32).astype(BF16)
    return cast(ma), cast(ma[:, :, :h1]), cast(mc), cast(mci), cast(mai)


def _kfft_kernel(kf_ref, kb_ref, bias_ref, mar_ref, mc_ref, o_ref, af_ref, ab_ref, *, N1, N2):
    H1, PA = N1 // 2, 2 * N1 + 8

    def stage_a(n2, c):
        for src, dst in ((kf_ref, af_ref), (kb_ref, ab_ref)):
            slab = src[pl.ds(n2, H1, stride=N2), :].astype(BF16)
            dst[pl.ds(pl.multiple_of(n2 * PA, 8), 2 * N1), :] = _dot(mar_ref[n2], slab)
        return c

    lax.fori_loop(0, N2, stage_a, 0, unroll=4)
    bias = bias_ref[...]

    def stage_c(k1, c):
        def rows(ref):
            return jnp.concatenate([ref[pl.ds(k1, N2, stride=PA), :], ref[pl.ds(N1 + k1, N2, stride=PA), :]], 0)
        x = _dot(mc_ref[...], jnp.concatenate([rows(af_ref), rows(ab_ref)], 1).astype(BF16))
        xf, xb = x[:, :LANES], x[:, LANES:]
        o_ref[k1] = jnp.concatenate([xf[:N2] + xb[:N2] + bias, xf[N2:] - xb[N2:]], 0).astype(o_ref.dtype)
        return c

    lax.fori_loop(0, N1, stage_c, 0, unroll=2)


def _hy_kfft(k4, bias, tabs, L):
    N1, N2 = _fft_split(L)
    _, mar, mc, _, _ = tabs
    nb = D // LANES
    return pl.pallas_call(
        partial(_kfft_kernel, N1=N1, N2=N2), grid=(2, nb),
        in_specs=[pl.BlockSpec((L, LANES), lambda o, j: (0, o * 2 * nb + j)),
                  pl.BlockSpec((L, LANES), lambda o, j: (0, o * 2 * nb + nb + j)),
                  pl.BlockSpec((None, 1, LANES), lambda o, j: (o, 0, j)),
                  pl.BlockSpec(mar.shape, lambda o, j: (0, 0, 0)), pl.BlockSpec(mc.shape, lambda o, j: (0, 0))],
        out_specs=pl.BlockSpec((None, N1, 2 * N2, LANES), lambda o, j: (o, 0, 0, j)),
        out_shape=jax.ShapeDtypeStruct((2, N1, 2 * N2, D), BF16),
        scratch_shapes=[pltpu.VMEM((N2 * (2 * N1 + 8), LANES), F32)] * 2, name="hykfft",
        compiler_params=_cp(("parallel", "parallel"), _VMEM_LIMIT))(k4, k4, bias[:, None, :], mar, mc)


def _hyconv_kernel(sig_ref, gate_ref, cws_ref, cbs_ref, cwg_ref, cbg_ref, kh_ref, ma_ref, mc_ref, mci_ref,
                   mai_ref, *rest, L, N1, N2, in_row0, out_row0, sig_conv, aliased):
    if aliased:
        rest = rest[1:]
    o_ref, nat0, nat1, s_ref, g_ref, a_ref = rest
    H1, PA = N1 // 2, 2 * N1 + 8
    NP, PADR = N2 + 8, 16
    nats = (nat0, nat1)

    def stage(ref, bb, r0):
        nat = nats[bb]
        nat[0:PADR, :] = jnp.zeros((PADR, LANES), F32)
        nat[PADR + H1 * NP:PADR + H1 * NP + 8, :] = jnp.zeros((8, LANES), F32)
        for n1 in range(H1):
            nat[PADR + n1 * NP:PADR + n1 * NP + N2, :] = ref[bb, r0 + n1 * N2:r0 + (n1 + 1) * N2, :].astype(F32)

    def permute(bb, dest, w_ref, b_ref):
        nat = nats[bb]
        if w_ref is not None:
            w = w_ref[...]
            b = b_ref[...]

        def body(n2, c):
            u = nat[pl.ds(PADR + n2, H1, stride=NP), :]
            if w_ref is not None:
                prev = jnp.where(n2 == 0, PADR - NP + N2 - 1, PADR + n2 - 1)
                nxt = jnp.where(n2 == N2 - 1, PADR + NP, PADR + n2 + 1)
                u = (w[0:1] * nat[pl.ds(prev, H1, stride=NP), :] + w[1:2] * u
                     + w[2:3] * nat[pl.ds(nxt, H1, stride=NP), :] + b)
            dest[n2, bb * H1:(bb + 1) * H1, :] = u.astype(dest.dtype)
            return c

        lax.fori_loop(0, N2, body, 0, unroll=4)

    for bb in range(2):
        stage(sig_ref, bb, in_row0 if sig_conv else 0)
        permute(bb, s_ref, cws_ref if sig_conv else None, cbs_ref)
    for bb in range(2):
        stage(gate_ref, bb, in_row0)
        permute(bb, g_ref, cwg_ref, cbg_ref)

    def stage_a(n2, c):
        a_ref[pl.ds(pl.multiple_of(n2 * PA, 8), 2 * N1), :] = _dot(ma_ref[n2], s_ref[n2])
        return c

    lax.fori_loop(0, N2, stage_a, 0, unroll=8)

    def stage_c(i, c):
        k1 = 2 * i

        def rows(k):
            return jnp.concatenate([a_ref[pl.ds(k, N2, stride=PA), :], a_ref[pl.ds(N1 + k, N2, stride=PA), :]], 0)

        x = _dot(mc_ref[...], jnp.concatenate([rows(k1), rows(k1 + 1)], 1).astype(BF16))
        kh = jnp.concatenate([kh_ref[k1], kh_ref[k1 + 1]], 1).astype(F32)
        xr, xi, kr, ki = x[:N2], x[N2:], kh[:N2], kh[N2:]
        y = jnp.concatenate([xr * kr - xi * ki, xr * ki + xi * kr], 0)
        cc = _dot(mci_ref[...], y.astype(BF16))
        for j in range(2):
            a_ref[pl.ds(k1 + j, N2, stride=PA), :] = cc[:N2, j * LANES:(j + 1) * LANES]
            a_ref[pl.ds(N1 + k1 + j, N2, stride=PA), :] = cc[N2:, j * LANES:(j + 1) * LANES]
        return c

    lax.fori_loop(0, N1 // 2, stage_c, 0, unroll=2)

    def stage_ai(n2, c):
        st = a_ref[pl.ds(pl.multiple_of(n2 * PA, 8), 2 * N1), :]
        r = _dot(mai_ref[n2], st.astype(BF16)) * g_ref[n2].astype(F32)
        nat0[pl.ds(PADR + n2, H1, stride=NP), :] = r[:H1]
        nat1[pl.ds(PADR + n2, H1, stride=NP), :] = r[H1:]
        return c

    lax.fori_loop(0, N2, stage_ai, 0, unroll=8)

    for bb in range(2):
        if out_row0:
            o_ref[bb, 0:out_row0, :] = jnp.zeros((out_row0, LANES), o_ref.dtype)
        for n1 in range(H1):
            o_ref[bb, out_row0 + n1 * N2:out_row0 + (n1 + 1) * N2, :] = (
                nats[bb][PADR + n1 * NP:PADR + n1 * NP + N2, :].astype(o_ref.dtype))


def _hy_conv(sig, sig_col, gate, gate_col, cw, cb, kh, order, tabs, L, in_row0, out_rows, out_row0, sig_conv,
             alias_into=None):
    N1, N2 = _fft_split(L)
    ma, _, mc, mci, mai = tabs
    H1 = N1 // 2
    B = gate.shape[0]
    nb = D // LANES
    scol = sig_col if sig_conv else 0
    cspec = lambda r, c0: pl.BlockSpec((r, LANES), lambda j, q: (0, c0 + j))
    const = lambda a: pl.BlockSpec(a.shape, lambda j, q: (0,) * a.ndim)
    in_specs = [pl.BlockSpec((2, sig.shape[1], LANES), lambda j, q: (q, 0, sig_col + j)),
                pl.BlockSpec((2, gate.shape[1], LANES), lambda j, q: (q, 0, gate_col + j)),
                cspec(3, scol), cspec(1, scol), cspec(3, gate_col), cspec(1, gate_col),
                pl.BlockSpec((None, N1, 2 * N2, LANES), lambda j, q: (order, 0, 0, j)),
                const(ma), const(mc), const(mci), const(mai)]
    args = [sig, gate, cw, cb, cw, cb, kh, ma, mc, mci, mai]
    aliases = {}
    if alias_into is not None:
        in_specs.append(pl.BlockSpec(memory_space=pl.ANY))
        args.append(alias_into)
        aliases = {len(args) - 1: 0}
        out_total = alias_into.shape[1]
    else:
        out_total = out_row0 + out_rows
    return pl.pallas_call(
        partial(_hyconv_kernel, L=L, N1=N1, N2=N2, in_row0=in_row0, out_row0=out_row0, sig_conv=sig_conv,
                aliased=alias_into is not None),
        grid=(nb, B // 2), in_specs=in_specs,
        out_specs=pl.BlockSpec((2, out_row0 + out_rows, LANES), lambda j, q: (q, 0, j)),
        out_shape=jax.ShapeDtypeStruct((B, out_total, D), BF16),
        scratch_shapes=[pltpu.VMEM((H1 * (N2 + 8) + 24, LANES), F32), pltpu.VMEM((H1 * (N2 + 8) + 24, LANES), F32),
                        pltpu.VMEM((N2, 2 * H1, LANES), BF16), pltpu.VMEM((N2, 2 * H1, LANES), BF16),
                        pltpu.VMEM((N2 * (2 * N1 + 8), LANES), F32)],
        input_output_aliases=aliases, name="hyconv",
        compiler_params=_cp(("parallel", "parallel"), _VMEM_LIMIT))(*args)


def _ssdpre_kernel(x_ref, w_ref, b_ref, o_ref, nat, *, T, CTX):
    W = x_ref.shape[1]
    nat[0:8, :] = jnp.zeros((8, W), F32)
    nat[8 + T:16 + T, :] = jnp.zeros((8, W), F32)
    for i in range(T // ROW_TILE):
        nat[8 + i * ROW_TILE:8 + (i + 1) * ROW_TILE, :] = x_ref[i * ROW_TILE:(i + 1) * ROW_TILE, :].astype(F32)
    w = w_ref[...]
    b = b_ref[...]
    row = lax.broadcasted_iota(jnp.int32, (ROW_TILE, 1), 0)
    for i in range(T // ROW_TILE):
        r0 = 8 + i * ROW_TILE
        xm = nat[r0 - 1:r0 - 1 + ROW_TILE, :]
        xp = nat[r0 + 1:r0 + 1 + ROW_TILE, :]
        if i * ROW_TILE == CTX:
            xm = jnp.where(row == 0, 0.0, xm)
        if (i + 1) * ROW_TILE == CTX:
            xp = jnp.where(row == ROW_TILE - 1, 0.0, xp)
        u = w[0:1] * xm + w[1:2] * nat[r0:r0 + ROW_TILE, :] + w[2:3] * xp + b
        o_ref[i * ROW_TILE:(i + 1) * ROW_TILE, :] = _silu(u).astype(o_ref.dtype)


def _ssd_pre(p3, cw, cb, CTX):
    B, T, _ = p3.shape
    W = 256
    nblk = cw.shape[1] // W
    return pl.pallas_call(
        partial(_ssdpre_kernel, T=T, CTX=CTX), grid=(B, nblk),
        in_specs=[pl.BlockSpec((None, T, W), lambda b, j: (b, 0, _PCOL_XBC // W + j)),
                  pl.BlockSpec((3, W), lambda b, j: (0, j)), pl.BlockSpec((1, W), lambda b, j: (0, j))],
        out_specs=pl.BlockSpec((None, T, W), lambda b, j: (b, 0, j)),
        out_shape=jax.ShapeDtypeStruct((B, T, cw.shape[1]), BF16),
        scratch_shapes=[pltpu.VMEM((T + 16, W), F32)], name="ssdpre",
        compiler_params=_cp(("parallel", "parallel")))(p3, cw, cb)


def _ssdscan_kernel(xbc_ref, dt_ref, dtb_ref, a_ref, tri_ref, *rest, rev):
    if rev:
        o_ref, s_ref = rest
    else:
        z_ref, yr_ref, dsk_ref, ng_ref, o_ref, s_ref, ybuf = rest
    Q = SSD_CHUNK

    @pl.when(pl.program_id(1) == 0)
    def _():
        s_ref[...] = jnp.zeros(s_ref.shape, F32)

    x = xbc_ref[...]
    xs, Bm, Cm = x[:, :D], x[:, D:D + 2 * LANES], x[:, D + 2 * LANES:]
    dtr = dt_ref[...].astype(F32) + dtb_ref[...]
    dt = jnp.maximum(dtr, 0.0) + jnp.log(1.0 + jnp.exp(-jnp.abs(dtr)))
    tri = tri_ref[...]
    mask = tri > 0.5
    cs = _dot(tri, dt * a_ref[...], HIGHEST)
    csT = cs.T
    dtT = dt.T
    last = 0 if rev else Q - 1
    col0 = SSD_HEADS if rev else 0
    lane_lo = lax.broadcasted_iota(jnp.int32, (1, LANES), 1) < SSD_HEADDIM
    for g in range(2):
        Bg = Bm[:, g * LANES:(g + 1) * LANES]
        Cg = Cm[:, g * LANES:(g + 1) * LANES]
        CB = lax.dot_general(Cg, Bg, (((1,), (1,)), ((), ())), preferred_element_type=F32)
        BgT = Bg.astype(F32).T
        Cgf = Cg.astype(F32)
        for pr in range(4):
            hp = g * 4 + pr
            xs_pair = xs[:, hp * LANES:(hp + 1) * LANES]
            Sp = s_ref[hp]
            rhs = jnp.concatenate([xs_pair, Sp.astype(BF16)], 0)
            ys, sts = [], []
            for hh in range(2):
                col = col0 + 2 * hp + hh
                csl = jnp.broadcast_to(cs[:, col:col + 1], (Q, Q))
                csr = csT[col:col + 1, :]
                dtr_row = dtT[col:col + 1, :]
                Lm = jnp.exp(jnp.where(mask, csl - csr, -1e30))
                lhs = jnp.concatenate([CB * Lm * dtr_row, Cgf * jnp.exp(csl)], 1).astype(BF16)
                ys.append(_dot(lhs, rhs))
                tot = csT[col:col + 1, last:last + 1]
                w_row = jnp.exp(tot - csr) * dtr_row
                st = _dot((BgT * w_row).astype(BF16), xs_pair)
                sts.append(jnp.exp(tot) * Sp + st)
            y_pair = jnp.where(lane_lo, ys[0], ys[1])
            s_ref[hp] = jnp.where(lane_lo, sts[0], sts[1])
            sl = slice(hp * LANES, (hp + 1) * LANES)
            if rev:
                o_ref[:, sl] = y_pair.astype(o_ref.dtype)
            else:
                ybuf[:, sl] = y_pair + yr_ref[:, sl].astype(F32) + xs_pair.astype(F32) * dsk_ref[:, sl]
    if not rev:
        y = ybuf[...] * _silu(z_ref[...].astype(F32))
        half = D // 2
        parts = []
        for g in range(2):
            yg = y[:, g * half:(g + 1) * half]
            parts.append(yg * lax.rsqrt(jnp.mean(yg * yg, axis=-1, keepdims=True) + EPS))
        o_ref[...] = (jnp.concatenate(parts, 1) * ng_ref[...]).astype(o_ref.dtype)


def _ssd_scan(xbc, p3, dtb, arow, rev, CTX, fin=None):
    B, T, _ = xbc.shape
    Q = SSD_CHUNK
    nch, nc = T // Q, CTX // Q
    if rev:
        chunk = lambda s: jnp.where(s < nc, nc - 1 - s, nch - 1 + nc - s)
        tri = jnp.asarray(np.triu(np.ones((Q, Q), np.float32)))
    else:
        chunk = lambda s: s
        tri = jnp.asarray(np.tril(np.ones((Q, Q), np.float32)))
    row = lambda w: pl.BlockSpec((1, w), lambda b, s: (0, 0))
    in_specs = [pl.BlockSpec((None, Q, xbc.shape[2]), lambda b, s: (b, chunk(s), 0)),
                pl.BlockSpec((None, Q, LANES), lambda b, s: (b, chunk(s), _PCOL_DT // LANES)),
                row(LANES), row(LANES), pl.BlockSpec((Q, Q), lambda b, s: (0, 0))]
    args = [xbc, p3, dtb, arow, tri]
    scratch = [pltpu.VMEM((SSD_HEADS // 2, LANES, LANES), F32)]
    if not rev:
        yrev, dskip, ng = fin
        in_specs += [pl.BlockSpec((None, Q, D), lambda b, s: (b, s, _PCOL_Z // D)),
                     pl.BlockSpec((None, Q, D), lambda b, s: (b, s, 0)), row(D), row(D)]
        args += [p3, yrev, dskip, ng]
        scratch.append(pltpu.VMEM((Q, D), F32))
    return pl.pallas_call(
        partial(_ssdscan_kernel, rev=rev), grid=(B, nch), in_specs=in_specs,
        out_specs=pl.BlockSpec((None, Q, D), lambda b, s: (b, chunk(s), 0)),
        out_shape=jax.ShapeDtypeStruct((B, T, D), BF16), scratch_shapes=scratch,
        name="ssdrev" if rev else "ssdfwd",
        compiler_params=_cp(("parallel", "arbitrary")))(*args)


def _dapre_kernel(q_ref, k_ref, gq_ref, gk_ref, cos_ref, sin_ref, ones_ref, qo_ref, ko_ref):
    ones = ones_ref[...]
    cosf = jnp.concatenate([cos_ref[...]] * (D // LANES), 1)
    sinf = jnp.concatenate([sin_ref[...]] * (D // LANES), 1)
    lane = lax.broadcasted_iota(jnp.int32, (1, D), 1)
    first_half = (lane % 32) < 16

    def norm_rope(x_ref, g_ref, scale):
        x = x_ref[...].astype(F32)
        sq = x * x
        hi = sq.astype(BF16)
        lo = (sq - hi.astype(F32)).astype(BF16)
        parts = []
        for blk in range(D // 256):
            sl = slice(blk * 256, (blk + 1) * 256)
            parts.append(_dot(hi[:, sl], ones) + _dot(lo[:, sl], ones))
        ss = jnp.concatenate(parts, 1)
        y = x * lax.rsqrt(ss * (1.0 / DA_HEAD_DIM) + EPS) * g_ref[...]
        partner = jnp.where(first_half, pltpu.roll(y, D - 16, 1), pltpu.roll(y, 16, 1))
        return ((y * cosf + partner * sinf) * scale).astype(BF16)

    qo_ref[...] = norm_rope(q_ref, gq_ref, DA_HEAD_DIM ** -0.5)
    ko_ref[...] = norm_rope(k_ref, gk_ref, 1.0)


def _rope_tables(L, CTX):
    rows = L // GRID_W
    r = np.repeat(np.arange(rows), GRID_W)
    c = np.tile(np.arange(GRID_W), rows)
    nf = DA_HEAD_DIM // 4
    inv = jnp.asarray(10000.0, F32) ** (-jnp.arange(nf, dtype=F32) / nf)
    ang = jnp.stack([jnp.asarray(r), jnp.asarray(c)], -1).astype(F32)[:, :, None] * inv
    cos, sin = jnp.cos(ang), jnp.sin(ang)
    cos64 = jnp.stack([cos, cos], 2).reshape(L, DA_HEAD_DIM)
    sin64 = jnp.stack([-sin, sin], 2).reshape(L, DA_HEAD_DIM)
    cos_t = jnp.concatenate([jnp.ones((CTX, DA_HEAD_DIM), F32), cos64], 0)
    sin_t = jnp.concatenate([jnp.zeros((CTX, DA_HEAD_DIM), F32), sin64], 0)
    return jnp.tile(cos_t, (1, 2)), jnp.tile(sin_t, (1, 2))


def _da_pre(p2, gq, gk, cos_t, sin_t, tpb):
    rows = p2.shape[0]
    ones = jnp.asarray(np.kron(np.eye(256 // DA_HEAD_DIM), np.ones((DA_HEAD_DIM, DA_HEAD_DIM))), BF16)
    tab = pl.BlockSpec((ROW_TILE, LANES), lambda t: (t % tpb, 0))
    row = pl.BlockSpec((1, D), lambda t: (0, 0))
    out = pl.BlockSpec((ROW_TILE, D), lambda t: (t, 0))
    return pl.pallas_call(
        _dapre_kernel, grid=(rows // ROW_TILE,),
        in_specs=[pl.BlockSpec((ROW_TILE, D), lambda t: (t, _PCOL_DA // D)),
                  pl.BlockSpec((ROW_TILE, D), lambda t: (t, _PCOL_DA // D + 1)), row, row, tab, tab,
                  pl.BlockSpec((256, 256), lambda t: (0, 0))],
        out_specs=[out, out],
        out_shape=[jax.ShapeDtypeStruct((rows, D), BF16)] * 2, name="dapre",
        compiler_params=_cp(("parallel",)))(p2, p2, gq, gk, cos_t, sin_t, ones)


def _attn_kernel(lam_ref, q_ref, k_ref, v_ref, g_ref, *rest, lam_init, aliased):
    o_ref = rest[-1]
    lv = lam_ref[...]
    lam = (jnp.exp(jnp.sum(lv[0:1] * lv[1:2], keepdims=True)) - jnp.exp(jnp.sum(lv[2:3] * lv[3:4], keepdims=True))
           + lam_init)
    q, k, v = q_ref[...], k_ref[...], v_ref[...]
    lane_lo = lax.broadcasted_iota(jnp.int32, (1, LANES), 1) < DA_HEAD_DIM
    outs = []
    for comp in range(2):
        qm = jnp.where(lane_lo if comp == 0 else jnp.logical_not(lane_lo), q, jnp.zeros_like(q))
        s = lax.dot_general(qm, k, (((1,), (1,)), ((), ())), preferred_element_type=F32)
        e = jnp.exp(s - jnp.max(s, axis=1, keepdims=True))
        outs.append(_dot(e.astype(BF16), v) / jnp.sum(e, axis=1, keepdims=True))
    o = outs[0] - lam * outs[1]
    y = o * lax.rsqrt(jnp.mean(o * o, axis=-1, keepdims=True) + EPS) * g_ref[...] * (1.0 - lam_init)
    o_ref[...] = y.astype(o_ref.dtype)


def _da_attn(lam4, qn, kn, p3, subg, lam_init, q_row0, nq, kv_rows, alias_into=None):
    B, T, _ = qn.shape
    tq = ROW_TILE
    in_specs = [pl.BlockSpec((4, DA_HEAD_DIM), lambda b, h, i: (0, 0)),
                pl.BlockSpec((None, tq, LANES), lambda b, h, i: (b, q_row0 + i, h)),
                pl.BlockSpec((None, kv_rows, LANES), lambda b, h, i: (b, 0, h)),
                pl.BlockSpec((None, kv_rows, LANES), lambda b, h, i: (b, 0, (_PCOL_DA + 2 * D) // LANES + h)),
                pl.BlockSpec((1, LANES), lambda b, h, i: (0, 0))]
    args = [lam4, qn, kn, p3, subg]
    aliases = {}
    if alias_into is not None:
        in_specs.append(pl.BlockSpec(memory_space=pl.ANY))
        args.append(alias_into)
        aliases = {len(args) - 1: 0}
    return pl.pallas_call(
        partial(_attn_kernel, lam_init=lam_init, aliased=alias_into is not None),
        grid=(B, DA_HEADS, nq), in_specs=in_specs,
        out_specs=pl.BlockSpec((None, tq, LANES), lambda b, h, i: (b, q_row0 + i, h)),
        out_shape=jax.ShapeDtypeStruct((B, T, D), BF16), input_output_aliases=aliases, name="daattn",
        compiler_params=_cp(("parallel", "parallel", "parallel"), _VMEM_LIMIT))(*args)


def _merge_kernel(yh_ref, ys_ref, yd_ref, g0_ref, g1_ref, g2_ref, h_ref, wb_ref, wo_ref, gate_ref, n2_ref, sh_ref,
                  sc_ref, *rest, route):
    acc = None
    for i, (y_ref, gl_ref) in enumerate(((yh_ref, g0_ref), (ys_ref, g1_ref), (yd_ref, g2_ref))):
        term = jax.nn.sigmoid(gl_ref[...].astype(F32)) * _dot(y_ref[...], wb_ref[i])
        acc = term if acc is None else acc + term
    hn = h_ref[...] + gate_ref[...] * _dot(acc.astype(BF16), wo_ref[...])
    xn = hn * lax.rsqrt(jnp.mean(hn * hn, axis=-1, keepdims=True) + EPS) * n2_ref[...]
    fin = xn * (1.0 + sc_ref[...]) + sh_ref[...]
    if not route:
        hout_ref, fin_ref = rest
    else:
        rw_ref, hout_ref, fin_ref, rg_ref = rest
        logits = _dot(fin, rw_ref[...], HIGHEST)
        lane = lax.broadcasted_iota(jnp.int32, logits.shape, 1)
        logits = jnp.where(lane < N_EXPERTS, logits, -jnp.inf)
        m1 = jnp.max(logits, axis=1, keepdims=True)
        i1 = jnp.min(jnp.where(logits == m1, lane, LANES), axis=1, keepdims=True)
        rest_l = jnp.where(lane == i1, -jnp.inf, logits)
        m2 = jnp.max(rest_l, axis=1, keepdims=True)
        i2 = jnp.min(jnp.where(rest_l == m2, lane, LANES), axis=1, keepdims=True)
        e2 = jnp.exp(m2 - m1)
        rg_ref[...] = jnp.where(lane == i1, 1.0 / (1.0 + e2), 0.0) + jnp.where(lane == i2, e2 / (1.0 + e2), 0.0)
    hout_ref[...] = hn
    fin_ref[...] = fin.astype(fin_ref.dtype)


def _merge(yh, ys, yd, p2, h, wb, wo, mod3, n2g, tpb, latent_only, router_w=None):
    rows = h.shape[0]
    if latent_only:
        nt = rows // ROW_TILE // tpb * (tpb - 1)
        src = lambda t: t // (tpb - 1) * tpb + 1 + t % (tpb - 1)
    else:
        nt = rows // ROW_TILE
        src = lambda t: t
    tile = lambda cb: pl.BlockSpec((ROW_TILE, D), lambda t: (src(t), cb))
    mspec = lambda k: pl.BlockSpec((None, 1, D), lambda t: (_modrow(src(t), tpb) * 6 + k, 0, 0))
    in_specs = [tile(0), tile(0), tile(0), tile(_PCOL_GATE // D), tile(_PCOL_GATE // D + 1),
                tile(_PCOL_GATE // D + 2), tile(0), pl.BlockSpec((3, D, D), lambda t: (0, 0, 0)),
                pl.BlockSpec((D, D), lambda t: (0, 0)), mspec(2), pl.BlockSpec((1, D), lambda t: (0, 0)),
                mspec(3), mspec(4)]
    args = [yh, ys, yd, p2, p2, p2, h, wb, wo, mod3, n2g, mod3, mod3]
    otile = pl.BlockSpec((ROW_TILE, D), lambda t: (t, 0))
    out_specs = [otile, otile]
    out_shape = [jax.ShapeDtypeStruct((nt * ROW_TILE, D), F32), jax.ShapeDtypeStruct((nt * ROW_TILE, D), BF16)]
    if router_w is not None:
        in_specs.append(pl.BlockSpec((D, LANES), lambda t: (0, 0)))
        args.append(router_w)
        out_specs.append(pl.BlockSpec((ROW_TILE, LANES), lambda t: (t, 0)))
        out_shape.append(jax.ShapeDtypeStruct((nt * ROW_TILE, LANES), F32))
    return pl.pallas_call(
        partial(_merge_kernel, route=router_w is not None), grid=(nt,), in_specs=in_specs, out_specs=out_specs,
        out_shape=out_shape, name="merge", compiler_params=_cp(("parallel",), _VMEM_LIMIT))(*args)


def _ffn_kernel(x_ref, w1_ref, w3_ref, w2_ref, h_ref, gc_ref, gl_ref, o_ref, acc, *, ctx_rows, tiles_per_batch):
    f = pl.program_id(1)
    x = x_ref[...]
    a = (_silu(_dot(x, w1_ref[...])) * _dot(x, w3_ref[...])).astype(BF16)
    part = _dot(a, w2_ref[...])

    @pl.when(f == 0)
    def _():
        acc[...] = part

    @pl.when(f > 0)
    def _():
        acc[...] += part

    @pl.when(f == pl.num_programs(1) - 1)
    def _():
        tm = x_ref.shape[0]
        r = (pl.program_id(0) % tiles_per_batch) * tm + lax.broadcasted_iota(jnp.int32, (tm, 1), 0)
        gate = jnp.where(r < ctx_rows, gc_ref[...], gl_ref[...])
        o_ref[...] = h_ref[...] + gate * acc[...]


def _ffn(fin, w1, w3, w2, h, mod3, T, CTX):
    rows = h.shape[0]
    tpb = 4
    tm = T // tpb
    ff = w1.shape[1]
    tf = 512
    return pl.pallas_call(
        partial(_ffn_kernel, ctx_rows=CTX, tiles_per_batch=tpb), grid=(rows // tm, ff // tf),
        in_specs=[pl.BlockSpec((tm, D), lambda i, f: (i, 0)), pl.BlockSpec((D, tf), lambda i, f: (0, f)),
                  pl.BlockSpec((D, tf), lambda i, f: (0, f)), pl.BlockSpec((tf, D), lambda i, f: (f, 0)),
                  pl.BlockSpec((tm, D), lambda i, f: (i, 0)),
                  pl.BlockSpec((None, 1, D), lambda i, f: (5, 0, 0)),
                  pl.BlockSpec((None, 1, D), lambda i, f: ((1 + i // tpb) * 6 + 5, 0, 0))],
        out_specs=pl.BlockSpec((tm, D), lambda i, f: (i, 0)),
        out_shape=jax.ShapeDtypeStruct((rows, D), F32), scratch_shapes=[pltpu.VMEM((tm, D), F32)], name="ffn",
        compiler_params=_cp(("parallel", "arbitrary"), _VMEM_LIMIT))(fin, w1, w3, w2, h, mod3, mod3)


def _moe_kernel(x_ref, rg_ref, w1_ref, w3_ref, w2_ref, h_ref, gl_ref, o_ref, acc):
    e, f = pl.program_id(1), pl.program_id(2)
    x = x_ref[...]
    a = _silu(_dot(x, w1_ref[...])) * _dot(x, w3_ref[...]) * rg_ref[...]
    part = _dot(a.astype(BF16), w2_ref[...])
    first = jnp.logical_and(e == 0, f == 0)

    @pl.when(first)
    def _():
        acc[...] = part

    @pl.when(jnp.logical_not(first))
    def _():
        acc[...] += part

    @pl.when(jnp.logical_and(e == pl.num_programs(1) - 1, f == pl.num_programs(2) - 1))
    def _():
        o_ref[...] = h_ref[...] + gl_ref[...] * acc[...]


def _moe(fin, rgate, w1, w3, w2, h, mod3, SEQ):
    rows = h.shape[0]
    tm = min(SEQ, 1024)
    ff = w1.shape[2]
    tf = 512
    tiles_per_batch = SEQ // tm
    rg = rgate[:, :N_EXPERTS].T[:, :, None]
    return pl.pallas_call(
        _moe_kernel, grid=(rows // tm, N_EXPERTS, ff // tf),
        in_specs=[pl.BlockSpec((tm, D), lambda i, e, f: (i, 0)),
                  pl.BlockSpec((None, tm, 1), lambda i, e, f: (e, i, 0)),
                  pl.BlockSpec((None, D, tf), lambda i, e, f: (e, 0, f)),
                  pl.BlockSpec((None, D, tf), lambda i, e, f: (e, 0, f)),
                  pl.BlockSpec((None, tf, D), lambda i, e, f: (e, f, 0)),
                  pl.BlockSpec((tm, D), lambda i, e, f: (i, 0)),
                  pl.BlockSpec((None, 1, D), lambda i, e, f: ((1 + i // tiles_per_batch) * 6 + 5, 0, 0))],
        out_specs=pl.BlockSpec((tm, D), lambda i, e, f: (i, 0)),
        out_shape=jax.ShapeDtypeStruct((rows, D), F32), scratch_shapes=[pltpu.VMEM((tm, D), F32)], name="moe",
        compiler_params=_cp(("parallel", "arbitrary", "arbitrary"), _VMEM_LIMIT))(fin, rg, w1, w3, w2, h, mod3)


def _regroup_cols(w):
    hy, ssd, da, gate = 3072, 2592, 3072, 3072
    parts = [w[..., :hy], w[..., hy + ssd + da:hy + ssd + da + gate], w[..., hy + ssd:hy + ssd + da],
             w[..., hy:hy + ssd]]
    pad = jnp.zeros(w.shape[:-1] + (_PCOLS - hy - ssd - da - gate,), w.dtype)
    return jnp.concatenate(parts + [pad], axis=-1)


def kernel(x, c, ctx, c_ctx, w_mod, b_mod, norm1_g, norm2_g, w_in, hy_conv_w, hy_conv_b, hy_w1, hy_b1, hy_w2, hy_b2,
           hy_w3, hy_b3, hy_w4, hy_freq, hy_bias, ssd_conv_w, ssd_conv_b, ssd_dt_bias, ssd_a_log, ssd_d, ssd_norm_g,
           da_q_norm, da_k_norm, da_lambda, da_subln_g, w_branch, w_out, ffn_w1, ffn_w3, ffn_w2, router_w, moe_w1,
           moe_w3, moe_w2):
    B, SEQ, _ = x.shape
    CTX = ctx.shape[1]
    assert CTX == ROW_TILE and SEQ % ROW_TILE == 0 and B % 2 == 0 and B + 1 <= 8
    T = CTX + SEQ
    tpb = T // ROW_TILE
    depth = w_in.shape[0]
    R = B * T

    h = jnp.concatenate([ctx, x], axis=1).reshape(R, D)
    cvec = jnp.zeros((8, D), F32).at[0].set(c_ctx).at[1:1 + B].set(c)
    cos_t, sin_t = _rope_tables(SEQ, CTX)
    tabs_l, tabs_c = _fft_tables(SEQ), _fft_tables(CTX)
    nb = D // LANES

    for i in range(depth):
        need_ctx = i < depth - 1
        mod = _mod(cvec, w_mod[i], b_mod[i][None])
        mod3 = mod.reshape(8 * 6, 1, D)
        xn = _normmod(h, norm1_g[i][None], mod3, tpb)
        p2 = _matmul(xn, _regroup_cols(w_in[i]).astype(BF16), T // 4, 1024, BF16)
        p3 = p2.reshape(B, T, _PCOLS)

        cw, cb = hy_conv_w[i], hy_conv_b[i][None]
        hy_args = (hy_w1[i], hy_b1[i], hy_w2[i], hy_b2[i], hy_w3[i], hy_b3[i], hy_w4[i], hy_freq[i])
        kh = _hy_kfft(_hy_filters(SEQ, *hy_args), hy_bias[i], tabs_l, SEQ)
        z1 = _hy_conv(p3, 0, p3, nb, cw, cb, kh, 0, tabs_l, SEQ, CTX, SEQ, 0, True)
        y_hy = _hy_conv(z1, 0, p3, 2 * nb, cw, cb, kh, 1, tabs_l, SEQ, CTX, SEQ, CTX, False)
        if need_ctx:
            khc = _hy_kfft(_hy_filters(CTX, *hy_args), hy_bias[i], tabs_c, CTX)
            z1c = _hy_conv(p3, 0, p3, nb, cw, cb, khc, 0, tabs_c, CTX, 0, CTX, 0, True)
            y_hy = _hy_conv(z1c, 0, p3, 2 * nb, cw, cb, khc, 1, tabs_c, CTX, 0, CTX, 0, False, alias_into=y_hy)

        xbc = _ssd_pre(p3, ssd_conv_w[i], ssd_conv_b[i][None], CTX)
        dtb = jnp.pad(ssd_dt_bias[i].reshape(1, 2 * SSD_HEADS), ((0, 0), (0, LANES - 2 * SSD_HEADS)))
        arow = jnp.pad(-jnp.exp(ssd_a_log[i].astype(F32)).reshape(1, 2 * SSD_HEADS),
                       ((0, 0), (0, LANES - 2 * SSD_HEADS)))
        y_rev = _ssd_scan(xbc, p3, dtb, arow, True, CTX)
        dskip = jnp.repeat(ssd_d[i].astype(F32), SSD_HEADDIM)[None]
        y_ssd = _ssd_scan(xbc, p3, dtb, arow, False, CTX, fin=(y_rev, dskip, ssd_norm_g[i][None]))

        lam_init = 0.8 - 0.6 * math.exp(-0.3 * (i + 1))
        gq = jnp.tile(da_q_norm[i], D // DA_HEAD_DIM)[None]
        gk = jnp.tile(da_k_norm[i], D // DA_HEAD_DIM)[None]
        qn, kn = _da_pre(p2, gq, gk, cos_t, sin_t, tpb)
        qn, kn = qn.reshape(B, T, D), kn.reshape(B, T, D)
        subg = da_subln_g[i][None]
        y_da = _da_attn(da_lambda[i], qn, kn, p3, subg, lam_init, 1, tpb - 1, T)
        if need_ctx:
            y_da = _da_attn(da_lambda[i], qn, kn, p3, subg, lam_init, 0, 1, CTX, alias_into=y_da)

        wb, wo = w_branch[i].astype(BF16), w_out[i].astype(BF16)
        flat = lambda a: a.reshape(R, D)
        j = i // 2
        if i % 2 == 0:
            h, fin = _merge(flat(y_hy), flat(y_ssd), flat(y_da), p2, h, wb, wo, mod3, norm2_g[i][None], tpb,
                            latent_only=not need_ctx)
            assert need_ctx, "dense FFN layers are expected to carry context rows"
            h = _ffn(fin, ffn_w1[j].astype(BF16), ffn_w3[j].astype(BF16), ffn_w2[j].astype(BF16), h, mod3, T, CTX)
        else:
            assert not need_ctx, "expert layers are expected to be latent only"
            rw = jnp.pad(router_w[j], ((0, 0), (0, LANES - N_EXPERTS)))
            h, fin, rgate = _merge(flat(y_hy), flat(y_ssd), flat(y_da), p2, h, wb, wo, mod3, norm2_g[i][None], tpb,
                                   latent_only=True, router_w=rw)
            h = _moe(fin, rgate, moe_w1[j].astype(BF16), moe_w3[j].astype(BF16), moe_w2[j].astype(BF16), h, mod3,
                     SEQ)
    return h.reshape(B, SEQ, D)
```

```python
import math
from functools import partial

import numpy as np
import jax
import jax.numpy as jnp
from jax import lax
from jax.experimental import pallas as pl
from jax.experimental.pallas import tpu as pltpu

F32, BF16 = jnp.float32, jnp.bfloat16
HIGHEST = lax.Precision.HIGHEST

D = 1024
EPS = 1e-6
GRID_W = 64
ROW_TILE = 256
LANES = 128
SSD_CHUNK = 128
SSD_HEADS = 16
SSD_HEADDIM = 64
DA_HEADS = 8
DA_HEAD_DIM = 64
N_EXPERTS = 8
HY_BANDS = 16
HY_EMB = 2 * HY_BANDS + 1
HY_FFN = 64

_PCOL_HY = 0
_PCOL_GATE = 3072
_PCOL_DA = 6144
_PCOL_Z = 9216
_PCOL_XBC = 10240
_PCOL_DT = 11776
_PCOLS = 12288
_VMEM_LIMIT = 56 * 1024 * 1024


def _cp(sem, vmem=None):
    return pltpu.CompilerParams(dimension_semantics=sem, vmem_limit_bytes=vmem)


def _dot(a, b, prec=None):
    return jnp.dot(a, b, preferred_element_type=F32, precision=prec)


def _silu(x):
    return x * jax.nn.sigmoid(x)


def _mod_kernel(c_ref, w_ref, b_ref, o_ref):
    o_ref[...] = _dot(_silu(c_ref[...]), w_ref[...], HIGHEST) + b_ref[...]


def _mod(cvec, w, b):
    return pl.pallas_call(
        _mod_kernel, grid=(6,),
        in_specs=[pl.BlockSpec((8, D), lambda j: (0, 0)), pl.BlockSpec((D, D), lambda j: (0, j)),
                  pl.BlockSpec((1, D), lambda j: (0, j))],
        out_specs=pl.BlockSpec((8, D), lambda j: (0, j)),
        out_shape=jax.ShapeDtypeStruct((8, 6 * D), F32), name="mod")(cvec, w, b)


def _modrow(t, tpb):
    return jnp.where(t % tpb == 0, 0, 1 + t // tpb)


def _mod_spec(k, tpb):
    return pl.BlockSpec((None, 1, D), lambda t: (_modrow(t, tpb) * 6 + k, 0, 0))


def _normmod_kernel(h_ref, g_ref, sh_ref, sc_ref, o_ref):
    x = h_ref[...]
    y = x * lax.rsqrt(jnp.mean(x * x, axis=-1, keepdims=True) + EPS) * g_ref[...]
    o_ref[...] = (y * (1.0 + sc_ref[...]) + sh_ref[...]).astype(o_ref.dtype)


def _normmod(h, g, mod3, tpb):
    rows = h.shape[0]
    return pl.pallas_call(
        _normmod_kernel, grid=(rows // ROW_TILE,),
        in_specs=[pl.BlockSpec((ROW_TILE, D), lambda t: (t, 0)), pl.BlockSpec((1, D), lambda t: (0, 0)),
                  _mod_spec(0, tpb), _mod_spec(1, tpb)],
        out_specs=pl.BlockSpec((ROW_TILE, D), lambda t: (t, 0)),
        out_shape=jax.ShapeDtypeStruct((rows, D), BF16), name="normmod",
        compiler_params=_cp(("parallel",)))(h, g, mod3, mod3)


def _mm_kernel(a_ref, b_ref, o_ref):
    o_ref[...] = _dot(a_ref[...], b_ref[...]).astype(o_ref.dtype)


def _matmul(a, b, tm, tn, out_dtype):
    m, k = a.shape
    n = b.shape[1]
    return pl.pallas_call(
        _mm_kernel, grid=(n // tn, m // tm),
        in_specs=[pl.BlockSpec((tm, k), lambda j, i: (i, 0)), pl.BlockSpec((k, tn), lambda j, i: (0, j))],
        out_specs=pl.BlockSpec((tm, tn), lambda j, i: (i, j)),
        out_shape=jax.ShapeDtypeStruct((m, n), out_dtype), name="inproj",
        compiler_params=_cp(("parallel", "parallel"), _VMEM_LIMIT))(a, b)


def _hyfilt_kernel(f_ref, w1, b1, w2, b2, w3, b3, w4, fr_ref, dl_ref, o_ref, *, shift):
    f = f_ref[...]
    fr = fr_ref[...]
    h = jnp.sin(fr[0:1] * (_dot(f, w1[...], HIGHEST) + b1[...]))
    h = jnp.sin(fr[1:2] * (_dot(h, w2[...], HIGHEST) + b2[...]))
    h = jnp.sin(fr[2:3] * (_dot(h, w3[...], HIGHEST) + b3[...]))
    k = _dot(h, w4[...], HIGHEST)
    o_ref[...] = k * (jnp.exp(-f[:, 0:1] * dl_ref[...]) + shift)


def _hy_filters(L, w1, b1, w2, b2, w3, b3, w4, freq):
    t = jnp.linspace(0.0, 1.0, L, dtype=F32)[:, None]
    w = 2.0 * math.pi * jnp.arange(L, dtype=F32)[:, None] / L
    f = jnp.linspace(1e-4, HY_BANDS - 1, HY_BANDS, dtype=F32)[None, :]
    feats = jnp.concatenate([t, jnp.cos(f * w), -jnp.sin(f * w)], axis=-1)
    feats = jnp.pad(feats, ((0, 0), (0, HY_FFN - HY_EMB)))
    w1p = jnp.pad(w1, ((0, HY_FFN - HY_EMB), (0, 0)))
    max_decay = math.log(1e-2) / 0.3
    min_decay = math.log(1e-2) / 1.5
    deltas = jnp.abs(jnp.linspace(min_decay, max_decay, D, dtype=F32))[None, :]
    tl = min(L, 512)
    full = lambda r, c: pl.BlockSpec((r, c), lambda i, j: (0, 0))
    return pl.pallas_call(
        partial(_hyfilt_kernel, shift=0.05), grid=(L // tl, 4),
        in_specs=[pl.BlockSpec((tl, HY_FFN), lambda i, j: (i, 0)), full(HY_FFN, HY_FFN), full(1, HY_FFN),
                  full(HY_FFN, HY_FFN), full(1, HY_FFN), full(HY_FFN, HY_FFN), full(1, HY_FFN),
                  pl.BlockSpec((HY_FFN, D), lambda i, j: (0, j)), full(3, HY_FFN), full(1, D)],
        out_specs=pl.BlockSpec((tl, D), lambda i, j: (i, j)),
        out_shape=jax.ShapeDtypeStruct((L, 4 * D), F32), name="hyfilt",
        compiler_params=_cp(("parallel", "parallel")))(
            feats, w1p, b1[None], w2, b2[None], w3, b3[None], w4, freq, deltas)


def _fft_split(L):
    n = 2 * L
    n2 = 128 if L >= 2048 else n // 32
    return n // n2, n2


def _fft_tables(L):
    n1_, n2_ = _fft_split(L)
    n, h1 = 2 * L, n1_ // 2
    n2 = np.arange(n2_)[:, None, None]
    k1 = np.arange(n1_)[None, :, None]
    n1 = np.arange(h1)[None, None, :]
    g = np.exp(-2j * np.pi * (n2 * k1 / n + n1 * k1 / n1_))
    ma = np.concatenate([np.concatenate([g.real, -g.imag], 2), np.concatenate([g.imag, g.real], 2)], 1)
    kk = np.arange(n2_)
    f = np.exp(-2j * np.pi * np.outer(kk, kk) / n2_)
    mc = np.block([[f.real, -f.imag], [f.imag, f.real]])
    mci = np.block([[f.real, f.imag], [-f.imag, f.real]])
    gi = np.exp(2j * np.pi * (n2 * np.arange(n1_)[None, None, :] / n
                              + np.arange(h1)[None, :, None] * np.arange(n1_)[None, None, :] / n1_)) / n
    mai = np.concatenate([np.concatenate([gi.real, -gi.imag], 2), np.concatenate([gi.imag, gi.real], 2)], 1)
    cast = lambda a: jnp.asarray(a, F32).astype(BF16)
    return cast(ma), cast(ma[:, :, :h1]), cast(mc), cast(mci), cast(mai)


def _kfft_kernel(kf_ref, kb_ref, bias_ref, mar_ref, mc_ref, o_ref, af_ref, ab_ref, *, N1, N2):
    H1, PA = N1 // 2, 2 * N1 + 8

    def stage_a(n2, c):
        for src, dst in ((kf_ref, af_ref), (kb_ref, ab_ref)):
            slab = src[pl.ds(n2, H1, stride=N2), :].astype(BF16)
            dst[pl.ds(pl.multiple_of(n2 * PA, 8), 2 * N1), :] = _dot(mar_ref[n2], slab)
        return c

    lax.fori_loop(0, N2, stage_a, 0, unroll=4)
    bias = bias_ref[...]

    def stage_c(k1, c):
        def rows(ref):
            return jnp.concatenate([ref[pl.ds(k1, N2, stride=PA), :], ref[pl.ds(N1 + k1, N2, stride=PA), :]], 0)
        x = _dot(mc_ref[...], jnp.concatenate([rows(af_ref), rows(ab_ref)], 1).astype(BF16))
        xf, xb = x[:, :LANES], x[:, LANES:]
        o_ref[k1] = jnp.concatenate([xf[:N2] + xb[:N2] + bias, xf[N2:] - xb[N2:]], 0).astype(o_ref.dtype)
        return c

    lax.fori_loop(0, N1, stage_c, 0, unroll=2)


def _hy_kfft(k4, bias, tabs, L):
    N1, N2 = _fft_split(L)
    _, mar, mc, _, _ = tabs
    nb = D // LANES
    return pl.pallas_call(
        partial(_kfft_kernel, N1=N1, N2=N2), grid=(2, nb),
        in_specs=[pl.BlockSpec((L, LANES), lambda o, j: (0, o * 2 * nb + j)),
                  pl.BlockSpec((L, LANES), lambda o, j: (0, o * 2 * nb + nb + j)),
                  pl.BlockSpec((None, 1, LANES), lambda o, j: (o, 0, j)),
                  pl.BlockSpec(mar.shape, lambda o, j: (0, 0, 0)), pl.BlockSpec(mc.shape, lambda o, j: (0, 0))],
        out_specs=pl.BlockSpec((None, N1, 2 * N2, LANES), lambda o, j: (o, 0, 0, j)),
        out_shape=jax.ShapeDtypeStruct((2, N1, 2 * N2, D), BF16),
        scratch_shapes=[pltpu.VMEM((N2 * (2 * N1 + 8), LANES), F32)] * 2, name="hykfft",
        compiler_params=_cp(("parallel", "parallel"), _VMEM_LIMIT))(k4, k4, bias[:, None, :], mar, mc)


def _hyconv_kernel(sig_ref, gate_ref, cws_ref, cbs_ref, cwg_ref, cbg_ref, kh_ref, ma_ref, mc_ref, mci_ref,
                   mai_ref, *rest, L, N1, N2, in_row0, out_row0, sig_conv, aliased):
    if aliased:
        rest = rest[1:]
    o_ref, nat0, nat1, s_ref, g_ref, a_ref = rest
    H1, PA = N1 // 2, 2 * N1 + 8
    NP, PADR = N2 + 8, 16
    nats = (nat0, nat1)

    def stage(ref, bb, r0):
        nat = nats[bb]
        nat[0:PADR, :] = jnp.zeros((PADR, LANES), F32)
        nat[PADR + H1 * NP:PADR + H1 * NP + 8, :] = jnp.zeros((8, LANES), F32)
        for n1 in range(H1):
            nat[PADR + n1 * NP:PADR + n1 * NP + N2, :] = ref[bb, r0 + n1 * N2:r0 + (n1 + 1) * N2, :].astype(F32)

    def permute(bb, dest, w_ref, b_ref):
        nat = nats[bb]
        if w_ref is not None:
            w = w_ref[...]
            b = b_ref[...]

        def body(n2, c):
            u = nat[pl.ds(PADR + n2, H1, stride=NP), :]
            if w_ref is not None:
                prev = jnp.where(n2 == 0, PADR - NP + N2 - 1, PADR + n2 - 1)
                nxt = jnp.where(n2 == N2 - 1, PADR + NP, PADR + n2 + 1)
                u = (w[0:1] * nat[pl.ds(prev, H1, stride=NP), :] + w[1:2] * u
                     + w[2:3] * nat[pl.ds(nxt, H1, stride=NP), :] + b)
            dest[n2, bb * H1:(bb + 1) * H1, :] = u.astype(dest.dtype)
            return c

        lax.fori_loop(0, N2, body, 0, unroll=4)

    for bb in range(2):
        stage(sig_ref, bb, in_row0 if sig_conv else 0)
        permute(bb, s_ref, cws_ref if sig_conv else None, cbs_ref)
    for bb in range(2):
        stage(gate_ref, bb, in_row0)
        permute(bb, g_ref, cwg_ref, cbg_ref)

    def stage_a(n2, c):
        a_ref[pl.ds(pl.multiple_of(n2 * PA, 8), 2 * N1), :] = _dot(ma_ref[n2], s_ref[n2])
        return c

    lax.fori_loop(0, N2, stage_a, 0, unroll=8)

    def stage_c(i, c):
        k1 = 2 * i

        def rows(k):
            return jnp.concatenate([a_ref[pl.ds(k, N2, stride=PA), :], a_ref[pl.ds(N1 + k, N2, stride=PA), :]], 0)

        x = _dot(mc_ref[...], jnp.concatenate([rows(k1), rows(k1 + 1)], 1).astype(BF16))
        kh = jnp.concatenate([kh_ref[k1], kh_ref[k1 + 1]], 1).astype(F32)
        xr, xi, kr, ki = x[:N2], x[N2:], kh[:N2], kh[N2:]
        y = jnp.concatenate([xr * kr - xi * ki, xr * ki + xi * kr], 0)
        cc = _dot(mci_ref[...], y.astype(BF16))
        for j in range(2):
            a_ref[pl.ds(k1 + j, N2, stride=PA), :] = cc[:N2, j * LANES:(j + 1) * LANES]
            a_ref[pl.ds(N1 + k1 + j, N2, stride=PA), :] = cc[N2:, j * LANES:(j + 1) * LANES]
        return c

    lax.fori_loop(0, N1 // 2, stage_c, 0, unroll=2)

    def stage_ai(n2, c):
        st = a_ref[pl.ds(pl.multiple_of(n2 * PA, 8), 2 * N1), :]
        r = _dot(mai_ref[n2], st.astype(BF16)) * g_ref[n2].astype(F32)
        nat0[pl.ds(PADR + n2, H1, stride=NP), :] = r[:H1]
        nat1[pl.ds(PADR + n2, H1, stride=NP), :] = r[H1:]
        return c

    lax.fori_loop(0, N2, stage_ai, 0, unroll=8)

    for bb in range(2):
        if out_row0:
            o_ref[bb, 0:out_row0, :] = jnp.zeros((out_row0, LANES), o_ref.dtype)
        for n1 in range(H1):
            o_ref[bb, out_row0 + n1 * N2:out_row0 + (n1 + 1) * N2, :] = (
                nats[bb][PADR + n1 * NP:PADR + n1 * NP + N2, :].astype(o_ref.dtype))


def _hy_conv(sig, sig_col, gate, gate_col, cw, cb, kh, order, tabs, L, in_row0, out_rows, out_row0, sig_conv,
             alias_into=None):
    N1, N2 = _fft_split(L)
    ma, _, mc, mci, mai = tabs
    H1 = N1 // 2
    B = gate.shape[0]
    nb = D // LANES
    scol = sig_col if sig_conv else 0
    cspec = lambda r, c0: pl.BlockSpec((r, LANES), lambda j, q: (0, c0 + j))
    const = lambda a: pl.BlockSpec(a.shape, lambda j, q: (0,) * a.ndim)
    in_specs = [pl.BlockSpec((2, sig.shape[1], LANES), lambda j, q: (q, 0, sig_col + j)),
                pl.BlockSpec((2, gate.shape[1], LANES), lambda j, q: (q, 0, gate_col + j)),
                cspec(3, scol), cspec(1, scol), cspec(3, gate_col), cspec(1, gate_col),
                pl.BlockSpec((None, N1, 2 * N2, LANES), lambda j, q: (order, 0, 0, j)),
                const(ma), const(mc), const(mci), const(mai)]
    args = [sig, gate, cw, cb, cw, cb, kh, ma, mc, mci, mai]
    aliases = {}
    if alias_into is not None:
        in_specs.append(pl.BlockSpec(memory_space=pl.ANY))
        args.append(alias_into)
        aliases = {len(args) - 1: 0}
        out_total = alias_into.shape[1]
    else:
        out_total = out_row0 + out_rows
    return pl.pallas_call(
        partial(_hyconv_kernel, L=L, N1=N1, N2=N2, in_row0=in_row0, out_row0=out_row0, sig_conv=sig_conv,
                aliased=alias_into is not None),
        grid=(nb, B // 2), in_specs=in_specs,
        out_specs=pl.BlockSpec((2, out_row0 + out_rows, LANES), lambda j, q: (q, 0, j)),
        out_shape=jax.ShapeDtypeStruct((B, out_total, D), BF16),
        scratch_shapes=[pltpu.VMEM((H1 * (N2 + 8) + 24, LANES), F32), pltpu.VMEM((H1 * (N2 + 8) + 24, LANES), F32),
                        pltpu.VMEM((N2, 2 * H1, LANES), BF16), pltpu.VMEM((N2, 2 * H1, LANES), BF16),
                        pltpu.VMEM((N2 * (2 * N1 + 8), LANES), F32)],
        input_output_aliases=aliases, name="hyconv",
        compiler_params=_cp(("parallel", "parallel"), _VMEM_LIMIT))(*args)


def _ssdpre_kernel(x_ref, w_ref, b_ref, o_ref, nat, *, T, CTX):
    W = x_ref.shape[1]
    nat[0:8, :] = jnp.zeros((8, W), F32)
    nat[8 + T:16 + T, :] = jnp.zeros((8, W), F32)
    for i in range(T // ROW_TILE):
        nat[8 + i * ROW_TILE:8 + (i + 1) * ROW_TILE, :] = x_ref[i * ROW_TILE:(i + 1) * ROW_TILE, :].astype(F32)
    w = w_ref[...]
    b = b_ref[...]
    row = lax.broadcasted_iota(jnp.int32, (ROW_TILE, 1), 0)
    for i in range(T // ROW_TILE):
        r0 = 8 + i * ROW_TILE
        xm = nat[r0 - 1:r0 - 1 + ROW_TILE, :]
        xp = nat[r0 + 1:r0 + 1 + ROW_TILE, :]
        if i * ROW_TILE == CTX:
            xm = jnp.where(row == 0, 0.0, xm)
        if (i + 1) * ROW_TILE == CTX:
            xp = jnp.where(row == ROW_TILE - 1, 0.0, xp)
        u = w[0:1] * xm + w[1:2] * nat[r0:r0 + ROW_TILE, :] + w[2:3] * xp + b
        o_ref[i * ROW_TILE:(i + 1) * ROW_TILE, :] = _silu(u).astype(o_ref.dtype)


def _ssd_pre(p3, cw, cb, CTX):
    B, T, _ = p3.shape
    W = 256
    nblk = cw.shape[1] // W
    return pl.pallas_call(
        partial(_ssdpre_kernel, T=T, CTX=CTX), grid=(B, nblk),
        in_specs=[pl.BlockSpec((None, T, W), lambda b, j: (b, 0, _PCOL_XBC // W + j)),
                  pl.BlockSpec((3, W), lambda b, j: (0, j)), pl.BlockSpec((1, W), lambda b, j: (0, j))],
        out_specs=pl.BlockSpec((None, T, W), lambda b, j: (b, 0, j)),
        out_shape=jax.ShapeDtypeStruct((B, T, cw.shape[1]), BF16),
        scratch_shapes=[pltpu.VMEM((T + 16, W), F32)], name="ssdpre",
        compiler_params=_cp(("parallel", "parallel")))(p3, cw, cb)


def _ssdscan_kernel(xbc_ref, dt_ref, dtb_ref, a_ref, tri_ref, *rest, rev):
    if rev:
        o_ref, s_ref = rest
    else:
        z_ref, yr_ref, dsk_ref, ng_ref, o_ref, s_ref, ybuf = rest
    Q = SSD_CHUNK

    @pl.when(pl.program_id(1) == 0)
    def _():
        s_ref[...] = jnp.zeros(s_ref.shape, F32)

    x = xbc_ref[...]
    xs, Bm, Cm = x[:, :D], x[:, D:D + 2 * LANES], x[:, D + 2 * LANES:]
    dtr = dt_ref[...].astype(F32) + dtb_ref[...]
    dt = jnp.maximum(dtr, 0.0) + jnp.log(1.0 + jnp.exp(-jnp.abs(dtr)))
    tri = tri_ref[...]
    mask = tri > 0.5
    cs = _dot(tri, dt * a_ref[...], HIGHEST)
    csT = cs.T
    dtT = dt.T
    last = 0 if rev else Q - 1
    col0 = SSD_HEADS if rev else 0
    lane_lo = lax.broadcasted_iota(jnp.int32, (1, LANES), 1) < SSD_HEADDIM
    for g in range(2):
        Bg = Bm[:, g * LANES:(g + 1) * LANES]
        Cg = Cm[:, g * LANES:(g + 1) * LANES]
        CB = lax.dot_general(Cg, Bg, (((1,), (1,)), ((), ())), preferred_element_type=F32)
        BgT = Bg.astype(F32).T
        Cgf = Cg.astype(F32)
        for pr in range(4):
            hp = g * 4 + pr
            xs_pair = xs[:, hp * LANES:(hp + 1) * LANES]
            Sp = s_ref[hp]
            rhs = jnp.concatenate([xs_pair, Sp.astype(BF16)], 0)
            ys, sts = [], []
            for hh in range(2):
                col = col0 + 2 * hp + hh
                csl = jnp.broadcast_to(cs[:, col:col + 1], (Q, Q))
                csr = csT[col:col + 1, :]
                dtr_row = dtT[col:col + 1, :]
                Lm = jnp.exp(jnp.where(mask, csl - csr, -1e30))
                lhs = jnp.concatenate([CB * Lm * dtr_row, Cgf * jnp.exp(csl)], 1).astype(BF16)
                ys.append(_dot(lhs, rhs))
                tot = csT[col:col + 1, last:last + 1]
                w_row = jnp.exp(tot - csr) * dtr_row
                st = _dot((BgT * w_row).astype(BF16), xs_pair)
                sts.append(jnp.exp(tot) * Sp + st)
            y_pair = jnp.where(lane_lo, ys[0], ys[1])
            s_ref[hp] = jnp.where(lane_lo, sts[0], sts[1])
            sl = slice(hp * LANES, (hp + 1) * LANES)
            if rev:
                o_ref[:, sl] = y_pair.astype(o_ref.dtype)
            else:
                ybuf[:, sl] = y_pair + yr_ref[:, sl].astype(F32) + xs_pair.astype(F32) * dsk_ref[:, sl]
    if not rev:
        y = ybuf[...] * _silu(z_ref[...].astype(F32))
        half = D // 2
        parts = []
        for g in range(2):
            yg = y[:, g * half:(g + 1) * half]
            parts.append(yg * lax.rsqrt(jnp.mean(yg * yg, axis=-1, keepdims=True) + EPS))
        o_ref[...] = (jnp.concatenate(parts, 1) * ng_ref[...]).astype(o_ref.dtype)


def _ssd_scan(xbc, p3, dtb, arow, rev, CTX, fin=None):
    B, T, _ = xbc.shape
    Q = SSD_CHUNK
    nch, nc = T // Q, CTX // Q
    if rev:
        chunk = lambda s: jnp.where(s < nc, nc - 1 - s, nch - 1 + nc - s)
        tri = jnp.asarray(np.triu(np.ones((Q, Q), np.float32)))
    else:
        chunk = lambda s: s
        tri = jnp.asarray(np.tril(np.ones((Q, Q), np.float32)))
    row = lambda w: pl.BlockSpec((1, w), lambda b, s: (0, 0))
    in_specs = [pl.BlockSpec((None, Q, xbc.shape[2]), lambda b, s: (b, chunk(s), 0)),
                pl.BlockSpec((None, Q, LANES), lambda b, s: (b, chunk(s), _PCOL_DT // LANES)),
                row(LANES), row(LANES), pl.BlockSpec((Q, Q), lambda b, s: (0, 0))]
    args = [xbc, p3, dtb, arow, tri]
    scratch = [pltpu.VMEM((SSD_HEADS // 2, LANES, LANES), F32)]
    if not rev:
        yrev, dskip, ng = fin
        in_specs += [pl.BlockSpec((None, Q, D), lambda b, s: (b, s, _PCOL_Z // D)),
                     pl.BlockSpec((None, Q, D), lambda b, s: (b, s, 0)), row(D), row(D)]
        args += [p3, yrev, dskip, ng]
        scratch.append(pltpu.VMEM((Q, D), F32))
    return pl.pallas_call(
        partial(_ssdscan_kernel, rev=rev), grid=(B, nch), in_specs=in_specs,
        out_specs=pl.BlockSpec((None, Q, D), lambda b, s: (b, chunk(s), 0)),
        out_shape=jax.ShapeDtypeStruct((B, T, D), BF16), scratch_shapes=scratch,
        name="ssdrev" if rev else "ssdfwd",
        compiler_params=_cp(("parallel", "arbitrary")))(*args)


def _dapre_kernel(q_ref, k_ref, v_ref, gq_ref, gk_ref, cos_ref, sin_ref, ones_ref, qo_ref, ko_ref, vt_ref):
    ones = ones_ref[...]
    cosf = jnp.concatenate([cos_ref[...]] * (D // LANES), 1)
    sinf = jnp.concatenate([sin_ref[...]] * (D // LANES), 1)
    lane = lax.broadcasted_iota(jnp.int32, (1, D), 1)
    first_half = (lane % 32) < 16

    def norm_rope(x_ref, g_ref, scale):
        x = x_ref[...].astype(F32)
        sq = x * x
        hi = sq.astype(BF16)
        lo = (sq - hi.astype(F32)).astype(BF16)
        parts = []
        for blk in range(D // 256):
            sl = slice(blk * 256, (blk + 1) * 256)
            parts.append(_dot(hi[:, sl], ones) + _dot(lo[:, sl], ones))
        ss = jnp.concatenate(parts, 1)
        y = x * lax.rsqrt(ss * (1.0 / DA_HEAD_DIM) + EPS) * g_ref[...]
        partner = jnp.where(first_half, pltpu.roll(y, D - 16, 1), pltpu.roll(y, 16, 1))
        return ((y * cosf + partner * sinf) * scale).astype(BF16)

    qo_ref[...] = norm_rope(q_ref, gq_ref, LOG2E * DA_HEAD_DIM ** -0.5)
    ko_ref[...] = norm_rope(k_ref, gk_ref, 1.0)
    vt_ref[...] = v_ref[...].astype(F32).T.astype(BF16)


def _rope_tables(L, CTX):
    rows = L // GRID_W
    r = np.repeat(np.arange(rows), GRID_W)
    c = np.tile(np.arange(GRID_W), rows)
    nf = DA_HEAD_DIM // 4
    inv = jnp.asarray(10000.0, F32) ** (-jnp.arange(nf, dtype=F32) / nf)
    ang = jnp.stack([jnp.asarray(r), jnp.asarray(c)], -1).astype(F32)[:, :, None] * inv
    cos, sin = jnp.cos(ang), jnp.sin(ang)
    cos64 = jnp.stack([cos, cos], 2).reshape(L, DA_HEAD_DIM)
    sin64 = jnp.stack([-sin, sin], 2).reshape(L, DA_HEAD_DIM)
    cos_t = jnp.concatenate([jnp.ones((CTX, DA_HEAD_DIM), F32), cos64], 0)
    sin_t = jnp.concatenate([jnp.zeros((CTX, DA_HEAD_DIM), F32), sin64], 0)
    return jnp.tile(cos_t, (1, 2)), jnp.tile(sin_t, (1, 2))


def _da_pre(p2, gq, gk, cos_t, sin_t, tpb):
    rows = p2.shape[0]
    ones = jnp.asarray(np.kron(np.eye(256 // DA_HEAD_DIM), np.ones((DA_HEAD_DIM, DA_HEAD_DIM))), BF16)
    tab = pl.BlockSpec((ROW_TILE, LANES), lambda t: (t % tpb, 0))
    row = pl.BlockSpec((1, D), lambda t: (0, 0))
    out = pl.BlockSpec((ROW_TILE, D), lambda t: (t, 0))
    seg = lambda k: pl.BlockSpec((ROW_TILE, D), lambda t: (t, _PCOL_DA // D + k))
    return pl.pallas_call(
        _dapre_kernel, grid=(rows // ROW_TILE,),
        in_specs=[seg(0), seg(1), seg(2), row, row, tab, tab, pl.BlockSpec((256, 256), lambda t: (0, 0))],
        out_specs=[out, out, pl.BlockSpec((None, D, ROW_TILE), lambda t: (t // tpb, 0, t % tpb))],
        out_shape=[jax.ShapeDtypeStruct((rows, D), BF16)] * 2
        + [jax.ShapeDtypeStruct((rows // ROW_TILE // tpb, D, tpb * ROW_TILE), BF16)], name="dapre",
        compiler_params=_cp(("parallel",)))(p2, p2, p2, gq, gk, cos_t, sin_t, ones)


ATTN_KEY_BLOCK = 1024
ATTN_SAFE_SCORE = 30.0
LOG2E = 1.4426950408889634


def _attn_kernel(safe_ref, lam_ref, q_ref, k_ref, vt_ref, g_ref, o_ref, *, lam_init, ctx_rows, ctx_tile):
    lv = lam_ref[...]
    lam = (jnp.exp(jnp.sum(lv[0:1] * lv[1:2], keepdims=True)) - jnp.exp(jnp.sum(lv[2:3] * lv[3:4], keepdims=True))
           + lam_init)
    q = q_ref[...]
    tq = q.shape[0]
    lane_lo = lax.broadcasted_iota(jnp.int32, (1, LANES), 1) < DA_HEAD_DIM
    qcat = jnp.concatenate([jnp.where(lane_lo, q, jnp.zeros_like(q)), jnp.where(lane_lo, jnp.zeros_like(q), q)], 0)

    nt = (((1,), (1,)), ((), ()))

    def finish(ot):
        o = ot[:, :tq] - lam * ot[:, tq:]
        y = o * lax.rsqrt(jnp.mean(o * o, axis=0, keepdims=True) + EPS) * g_ref[...] * (1.0 - lam_init)
        o_ref[...] = y.T.astype(o_ref.dtype)

    def attend_bounded(nkeys):
        l = jnp.zeros((1, 2 * tq), F32)
        acc = jnp.zeros((LANES, 2 * tq), F32)
        blocks = [(k0, min(k0 + ATTN_KEY_BLOCK, nkeys)) for k0 in range(0, nkeys, ATTN_KEY_BLOCK)]
        scores = lambda b: lax.dot_general(k_ref[b[0]:b[1], :], qcat, nt, preferred_element_type=F32)
        st = scores(blocks[0])
        for j, (k0, k1) in enumerate(blocks):
            st_next = scores(blocks[j + 1]) if j + 1 < len(blocks) else None
            e = jnp.exp2(st)
            l = l + jnp.sum(e, axis=0, keepdims=True)
            acc = acc + _dot(vt_ref[:, k0:k1], e.astype(BF16))
            st = st_next
        finish(acc / l)

    def attend_shifted(nkeys):
        st = lax.dot_general(k_ref[0:nkeys, :], qcat, nt, preferred_element_type=F32)
        e = jnp.exp2(st - jnp.max(st, axis=0, keepdims=True))
        finish(_dot(vt_ref[:, 0:nkeys], e.astype(BF16)) / jnp.sum(e, axis=0, keepdims=True))

    def attend(nkeys):
        @pl.when(safe_ref[0] == 1)
        def _():
            attend_bounded(nkeys)

        @pl.when(safe_ref[0] != 1)
        def _():
            attend_shifted(nkeys)

    if ctx_tile:
        @pl.when(pl.program_id(2) == 0)
        def _():
            attend(ctx_rows)

        @pl.when(pl.program_id(2) > 0)
        def _():
            attend(k_ref.shape[0])
    else:
        attend(k_ref.shape[0])


def _da_attn(safe, lam4, qn, kn, vt, subg_col, lam_init, CTX, with_ctx):
    B, T, _ = qn.shape
    tq = ROW_TILE
    q0 = 0 if with_ctx else CTX // tq
    nq = T // tq - q0
    grid_spec = pltpu.PrefetchScalarGridSpec(
        num_scalar_prefetch=1, grid=(B, DA_HEADS, nq),
        in_specs=[pl.BlockSpec((4, DA_HEAD_DIM), lambda b, h, i, s: (0, 0)),
                  pl.BlockSpec((None, tq, LANES), lambda b, h, i, s: (b, q0 + i, h)),
                  pl.BlockSpec((None, T, LANES), lambda b, h, i, s: (b, 0, h)),
                  pl.BlockSpec((None, LANES, T), lambda b, h, i, s: (b, h, 0)),
                  pl.BlockSpec((LANES, 1), lambda b, h, i, s: (0, 0))],
        out_specs=pl.BlockSpec((None, tq, LANES), lambda b, h, i, s: (b, i, h)))
    return pl.pallas_call(
        partial(_attn_kernel, lam_init=lam_init, ctx_rows=CTX, ctx_tile=with_ctx), grid_spec=grid_spec,
        out_shape=jax.ShapeDtypeStruct((B, nq * tq, D), BF16), name="daattn",
        compiler_params=_cp(("parallel", "parallel", "parallel"), _VMEM_LIMIT))(safe, lam4, qn, kn, vt, subg_col)


def _merge_kernel(yh_ref, ys_ref, yd_ref, g0_ref, g1_ref, g2_ref, h_ref, wb_ref, wo_ref, gate_ref, n2_ref, sh_ref,
                  sc_ref, *rest, route):
    acc = None
    for i, (y_ref, gl_ref) in enumerate(((yh_ref, g0_ref), (ys_ref, g1_ref), (yd_ref, g2_ref))):
        term = jax.nn.sigmoid(gl_ref[...].astype(F32)) * _dot(y_ref[...], wb_ref[i])
        acc = term if acc is None else acc + term
    hn = h_ref[...] + gate_ref[...] * _dot(acc.astype(BF16), wo_ref[...])
    xn = hn * lax.rsqrt(jnp.mean(hn * hn, axis=-1, keepdims=True) + EPS) * n2_ref[...]
    fin = xn * (1.0 + sc_ref[...]) + sh_ref[...]
    if not route:
        hout_ref, fin_ref = rest
    else:
        rw_ref, hout_ref, fin_ref, rg_ref = rest
        logits = _dot(fin, rw_ref[...], HIGHEST)
        lane = lax.broadcasted_iota(jnp.int32, logits.shape, 1)
        logits = jnp.where(lane < N_EXPERTS, logits, -jnp.inf)
        m1 = jnp.max(logits, axis=1, keepdims=True)
        i1 = jnp.min(jnp.where(logits == m1, lane, LANES), axis=1, keepdims=True)
        rest_l = jnp.where(lane == i1, -jnp.inf, logits)
        m2 = jnp.max(rest_l, axis=1, keepdims=True)
        i2 = jnp.min(jnp.where(rest_l == m2, lane, LANES), axis=1, keepdims=True)
        e2 = jnp.exp(m2 - m1)
        rg_ref[...] = jnp.where(lane == i1, 1.0 / (1.0 + e2), 0.0) + jnp.where(lane == i2, e2 / (1.0 + e2), 0.0)
    hout_ref[...] = hn
    fin_ref[...] = fin.astype(fin_ref.dtype)


def _merge(yh, ys, yd, p2, h, wb, wo, mod3, n2g, tpb, latent_only, router_w=None):
    rows = h.shape[0]
    if latent_only:
        nt = rows // ROW_TILE // tpb * (tpb - 1)
        src = lambda t: t // (tpb - 1) * tpb + 1 + t % (tpb - 1)
    else:
        nt = rows // ROW_TILE
        src = lambda t: t
    tile = lambda cb: pl.BlockSpec((ROW_TILE, D), lambda t: (src(t), cb))
    mspec = lambda k: pl.BlockSpec((None, 1, D), lambda t: (_modrow(src(t), tpb) * 6 + k, 0, 0))
    in_specs = [tile(0), tile(0), pl.BlockSpec((ROW_TILE, D), lambda t: (t, 0)) if latent_only else tile(0),
                tile(_PCOL_GATE // D), tile(_PCOL_GATE // D + 1),
                tile(_PCOL_GATE // D + 2), tile(0), pl.BlockSpec((3, D, D), lambda t: (0, 0, 0)),
                pl.BlockSpec((D, D), lambda t: (0, 0)), mspec(2), pl.BlockSpec((1, D), lambda t: (0, 0)),
                mspec(3), mspec(4)]
    args = [yh, ys, yd, p2, p2, p2, h, wb, wo, mod3, n2g, mod3, mod3]
    otile = pl.BlockSpec((ROW_TILE, D), lambda t: (t, 0))
    out_specs = [otile, otile]
    out_shape = [jax.ShapeDtypeStruct((nt * ROW_TILE, D), F32), jax.ShapeDtypeStruct((nt * ROW_TILE, D), BF16)]
    if router_w is not None:
        in_specs.append(pl.BlockSpec((D, LANES), lambda t: (0, 0)))
        args.append(router_w)
        out_specs.append(pl.BlockSpec((ROW_TILE, LANES), lambda t: (t, 0)))
        out_shape.append(jax.ShapeDtypeStruct((nt * ROW_TILE, LANES), F32))
    return pl.pallas_call(
        partial(_merge_kernel, route=router_w is not None), grid=(nt,), in_specs=in_specs, out_specs=out_specs,
        out_shape=out_shape, name="merge", compiler_params=_cp(("parallel",), _VMEM_LIMIT))(*args)


def _ffn_kernel(x_ref, w1_ref, w3_ref, w2_ref, h_ref, gc_ref, gl_ref, o_ref, acc, *, ctx_rows, tiles_per_batch):
    f = pl.program_id(1)
    x = x_ref[...]
    a = (_silu(_dot(x, w1_ref[...])) * _dot(x, w3_ref[...])).astype(BF16)
    part = _dot(a, w2_ref[...])

    @pl.when(f == 0)
    def _():
        acc[...] = part

    @pl.when(f > 0)
    def _():
        acc[...] += part

    @pl.when(f == pl.num_programs(1) - 1)
    def _():
        tm = x_ref.shape[0]
        r = (pl.program_id(0) % tiles_per_batch) * tm + lax.broadcasted_iota(jnp.int32, (tm, 1), 0)
        gate = jnp.where(r < ctx_rows, gc_ref[...], gl_ref[...])
        o_ref[...] = h_ref[...] + gate * acc[...]


def _ffn(fin, w1, w3, w2, h, mod3, T, CTX):
    rows = h.shape[0]
    tpb = 4
    tm = T // tpb
    ff = w1.shape[1]
    tf = 512
    return pl.pallas_call(
        partial(_ffn_kernel, ctx_rows=CTX, tiles_per_batch=tpb), grid=(rows // tm, ff // tf),
        in_specs=[pl.BlockSpec((tm, D), lambda i, f: (i, 0)), pl.BlockSpec((D, tf), lambda i, f: (0, f)),
                  pl.BlockSpec((D, tf), lambda i, f: (0, f)), pl.BlockSpec((tf, D), lambda i, f: (f, 0)),
                  pl.BlockSpec((tm, D), lambda i, f: (i, 0)),
                  pl.BlockSpec((None, 1, D), lambda i, f: (5, 0, 0)),
                  pl.BlockSpec((None, 1, D), lambda i, f: ((1 + i // tpb) * 6 + 5, 0, 0))],
        out_specs=pl.BlockSpec((tm, D), lambda i, f: (i, 0)),
        out_shape=jax.ShapeDtypeStruct((rows, D), F32), scratch_shapes=[pltpu.VMEM((tm, D), F32)], name="ffn",
        compiler_params=_cp(("parallel", "arbitrary"), _VMEM_LIMIT))(fin, w1, w3, w2, h, mod3, mod3)


MOE_TILE = 1024
MOE_CHUNK = 256


def _slots_kernel(rg_ref, tri_ref, slot_ref, slott_ref, cnt_ref):
    blk = tri_ref.shape[0]
    carry = jnp.zeros((1, LANES), F32)
    for i in range(rg_ref.shape[0] // blk):
        sel = rg_ref[i * blk:(i + 1) * blk, :] > 0.0
        self32 = jnp.where(sel, 1.0, 0.0)
        rank = _dot(tri_ref[...], self32.astype(BF16)) + carry
        slot = jnp.where(sel, rank, -1.0)
        slot_ref[i * blk:(i + 1) * blk, :] = slot
        slott_ref[:, i * blk:(i + 1) * blk] = slot.T
        carry = carry + jnp.sum(self32, axis=0, keepdims=True)
    cnt_ref[...] = jnp.broadcast_to(carry, cnt_ref.shape)


def _moe_slots(rgate, tm):
    rows = rgate.shape[0]
    blk = 256
    tri = jnp.asarray(np.tril(np.ones((blk, blk), np.float32), -1), BF16)
    return pl.pallas_call(
        _slots_kernel, grid=(rows // tm,),
        in_specs=[pl.BlockSpec((tm, LANES), lambda i: (i, 0)), pl.BlockSpec((blk, blk), lambda i: (0, 0))],
        out_specs=[pl.BlockSpec((tm, LANES), lambda i: (i, 0)), pl.BlockSpec((LANES, tm), lambda i: (0, i)),
                   pl.BlockSpec((8, LANES), lambda i: (i, 0))],
        out_shape=[jax.ShapeDtypeStruct((rows, LANES), F32), jax.ShapeDtypeStruct((LANES, rows), F32),
                   jax.ShapeDtypeStruct((rows // tm * 8, LANES), F32)], name="moeslots",
        compiler_params=_cp(("parallel",)))(rgate, tri)


def _moe_kernel(nch_ref, x_ref, srow_ref, scol_ref, gcol_ref, w1_ref, w3_ref, w2_ref, h_ref, gl_ref, o_ref,
                xe_ref, ye_ref):
    i, e, f = pl.program_id(0), pl.program_id(1), pl.program_id(2)
    CH = MOE_CHUNK
    nch = nch_ref[i * N_EXPERTS + e]

    @pl.when(jnp.logical_and(e == 0, f == 0))
    def _():
        o_ref[...] = h_ref[...]

    def expert(c):
        xe = xe_ref[c]
        a = (_silu(_dot(xe, w1_ref[...])) * _dot(xe, w3_ref[...])).astype(BF16)
        return _dot(a, w2_ref[...])

    @pl.when(f == 0)
    def _():
        def chunk(c, carry):
            want = lax.broadcasted_iota(jnp.int32, (CH, 1), 0).astype(F32) + (c * CH).astype(F32)
            pick = jnp.where(srow_ref[...] == want, 1.0, 0.0).astype(BF16)
            xe_ref[c] = _dot(pick, x_ref[...]).astype(BF16)
            ye_ref[c] = expert(c)
            return carry

        lax.fori_loop(0, nch, chunk, 0)

    @pl.when(f == 1)
    def _():
        scale = gcol_ref[...] * gl_ref[...]

        def chunk(c, carry):
            ye = (ye_ref[c] + expert(c)).astype(BF16)
            want = lax.broadcasted_iota(jnp.int32, (1, CH), 1).astype(F32) + (c * CH).astype(F32)
            put = jnp.where(scol_ref[...] == want, 1.0, 0.0).astype(BF16)
            o_ref[...] += scale * _dot(put, ye)
            return carry

        lax.fori_loop(0, nch, chunk, 0)


def _moe(fin, rgate, w1, w3, w2, h, mod3, SEQ):
    rows = h.shape[0]
    tm = min(SEQ, MOE_TILE)
    ff = w1.shape[2]
    tf = ff // 2
    tiles_per_batch = SEQ // tm
    nt = rows // tm
    slot, slott, cnt = _moe_slots(rgate, tm)
    counts = cnt.reshape(nt, 8, LANES)[:, 0, :N_EXPERTS].astype(jnp.int32)
    nch = ((counts + MOE_CHUNK - 1) // MOE_CHUNK).reshape(nt * N_EXPERTS)
    col = lambda a: a[:, :N_EXPERTS].T[:, :, None]
    srow = slott[:N_EXPERTS][:, None, :]
    grid_spec = pltpu.PrefetchScalarGridSpec(
        num_scalar_prefetch=1, grid=(nt, N_EXPERTS, 2),
        in_specs=[pl.BlockSpec((tm, D), lambda i, e, f, n: (i, 0)),
                  pl.BlockSpec((None, 1, tm), lambda i, e, f, n: (e, 0, i)),
                  pl.BlockSpec((None, tm, 1), lambda i, e, f, n: (e, i, 0)),
                  pl.BlockSpec((None, tm, 1), lambda i, e, f, n: (e, i, 0)),
                  pl.BlockSpec((None, D, tf), lambda i, e, f, n: (e, 0, f)),
                  pl.BlockSpec((None, D, tf), lambda i, e, f, n: (e, 0, f)),
                  pl.BlockSpec((None, tf, D), lambda i, e, f, n: (e, f, 0)),
                  pl.BlockSpec((tm, D), lambda i, e, f, n: (i, 0)),
                  pl.BlockSpec((None, 1, D), lambda i, e, f, n: ((1 + i // tiles_per_batch) * 6 + 5, 0, 0))],
        out_specs=pl.BlockSpec((tm, D), lambda i, e, f, n: (i, 0)),
        scratch_shapes=[pltpu.VMEM((tm // MOE_CHUNK, MOE_CHUNK, D), BF16),
                        pltpu.VMEM((tm // MOE_CHUNK, MOE_CHUNK, D), F32)])
    return pl.pallas_call(
        _moe_kernel, grid_spec=grid_spec, out_shape=jax.ShapeDtypeStruct((rows, D), F32), name="moe",
        compiler_params=_cp(("parallel", "arbitrary", "arbitrary"), _VMEM_LIMIT))(
            nch, fin, srow, col(slot), col(rgate), w1, w3, w2, h, mod3)


def _regroup_cols(w):
    hy, ssd, da, gate = 3072, 2592, 3072, 3072
    parts = [w[..., :hy], w[..., hy + ssd + da:hy + ssd + da + gate], w[..., hy + ssd:hy + ssd + da],
             w[..., hy:hy + ssd]]
    pad = jnp.zeros(w.shape[:-1] + (_PCOLS - hy - ssd - da - gate,), w.dtype)
    return jnp.concatenate(parts + [pad], axis=-1)


def kernel(x, c, ctx, c_ctx, w_mod, b_mod, norm1_g, norm2_g, w_in, hy_conv_w, hy_conv_b, hy_w1, hy_b1, hy_w2, hy_b2,
           hy_w3, hy_b3, hy_w4, hy_freq, hy_bias, ssd_conv_w, ssd_conv_b, ssd_dt_bias, ssd_a_log, ssd_d, ssd_norm_g,
           da_q_norm, da_k_norm, da_lambda, da_subln_g, w_branch, w_out, ffn_w1, ffn_w3, ffn_w2, router_w, moe_w1,
           moe_w3, moe_w2):
    B, SEQ, _ = x.shape
    CTX = ctx.shape[1]
    assert CTX == ROW_TILE and SEQ % ROW_TILE == 0 and B % 2 == 0 and B + 1 <= 8
    T = CTX + SEQ
    tpb = T // ROW_TILE
    depth = w_in.shape[0]
    R = B * T

    h = jnp.concatenate([ctx, x], axis=1).reshape(R, D)
    cvec = jnp.zeros((8, D), F32).at[0].set(c_ctx).at[1:1 + B].set(c)
    cos_t, sin_t = _rope_tables(SEQ, CTX)
    tabs_l, tabs_c = _fft_tables(SEQ), _fft_tables(CTX)
    nb = D // LANES

    for i in range(depth):
        need_ctx = i < depth - 1
        mod = _mod(cvec, w_mod[i], b_mod[i][None])
        mod3 = mod.reshape(8 * 6, 1, D)
        xn = _normmod(h, norm1_g[i][None], mod3, tpb)
        p2 = _matmul(xn, _regroup_cols(w_in[i]).astype(BF16), T // 4, 1024, BF16)
        p3 = p2.reshape(B, T, _PCOLS)

        cw, cb = hy_conv_w[i], hy_conv_b[i][None]
        hy_args = (hy_w1[i], hy_b1[i], hy_w2[i], hy_b2[i], hy_w3[i], hy_b3[i], hy_w4[i], hy_freq[i])
        kh = _hy_kfft(_hy_filters(SEQ, *hy_args), hy_bias[i], tabs_l, SEQ)
        z1 = _hy_conv(p3, 0, p3, nb, cw, cb, kh, 0, tabs_l, SEQ, CTX, SEQ, 0, True)
        y_hy = _hy_conv(z1, 0, p3, 2 * nb, cw, cb, kh, 1, tabs_l, SEQ, CTX, SEQ, CTX, False)
        if need_ctx:
            khc = _hy_kfft(_hy_filters(CTX, *hy_args), hy_bias[i], tabs_c, CTX)
            z1c = _hy_conv(p3, 0, p3, nb, cw, cb, khc, 0, tabs_c, CTX, 0, CTX, 0, True)
            y_hy = _hy_conv(z1c, 0, p3, 2 * nb, cw, cb, khc, 1, tabs_c, CTX, 0, CTX, 0, False, alias_into=y_hy)

        xbc = _ssd_pre(p3, ssd_conv_w[i], ssd_conv_b[i][None], CTX)
        dtb = jnp.pad(ssd_dt_bias[i].reshape(1, 2 * SSD_HEADS), ((0, 0), (0, LANES - 2 * SSD_HEADS)))
        arow = jnp.pad(-jnp.exp(ssd_a_log[i].astype(F32)).reshape(1, 2 * SSD_HEADS),
                       ((0, 0), (0, LANES - 2 * SSD_HEADS)))
        y_rev = _ssd_scan(xbc, p3, dtb, arow, True, CTX)
        dskip = jnp.repeat(ssd_d[i].astype(F32), SSD_HEADDIM)[None]
        y_ssd = _ssd_scan(xbc, p3, dtb, arow, False, CTX, fin=(y_rev, dskip, ssd_norm_g[i][None]))

        lam_init = 0.8 - 0.6 * math.exp(-0.3 * (i + 1))
        gq = jnp.tile(da_q_norm[i], D // DA_HEAD_DIM)[None]
        gk = jnp.tile(da_k_norm[i], D // DA_HEAD_DIM)[None]
        qn, kn, vt = _da_pre(p2, gq, gk, cos_t, sin_t, tpb)
        score_bound = 8.0 * jnp.max(jnp.abs(da_q_norm[i])) * jnp.max(jnp.abs(da_k_norm[i]))
        safe = (score_bound <= ATTN_SAFE_SCORE).astype(jnp.int32).reshape(1)
        y_da = _da_attn(safe, da_lambda[i], qn.reshape(B, T, D), kn.reshape(B, T, D), vt, da_subln_g[i][:, None],
                        lam_init, CTX, need_ctx)

        wb, wo = w_branch[i].astype(BF16), w_out[i].astype(BF16)
        flat = lambda a: a.reshape(-1, D)
        j = i // 2
        if i % 2 == 0:
            h, fin = _merge(flat(y_hy), flat(y_ssd), flat(y_da), p2, h, wb, wo, mod3, norm2_g[i][None], tpb,
                            latent_only=not need_ctx)
            assert need_ctx, "dense FFN layers are expected to carry context rows"
            h = _ffn(fin, ffn_w1[j].astype(BF16), ffn_w3[j].astype(BF16), ffn_w2[j].astype(BF16), h, mod3, T, CTX)
        else:
            assert not need_ctx, "expert layers are expected to be latent only"
            rw = jnp.pad(router_w[j], ((0, 0), (0, LANES - N_EXPERTS)))
            h, fin, rgate = _merge(flat(y_hy), flat(y_ssd), flat(y_da), p2, h, wb, wo, mod3, norm2_g[i][None], tpb,
                                   latent_only=True, router_w=rw)
            h = _moe(fin, rgate, moe_w1[j].astype(BF16), moe_w3[j].astype(BF16), moe_w2[j].astype(BF16), h, mod3,
                     SEQ)
    return h.reshape(B, SEQ, D)
```

```python
import math
from functools import partial

import numpy as np
import jax
import jax.numpy as jnp
from jax import lax
from jax.experimental import pallas as pl
from jax.experimental.pallas import tpu as pltpu

F32, BF16 = jnp.float32, jnp.bfloat16
HIGHEST = lax.Precision.HIGHEST

D = 1024
EPS = 1e-6
GRID_W = 64
ROW_TILE = 256
LANES = 128
SSD_CHUNK = 128
SSD_HEADS = 16
SSD_HEADDIM = 64
DA_HEADS = 8
DA_HEAD_DIM = 64
N_EXPERTS = 8
HY_BANDS = 16
HY_EMB = 2 * HY_BANDS + 1
HY_FFN = 64

_PCOL_HY = 0
_PCOL_GATE = 3072
_PCOL_DA = 6144
_PCOL_Z = 9216
_PCOL_XBC = 10240
_PCOL_DT = 11776
_PCOLS = 12288
_VMEM_LIMIT = 56 * 1024 * 1024


def _cp(sem, vmem=None):
    return pltpu.CompilerParams(dimension_semantics=sem, vmem_limit_bytes=vmem)


def _dot(a, b, prec=None):
    return jnp.dot(a, b, preferred_element_type=F32, precision=prec)


def _silu(x):
    return x * jax.nn.sigmoid(x)


def _mod_kernel(c_ref, w_ref, b_ref, o_ref):
    o_ref[...] = _dot(_silu(c_ref[...]), w_ref[...], HIGHEST) + b_ref[...]


def _mod(cvec, w, b):
    return pl.pallas_call(
        _mod_kernel, grid=(6,),
        in_specs=[pl.BlockSpec((8, D), lambda j: (0, 0)), pl.BlockSpec((D, D), lambda j: (0, j)),
                  pl.BlockSpec((1, D), lambda j: (0, j))],
        out_specs=pl.BlockSpec((8, D), lambda j: (0, j)),
        out_shape=jax.ShapeDtypeStruct((8, 6 * D), F32), name="mod")(cvec, w, b)


def _modrow(t, tpb):
    return jnp.where(t % tpb == 0, 0, 1 + t // tpb)


def _mod_spec(k, tpb):
    return pl.BlockSpec((None, 1, D), lambda t: (_modrow(t, tpb) * 6 + k, 0, 0))


def _normmod_kernel(h_ref, g_ref, sh_ref, sc_ref, o_ref):
    x = h_ref[...]
    y = x * lax.rsqrt(jnp.mean(x * x, axis=-1, keepdims=True) + EPS) * g_ref[...]
    o_ref[...] = (y * (1.0 + sc_ref[...]) + sh_ref[...]).astype(o_ref.dtype)


def _normmod(h, g, mod3, tpb):
    rows = h.shape[0]
    return pl.pallas_call(
        _normmod_kernel, grid=(rows // ROW_TILE,),
        in_specs=[pl.BlockSpec((ROW_TILE, D), lambda t: (t, 0)), pl.BlockSpec((1, D), lambda t: (0, 0)),
                  _mod_spec(0, tpb), _mod_spec(1, tpb)],
        out_specs=pl.BlockSpec((ROW_TILE, D), lambda t: (t, 0)),
        out_shape=jax.ShapeDtypeStruct((rows, D), BF16), name="normmod",
        compiler_params=_cp(("parallel",)))(h, g, mod3, mod3)


def _mm_kernel(a_ref, b_ref, o_ref):
    o_ref[...] = _dot(a_ref[...], b_ref[...]).astype(o_ref.dtype)


def _matmul(a, b, tm, tn, out_dtype):
    m, k = a.shape
    n = b.shape[1]
    return pl.pallas_call(
        _mm_kernel, grid=(n // tn, m // tm),
        in_specs=[pl.BlockSpec((tm, k), lambda j, i: (i, 0)), pl.BlockSpec((k, tn), lambda j, i: (0, j))],
        out_specs=pl.BlockSpec((tm, tn), lambda j, i: (i, j)),
        out_shape=jax.ShapeDtypeStruct((m, n), out_dtype), name="inproj",
        compiler_params=_cp(("parallel", "parallel"), _VMEM_LIMIT))(a, b)


def _hyfilt_kernel(f_ref, w1, b1, w2, b2, w3, b3, w4, fr_ref, dl_ref, o_ref, h_ref, win_ref, *, shift):
    @pl.when(pl.program_id(1) == 0)
    def _():
        f = f_ref[...]
        fr = fr_ref[...]
        h = jnp.sin(fr[0:1] * (_dot(f, w1[...], HIGHEST) + b1[...]))
        h = jnp.sin(fr[1:2] * (_dot(h, w2[...], HIGHEST) + b2[...]))
        h = jnp.sin(fr[2:3] * (_dot(h, w3[...], HIGHEST) + b3[...]))
        h_ref[...] = h.astype(BF16)
        win_ref[...] = jnp.exp(-f[:, 0:1] * dl_ref[...]) + shift

    o_ref[...] = _dot(h_ref[...], w4[...]) * win_ref[...]


def _hy_filters(L, w1, b1, w2, b2, w3, b3, w4, freq):
    t = jnp.linspace(0.0, 1.0, L, dtype=F32)[:, None]
    w = 2.0 * math.pi * jnp.arange(L, dtype=F32)[:, None] / L
    f = jnp.linspace(1e-4, HY_BANDS - 1, HY_BANDS, dtype=F32)[None, :]
    feats = jnp.concatenate([t, jnp.cos(f * w), -jnp.sin(f * w)], axis=-1)
    feats = jnp.pad(feats, ((0, 0), (0, HY_FFN - HY_EMB)))
    w1p = jnp.pad(w1, ((0, HY_FFN - HY_EMB), (0, 0)))
    max_decay = math.log(1e-2) / 0.3
    min_decay = math.log(1e-2) / 1.5
    deltas = jnp.abs(jnp.linspace(min_decay, max_decay, D, dtype=F32))[None, :]
    tl = min(L, 512)
    full = lambda r, c: pl.BlockSpec((r, c), lambda i, j: (0, 0))
    return pl.pallas_call(
        partial(_hyfilt_kernel, shift=0.05), grid=(L // tl, 4),
        in_specs=[pl.BlockSpec((tl, HY_FFN), lambda i, j: (i, 0)), full(HY_FFN, HY_FFN), full(1, HY_FFN),
                  full(HY_FFN, HY_FFN), full(1, HY_FFN), full(HY_FFN, HY_FFN), full(1, HY_FFN),
                  pl.BlockSpec((HY_FFN, D), lambda i, j: (0, j)), full(3, HY_FFN), full(1, D)],
        out_specs=pl.BlockSpec((tl, D), lambda i, j: (i, j)),
        out_shape=jax.ShapeDtypeStruct((L, 4 * D), F32), name="hyfilt",
        scratch_shapes=[pltpu.VMEM((tl, HY_FFN), BF16), pltpu.VMEM((tl, D), F32)],
        compiler_params=_cp(("parallel", "arbitrary")))(
            feats, w1p, b1[None], w2, b2[None], w3, b3[None], w4.astype(BF16), freq, deltas)


def _fft_split(L):
    n = 2 * L
    n2 = 128 if L >= 2048 else n // 32
    return n // n2, n2


def _fft_tables(L):
    n1_, n2_ = _fft_split(L)
    n, h1 = 2 * L, n1_ // 2
    n2 = np.arange(n2_)[:, None, None]
    k1 = np.arange(n1_)[None, :, None]
    n1 = np.arange(h1)[None, None, :]
    g = np.exp(-2j * np.pi * (n2 * k1 / n + n1 * k1 / n1_))
    ma = np.concatenate([np.concatenate([g.real, -g.imag], 2), np.concatenate([g.imag, g.real], 2)], 1)
    kk = np.arange(n2_)
    f = np.exp(-2j * np.pi * np.outer(kk, kk) / n2_)
    mc = np.block([[f.real, -f.imag], [f.imag, f.real]])
    mci = np.block([[f.real, f.imag], [-f.imag, f.real]])
    gi = np.exp(2j * np.pi * (n2 * np.arange(n1_)[None, None, :] / n
                              + np.arange(h1)[None, :, None] * np.arange(n1_)[None, None, :] / n1_)) / n
    mai = np.concatenate([np.concatenate([gi.real, -gi.imag], 2), np.concatenate([gi.imag, gi.real], 2)], 1)
    cast = lambda a: jnp.asarray(a, F32).astype(BF16)
    return cast(ma), cast(ma[:, :, :h1]), cast(mc), cast(mci), cast(mai)


def _kfft_kernel(kf_ref, kb_ref, bias_ref, mar_ref, mc_ref, o_ref, af_ref, ab_ref, *, N1, N2):
    H1, PA = N1 // 2, 2 * N1 + 8

    def stage_a(n2, c):
        for src, dst in ((kf_ref, af_ref), (kb_ref, ab_ref)):
            slab = src[pl.ds(n2, H1, stride=N2), :].astype(BF16)
            dst[pl.ds(pl.multiple_of(n2 * PA, 8), 2 * N1), :] = _dot(mar_ref[n2], slab)
        return c

    lax.fori_loop(0, N2, stage_a, 0, unroll=8)
    bias = bias_ref[...]

    def stage_c(k1, c):
        def rows(ref):
            return jnp.concatenate([ref[pl.ds(k1, N2, stride=PA), :], ref[pl.ds(N1 + k1, N2, stride=PA), :]], 0)
        x = _dot(mc_ref[...], jnp.concatenate([rows(af_ref), rows(ab_ref)], 1).astype(BF16))
        xf, xb = x[:, :LANES], x[:, LANES:]
        o_ref[k1] = jnp.concatenate([xf[:N2] + xb[:N2] + bias, xf[N2:] - xb[N2:]], 0).astype(o_ref.dtype)
        return c

    lax.fori_loop(0, N1, stage_c, 0, unroll=16)


def _hy_kfft(k4, bias, tabs, L):
    N1, N2 = _fft_split(L)
    _, mar, mc, _, _ = tabs
    nb = D // LANES
    return pl.pallas_call(
        partial(_kfft_kernel, N1=N1, N2=N2), grid=(2, nb),
        in_specs=[pl.BlockSpec((L, LANES), lambda o, j: (0, o * 2 * nb + j)),
                  pl.BlockSpec((L, LANES), lambda o, j: (0, o * 2 * nb + nb + j)),
                  pl.BlockSpec((None, 1, LANES), lambda o, j: (o, 0, j)),
                  pl.BlockSpec(mar.shape, lambda o, j: (0, 0, 0)), pl.BlockSpec(mc.shape, lambda o, j: (0, 0))],
        out_specs=pl.BlockSpec((None, N1, 2 * N2, LANES), lambda o, j: (o, 0, 0, j)),
        out_shape=jax.ShapeDtypeStruct((2, N1, 2 * N2, D), BF16),
        scratch_shapes=[pltpu.VMEM((N2 * (2 * N1 + 8), LANES), F32)] * 2, name="hykfft",
        compiler_params=_cp(("parallel", "parallel"), _VMEM_LIMIT))(k4, k4, bias[:, None, :], mar, mc)


def _hyconv_kernel(sig_ref, gate_ref, cws_ref, cbs_ref, cwg_ref, cbg_ref, kh_ref, ma_ref, mc_ref, mci_ref,
                   mai_ref, *rest, L, N1, N2, in_row0, out_row0, sig_conv, aliased):
    if aliased:
        rest = rest[1:]
    o_ref, nat0, nat1, s_ref, g_ref, a_ref = rest
    H1, PA = N1 // 2, 2 * N1 + 8
    NP, PADR = N2 + 8, 16
    nats = (nat0, nat1)

    def stage(ref, bb, r0):
        nat = nats[bb]
        nat[0:PADR, :] = jnp.zeros((PADR, LANES), F32)
        nat[PADR + H1 * NP:PADR + H1 * NP + 8, :] = jnp.zeros((8, LANES), F32)
        for n1 in range(H1):
            nat[PADR + n1 * NP:PADR + n1 * NP + N2, :] = ref[bb, r0 + n1 * N2:r0 + (n1 + 1) * N2, :].astype(F32)

    def permute(bb, dest, w_ref, b_ref):
        nat = nats[bb]
        if w_ref is not None:
            w = w_ref[...]
            b = b_ref[...]

        def body(n2, c):
            u = nat[pl.ds(PADR + n2, H1, stride=NP), :]
            if w_ref is not None:
                prev = jnp.where(n2 == 0, PADR - NP + N2 - 1, PADR + n2 - 1)
                nxt = jnp.where(n2 == N2 - 1, PADR + NP, PADR + n2 + 1)
                u = (w[0:1] * nat[pl.ds(prev, H1, stride=NP), :] + w[1:2] * u
                     + w[2:3] * nat[pl.ds(nxt, H1, stride=NP), :] + b)
            dest[n2, bb * H1:(bb + 1) * H1, :] = u.astype(dest.dtype)
            return c

        lax.fori_loop(0, N2, body, 0, unroll=8)

    for bb in range(2):
        stage(sig_ref, bb, in_row0 if sig_conv else 0)
        permute(bb, s_ref, cws_ref if sig_conv else None, cbs_ref)
    for bb in range(2):
        stage(gate_ref, bb, in_row0)
        permute(bb, g_ref, cwg_ref, cbg_ref)

    def stage_a(n2, c):
        a_ref[pl.ds(pl.multiple_of(n2 * PA, 8), 2 * N1), :] = _dot(ma_ref[n2], s_ref[n2])
        return c

    lax.fori_loop(0, N2, stage_a, 0, unroll=16)

    def stage_c(i, c):
        k1 = 2 * i

        def rows(k):
            return jnp.concatenate([a_ref[pl.ds(k, N2, stride=PA), :], a_ref[pl.ds(N1 + k, N2, stride=PA), :]], 0)

        x = _dot(mc_ref[...], jnp.concatenate([rows(k1), rows(k1 + 1)], 1).astype(BF16))
        kh = jnp.concatenate([kh_ref[k1], kh_ref[k1 + 1]], 1).astype(F32)
        xr, xi, kr, ki = x[:N2], x[N2:], kh[:N2], kh[N2:]
        y = jnp.concatenate([xr * kr - xi * ki, xr * ki + xi * kr], 0)
        cc = _dot(mci_ref[...], y.astype(BF16))
        for j in range(2):
            a_ref[pl.ds(k1 + j, N2, stride=PA), :] = cc[:N2, j * LANES:(j + 1) * LANES]
            a_ref[pl.ds(N1 + k1 + j, N2, stride=PA), :] = cc[N2:, j * LANES:(j + 1) * LANES]
        return c

    lax.fori_loop(0, N1 // 2, stage_c, 0, unroll=16)

    def stage_ai(n2, c):
        st = a_ref[pl.ds(pl.multiple_of(n2 * PA, 8), 2 * N1), :]
        r = _dot(mai_ref[n2], st.astype(BF16)) * g_ref[n2].astype(F32)
        nat0[pl.ds(PADR + n2, H1, stride=NP), :] = r[:H1]
        nat1[pl.ds(PADR + n2, H1, stride=NP), :] = r[H1:]
        return c

    lax.fori_loop(0, N2, stage_ai, 0, unroll=16)

    for bb in range(2):
        if out_row0:
            o_ref[bb, 0:out_row0, :] = jnp.zeros((out_row0, LANES), o_ref.dtype)
        for n1 in range(H1):
            o_ref[bb, out_row0 + n1 * N2:out_row0 + (n1 + 1) * N2, :] = (
                nats[bb][PADR + n1 * NP:PADR + n1 * NP + N2, :].astype(o_ref.dtype))


def _hy_conv(sig, sig_col, gate, gate_col, cw, cb, kh, order, tabs, L, in_row0, out_rows, out_row0, sig_conv,
             alias_into=None):
    N1, N2 = _fft_split(L)
    ma, _, mc, mci, mai = tabs
    H1 = N1 // 2
    B = gate.shape[0]
    nb = D // LANES
    scol = sig_col if sig_conv else 0
    cspec = lambda r, c0: pl.BlockSpec((r, LANES), lambda j, q: (0, c0 + j))
    const = lambda a: pl.BlockSpec(a.shape, lambda j, q: (0,) * a.ndim)
    in_specs = [pl.BlockSpec((2, sig.shape[1], LANES), lambda j, q: (q, 0, sig_col + j)),
                pl.BlockSpec((2, gate.shape[1], LANES), lambda j, q: (q, 0, gate_col + j)),
                cspec(3, scol), cspec(1, scol), cspec(3, gate_col), cspec(1, gate_col),
                pl.BlockSpec((None, N1, 2 * N2, LANES), lambda j, q: (order, 0, 0, j)),
                const(ma), const(mc), const(mci), const(mai)]
    args = [sig, gate, cw, cb, cw, cb, kh, ma, mc, mci, mai]
    aliases = {}
    if alias_into is not None:
        in_specs.append(pl.BlockSpec(memory_space=pl.ANY))
        args.append(alias_into)
        aliases = {len(args) - 1: 0}
        out_total = alias_into.shape[1]
    else:
        out_total = out_row0 + out_rows
    return pl.pallas_call(
        partial(_hyconv_kernel, L=L, N1=N1, N2=N2, in_row0=in_row0, out_row0=out_row0, sig_conv=sig_conv,
                aliased=alias_into is not None),
        grid=(nb, B // 2), in_specs=in_specs,
        out_specs=pl.BlockSpec((2, out_row0 + out_rows, LANES), lambda j, q: (q, 0, j)),
        out_shape=jax.ShapeDtypeStruct((B, out_total, D), BF16),
        scratch_shapes=[pltpu.VMEM((H1 * (N2 + 8) + 24, LANES), F32), pltpu.VMEM((H1 * (N2 + 8) + 24, LANES), F32),
                        pltpu.VMEM((N2, 2 * H1, LANES), BF16), pltpu.VMEM((N2, 2 * H1, LANES), BF16),
                        pltpu.VMEM((N2 * (2 * N1 + 8), LANES), F32)],
        input_output_aliases=aliases, name="hyconv",
        compiler_params=_cp(("parallel", "parallel"), _VMEM_LIMIT))(*args)


def _ssdpre_kernel(x_ref, w_ref, b_ref, o_ref, nat, *, T, CTX):
    W = x_ref.shape[1]
    nat[0:8, :] = jnp.zeros((8, W), F32)
    nat[8 + T:16 + T, :] = jnp.zeros((8, W), F32)
    for i in range(T // ROW_TILE):
        nat[8 + i * ROW_TILE:8 + (i + 1) * ROW_TILE, :] = x_ref[i * ROW_TILE:(i + 1) * ROW_TILE, :].astype(F32)
    w = w_ref[...]
    b = b_ref[...]
    row = lax.broadcasted_iota(jnp.int32, (ROW_TILE, 1), 0)
    for i in range(T // ROW_TILE):
        r0 = 8 + i * ROW_TILE
        xm = nat[r0 - 1:r0 - 1 + ROW_TILE, :]
        xp = nat[r0 + 1:r0 + 1 + ROW_TILE, :]
        if i * ROW_TILE == CTX:
            xm = jnp.where(row == 0, 0.0, xm)
        if (i + 1) * ROW_TILE == CTX:
            xp = jnp.where(row == ROW_TILE - 1, 0.0, xp)
        u = w[0:1] * xm + w[1:2] * nat[r0:r0 + ROW_TILE, :] + w[2:3] * xp + b
        o_ref[i * ROW_TILE:(i + 1) * ROW_TILE, :] = _silu(u).astype(o_ref.dtype)


def _ssd_pre(p3, cw, cb, CTX):
    B, T, _ = p3.shape
    W = 256
    nblk = cw.shape[1] // W
    return pl.pallas_call(
        partial(_ssdpre_kernel, T=T, CTX=CTX), grid=(B, nblk),
        in_specs=[pl.BlockSpec((None, T, W), lambda b, j: (b, 0, _PCOL_XBC // W + j)),
                  pl.BlockSpec((3, W), lambda b, j: (0, j)), pl.BlockSpec((1, W), lambda b, j: (0, j))],
        out_specs=pl.BlockSpec((None, T, W), lambda b, j: (b, 0, j)),
        out_shape=jax.ShapeDtypeStruct((B, T, cw.shape[1]), BF16),
        scratch_shapes=[pltpu.VMEM((T + 16, W), F32)], name="ssdpre",
        compiler_params=_cp(("parallel", "parallel")))(p3, cw, cb)


def _ssdscan_kernel(xbc_ref, dt_ref, dtb_ref, a_ref, tri_ref, *rest, rev):
    if rev:
        o_ref, s_ref = rest
    else:
        z_ref, yr_ref, dsk_ref, ng_ref, o_ref, s_ref, ybuf = rest
    Q = SSD_CHUNK

    @pl.when(pl.program_id(1) == 0)
    def _():
        s_ref[...] = jnp.zeros(s_ref.shape, F32)

    x = xbc_ref[...]
    xs, Bm, Cm = x[:, :D], x[:, D:D + 2 * LANES], x[:, D + 2 * LANES:]
    dtr = dt_ref[...].astype(F32) + dtb_ref[...]
    dt = jnp.maximum(dtr, 0.0) + jnp.log(1.0 + jnp.exp(-jnp.abs(dtr)))
    tri = tri_ref[...]
    mask = tri > 0.5
    cs = _dot(tri, dt * a_ref[...], HIGHEST)
    csT = cs.T
    dtT = dt.T
    last = 0 if rev else Q - 1
    col0 = SSD_HEADS if rev else 0
    lane_lo = lax.broadcasted_iota(jnp.int32, (1, LANES), 1) < SSD_HEADDIM
    for g in range(2):
        Bg = Bm[:, g * LANES:(g + 1) * LANES]
        Cg = Cm[:, g * LANES:(g + 1) * LANES]
        CB = lax.dot_general(Cg, Bg, (((1,), (1,)), ((), ())), preferred_element_type=F32)
        BgT = Bg.astype(F32).T
        Cgf = Cg.astype(F32)
        for pr in range(4):
            hp = g * 4 + pr
            xs_pair = xs[:, hp * LANES:(hp + 1) * LANES]
            Sp = s_ref[hp]
            rhs = jnp.concatenate([xs_pair, Sp.astype(BF16)], 0)
            ys, sts = [], []
            for hh in range(2):
                col = col0 + 2 * hp + hh
                csl = jnp.broadcast_to(cs[:, col:col + 1], (Q, Q))
                csr = csT[col:col + 1, :]
                dtr_row = dtT[col:col + 1, :]
                Lm = jnp.exp(jnp.where(mask, csl - csr, -1e30))
                lhs = jnp.concatenate([CB * Lm * dtr_row, Cgf * jnp.exp(csl)], 1).astype(BF16)
                ys.append(_dot(lhs, rhs))
                tot = csT[col:col + 1, last:last + 1]
                w_row = jnp.exp(tot - csr) * dtr_row
                st = _dot((BgT * w_row).astype(BF16), xs_pair)
                sts.append(jnp.exp(tot) * Sp + st)
            y_pair = jnp.where(lane_lo, ys[0], ys[1])
            s_ref[hp] = jnp.where(lane_lo, sts[0], sts[1])
            sl = slice(hp * LANES, (hp + 1) * LANES)
            if rev:
                o_ref[:, sl] = y_pair.astype(o_ref.dtype)
            else:
                ybuf[:, sl] = y_pair + yr_ref[:, sl].astype(F32) + xs_pair.astype(F32) * dsk_ref[:, sl]
    if not rev:
        y = ybuf[...] * _silu(z_ref[...].astype(F32))
        half = D // 2
        parts = []
        for g in range(2):
            yg = y[:, g * half:(g + 1) * half]
            parts.append(yg * lax.rsqrt(jnp.mean(yg * yg, axis=-1, keepdims=True) + EPS))
        o_ref[...] = (jnp.concatenate(parts, 1) * ng_ref[...]).astype(o_ref.dtype)


def _ssd_scan(xbc, p3, dtb, arow, rev, CTX, fin=None):
    B, T, _ = xbc.shape
    Q = SSD_CHUNK
    nch, nc = T // Q, CTX // Q
    if rev:
        chunk = lambda s: jnp.where(s < nc, nc - 1 - s, nch - 1 + nc - s)
        tri = jnp.asarray(np.triu(np.ones((Q, Q), np.float32)))
    else:
        chunk = lambda s: s
        tri = jnp.asarray(np.tril(np.ones((Q, Q), np.float32)))
    row = lambda w: pl.BlockSpec((1, w), lambda b, s: (0, 0))
    in_specs = [pl.BlockSpec((None, Q, xbc.shape[2]), lambda b, s: (b, chunk(s), 0)),
                pl.BlockSpec((None, Q, LANES), lambda b, s: (b, chunk(s), _PCOL_DT // LANES)),
                row(LANES), row(LANES), pl.BlockSpec((Q, Q), lambda b, s: (0, 0))]
    args = [xbc, p3, dtb, arow, tri]
    scratch = [pltpu.VMEM((SSD_HEADS // 2, LANES, LANES), F32)]
    if not rev:
        yrev, dskip, ng = fin
        in_specs += [pl.BlockSpec((None, Q, D), lambda b, s: (b, s, _PCOL_Z // D)),
                     pl.BlockSpec((None, Q, D), lambda b, s: (b, s, 0)), row(D), row(D)]
        args += [p3, yrev, dskip, ng]
        scratch.append(pltpu.VMEM((Q, D), F32))
    return pl.pallas_call(
        partial(_ssdscan_kernel, rev=rev), grid=(B, nch), in_specs=in_specs,
        out_specs=pl.BlockSpec((None, Q, D), lambda b, s: (b, chunk(s), 0)),
        out_shape=jax.ShapeDtypeStruct((B, T, D), BF16), scratch_shapes=scratch,
        name="ssdrev" if rev else "ssdfwd",
        compiler_params=_cp(("parallel", "arbitrary")))(*args)


def _dapre_kernel(q_ref, k_ref, v_ref, gq_ref, gk_ref, cos_ref, sin_ref, ones_ref, qo_ref, ko_ref, vt_ref):
    ones = ones_ref[...]
    cosf = jnp.concatenate([cos_ref[...]] * (D // LANES), 1)
    sinf = jnp.concatenate([sin_ref[...]] * (D // LANES), 1)
    lane = lax.broadcasted_iota(jnp.int32, (1, D), 1)
    first_half = (lane % 32) < 16

    def norm_rope(x_ref, g_ref, scale):
        x = x_ref[...].astype(F32)
        sq = x * x
        hi = sq.astype(BF16)
        lo = (sq - hi.astype(F32)).astype(BF16)
        parts = []
        for blk in range(D // 256):
            sl = slice(blk * 256, (blk + 1) * 256)
            parts.append(_dot(hi[:, sl], ones) + _dot(lo[:, sl], ones))
        ss = jnp.concatenate(parts, 1)
        y = x * lax.rsqrt(ss * (1.0 / DA_HEAD_DIM) + EPS) * g_ref[...]
        partner = jnp.where(first_half, pltpu.roll(y, D - 16, 1), pltpu.roll(y, 16, 1))
        return ((y * cosf + partner * sinf) * scale).astype(BF16)

    qo_ref[...] = norm_rope(q_ref, gq_ref, LOG2E * DA_HEAD_DIM ** -0.5)
    ko_ref[...] = norm_rope(k_ref, gk_ref, 1.0)
    vt_ref[...] = v_ref[...].astype(F32).T.astype(BF16)


def _rope_tables(L, CTX):
    rows = L // GRID_W
    r = np.repeat(np.arange(rows), GRID_W)
    c = np.tile(np.arange(GRID_W), rows)
    nf = DA_HEAD_DIM // 4
    inv = jnp.asarray(10000.0, F32) ** (-jnp.arange(nf, dtype=F32) / nf)
    ang = jnp.stack([jnp.asarray(r), jnp.asarray(c)], -1).astype(F32)[:, :, None] * inv
    cos, sin = jnp.cos(ang), jnp.sin(ang)
    cos64 = jnp.stack([cos, cos], 2).reshape(L, DA_HEAD_DIM)
    sin64 = jnp.stack([-sin, sin], 2).reshape(L, DA_HEAD_DIM)
    cos_t = jnp.concatenate([jnp.ones((CTX, DA_HEAD_DIM), F32), cos64], 0)
    sin_t = jnp.concatenate([jnp.zeros((CTX, DA_HEAD_DIM), F32), sin64], 0)
    return jnp.tile(cos_t, (1, 2)), jnp.tile(sin_t, (1, 2))


def _da_pre(p2, gq, gk, cos_t, sin_t, tpb):
    rows = p2.shape[0]
    ones = jnp.asarray(np.kron(np.eye(256 // DA_HEAD_DIM), np.ones((DA_HEAD_DIM, DA_HEAD_DIM))), BF16)
    tab = pl.BlockSpec((ROW_TILE, LANES), lambda t: (t % tpb, 0))
    row = pl.BlockSpec((1, D), lambda t: (0, 0))
    out = pl.BlockSpec((ROW_TILE, D), lambda t: (t, 0))
    seg = lambda k: pl.BlockSpec((ROW_TILE, D), lambda t: (t, _PCOL_DA // D + k))
    return pl.pallas_call(
        _dapre_kernel, grid=(rows // ROW_TILE,),
        in_specs=[seg(0), seg(1), seg(2), row, row, tab, tab, pl.BlockSpec((256, 256), lambda t: (0, 0))],
        out_specs=[out, out, pl.BlockSpec((None, D, ROW_TILE), lambda t: (t // tpb, 0, t % tpb))],
        out_shape=[jax.ShapeDtypeStruct((rows, D), BF16)] * 2
        + [jax.ShapeDtypeStruct((rows // ROW_TILE // tpb, D, tpb * ROW_TILE), BF16)], name="dapre",
        compiler_params=_cp(("parallel",)))(p2, p2, p2, gq, gk, cos_t, sin_t, ones)


ATTN_KEY_BLOCK = 1024
ATTN_SAFE_SCORE = 30.0
LOG2E = 1.4426950408889634


def _attn_kernel(safe_ref, lam_ref, q_ref, k_ref, vt_ref, g_ref, o_ref, *, lam_init, ctx_rows, ctx_tile):
    lv = lam_ref[...]
    lam = (jnp.exp(jnp.sum(lv[0:1] * lv[1:2], keepdims=True)) - jnp.exp(jnp.sum(lv[2:3] * lv[3:4], keepdims=True))
           + lam_init)
    q = q_ref[...]
    tq = q.shape[0]
    lane_lo = lax.broadcasted_iota(jnp.int32, (1, LANES), 1) < DA_HEAD_DIM
    qcat = jnp.concatenate([jnp.where(lane_lo, q, jnp.zeros_like(q)), jnp.where(lane_lo, jnp.zeros_like(q), q)], 0)

    nt = (((1,), (1,)), ((), ()))

    def finish(ot):
        o = ot[:, :tq] - lam * ot[:, tq:]
        y = o * lax.rsqrt(jnp.mean(o * o, axis=0, keepdims=True) + EPS) * g_ref[...] * (1.0 - lam_init)
        o_ref[...] = y.T.astype(o_ref.dtype)

    def attend_bounded(nkeys):
        l = jnp.zeros((1, 2 * tq), F32)
        acc = jnp.zeros((LANES, 2 * tq), F32)
        blocks = [(k0, min(k0 + ATTN_KEY_BLOCK, nkeys)) for k0 in range(0, nkeys, ATTN_KEY_BLOCK)]
        scores = lambda b: lax.dot_general(k_ref[b[0]:b[1], :], qcat, nt, preferred_element_type=F32)
        st = scores(blocks[0])
        for j, (k0, k1) in enumerate(blocks):
            st_next = scores(blocks[j + 1]) if j + 1 < len(blocks) else None
            e = jnp.exp2(st)
            l = l + jnp.sum(e, axis=0, keepdims=True)
            acc = acc + _dot(vt_ref[:, k0:k1], e.astype(BF16))
            st = st_next
        finish(acc / l)

    def attend_shifted(nkeys):
        st = lax.dot_general(k_ref[0:nkeys, :], qcat, nt, preferred_element_type=F32)
        e = jnp.exp2(st - jnp.max(st, axis=0, keepdims=True))
        finish(_dot(vt_ref[:, 0:nkeys], e.astype(BF16)) / jnp.sum(e, axis=0, keepdims=True))

    def attend(nkeys):
        @pl.when(safe_ref[0] == 1)
        def _():
            attend_bounded(nkeys)

        @pl.when(safe_ref[0] != 1)
        def _():
            attend_shifted(nkeys)

    if ctx_tile:
        @pl.when(pl.program_id(2) == 0)
        def _():
            attend(ctx_rows)

        @pl.when(pl.program_id(2) > 0)
        def _():
            attend(k_ref.shape[0])
    else:
        attend(k_ref.shape[0])


def _da_attn(safe, lam4, qn, kn, vt, subg_col, lam_init, CTX, with_ctx):
    B, T, _ = qn.shape
    tq = ROW_TILE
    q0 = 0 if with_ctx else CTX // tq
    nq = T // tq - q0
    grid_spec = pltpu.PrefetchScalarGridSpec(
        num_scalar_prefetch=1, grid=(B, DA_HEADS, nq),
        in_specs=[pl.BlockSpec((4, DA_HEAD_DIM), lambda b, h, i, s: (0, 0)),
                  pl.BlockSpec((None, tq, LANES), lambda b, h, i, s: (b, q0 + i, h)),
                  pl.BlockSpec((None, T, LANES), lambda b, h, i, s: (b, 0, h)),
                  pl.BlockSpec((None, LANES, T), lambda b, h, i, s: (b, h, 0)),
                  pl.BlockSpec((LANES, 1), lambda b, h, i, s: (0, 0))],
        out_specs=pl.BlockSpec((None, tq, LANES), lambda b, h, i, s: (b, i, h)))
    return pl.pallas_call(
        partial(_attn_kernel, lam_init=lam_init, ctx_rows=CTX, ctx_tile=with_ctx), grid_spec=grid_spec,
        out_shape=jax.ShapeDtypeStruct((B, nq * tq, D), BF16), name="daattn",
        compiler_params=_cp(("parallel", "parallel", "parallel"), _VMEM_LIMIT))(safe, lam4, qn, kn, vt, subg_col)


def _merge_kernel(yh_ref, ys_ref, yd_ref, g0_ref, g1_ref, g2_ref, h_ref, wb_ref, wo_ref, gate_ref, n2_ref, sh_ref,
                  sc_ref, *rest, route):
    acc = None
    for i, (y_ref, gl_ref) in enumerate(((yh_ref, g0_ref), (ys_ref, g1_ref), (yd_ref, g2_ref))):
        term = jax.nn.sigmoid(gl_ref[...].astype(F32)) * _dot(y_ref[...], wb_ref[i])
        acc = term if acc is None else acc + term
    hn = h_ref[...] + gate_ref[...] * _dot(acc.astype(BF16), wo_ref[...])
    xn = hn * lax.rsqrt(jnp.mean(hn * hn, axis=-1, keepdims=True) + EPS) * n2_ref[...]
    fin = xn * (1.0 + sc_ref[...]) + sh_ref[...]
    if not route:
        hout_ref, fin_ref = rest
    else:
        rw_ref, hout_ref, fin_ref, rg_ref = rest
        logits = _dot(fin, rw_ref[...], HIGHEST)
        lane = lax.broadcasted_iota(jnp.int32, logits.shape, 1)
        logits = jnp.where(lane < N_EXPERTS, logits, -jnp.inf)
        m1 = jnp.max(logits, axis=1, keepdims=True)
        i1 = jnp.min(jnp.where(logits == m1, lane, LANES), axis=1, keepdims=True)
        rest_l = jnp.where(lane == i1, -jnp.inf, logits)
        m2 = jnp.max(rest_l, axis=1, keepdims=True)
        i2 = jnp.min(jnp.where(rest_l == m2, lane, LANES), axis=1, keepdims=True)
        e2 = jnp.exp(m2 - m1)
        rg_ref[...] = jnp.where(lane == i1, 1.0 / (1.0 + e2), 0.0) + jnp.where(lane == i2, e2 / (1.0 + e2), 0.0)
    hout_ref[...] = hn
    fin_ref[...] = fin.astype(fin_ref.dtype)


def _merge(yh, ys, yd, p2, h, wb, wo, mod3, n2g, tpb, latent_only, router_w=None):
    rows = h.shape[0]
    if latent_only:
        nt = rows // ROW_TILE // tpb * (tpb - 1)
        src = lambda t: t // (tpb - 1) * tpb + 1 + t % (tpb - 1)
    else:
        nt = rows // ROW_TILE
        src = lambda t: t
    tile = lambda cb: pl.BlockSpec((ROW_TILE, D), lambda t: (src(t), cb))
    mspec = lambda k: pl.BlockSpec((None, 1, D), lambda t: (_modrow(src(t), tpb) * 6 + k, 0, 0))
    in_specs = [tile(0), tile(0), pl.BlockSpec((ROW_TILE, D), lambda t: (t, 0)) if latent_only else tile(0),
                tile(_PCOL_GATE // D), tile(_PCOL_GATE // D + 1),
                tile(_PCOL_GATE // D + 2), tile(0), pl.BlockSpec((3, D, D), lambda t: (0, 0, 0)),
                pl.BlockSpec((D, D), lambda t: (0, 0)), mspec(2), pl.BlockSpec((1, D), lambda t: (0, 0)),
                mspec(3), mspec(4)]
    args = [yh, ys, yd, p2, p2, p2, h, wb, wo, mod3, n2g, mod3, mod3]
    otile = pl.BlockSpec((ROW_TILE, D), lambda t: (t, 0))
    out_specs = [otile, otile]
    out_shape = [jax.ShapeDtypeStruct((nt * ROW_TILE, D), F32), jax.ShapeDtypeStruct((nt * ROW_TILE, D), BF16)]
    if router_w is not None:
        in_specs.append(pl.BlockSpec((D, LANES), lambda t: (0, 0)))
        args.append(router_w)
        out_specs.append(pl.BlockSpec((ROW_TILE, LANES), lambda t: (t, 0)))
        out_shape.append(jax.ShapeDtypeStruct((nt * ROW_TILE, LANES), F32))
    return pl.pallas_call(
        partial(_merge_kernel, route=router_w is not None), grid=(nt,), in_specs=in_specs, out_specs=out_specs,
        out_shape=out_shape, name="merge", compiler_params=_cp(("parallel",), _VMEM_LIMIT))(*args)


def _ffn_kernel(x_ref, w1_ref, w3_ref, w2_ref, h_ref, gc_ref, gl_ref, o_ref, acc, *, ctx_rows, tiles_per_batch):
    f = pl.program_id(1)
    x = x_ref[...]
    a = (_silu(_dot(x, w1_ref[...])) * _dot(x, w3_ref[...])).astype(BF16)
    part = _dot(a, w2_ref[...])

    @pl.when(f == 0)
    def _():
        acc[...] = part

    @pl.when(f > 0)
    def _():
        acc[...] += part

    @pl.when(f == pl.num_programs(1) - 1)
    def _():
        tm = x_ref.shape[0]
        r = (pl.program_id(0) % tiles_per_batch) * tm + lax.broadcasted_iota(jnp.int32, (tm, 1), 0)
        gate = jnp.where(r < ctx_rows, gc_ref[...], gl_ref[...])
        o_ref[...] = h_ref[...] + gate * acc[...]


def _ffn(fin, w1, w3, w2, h, mod3, T, CTX):
    rows = h.shape[0]
    tpb = 4
    tm = T // tpb
    ff = w1.shape[1]
    tf = 512
    return pl.pallas_call(
        partial(_ffn_kernel, ctx_rows=CTX, tiles_per_batch=tpb), grid=(rows // tm, ff // tf),
        in_specs=[pl.BlockSpec((tm, D), lambda i, f: (i, 0)), pl.BlockSpec((D, tf), lambda i, f: (0, f)),
                  pl.BlockSpec((D, tf), lambda i, f: (0, f)), pl.BlockSpec((tf, D), lambda i, f: (f, 0)),
                  pl.BlockSpec((tm, D), lambda i, f: (i, 0)),
                  pl.BlockSpec((None, 1, D), lambda i, f: (5, 0, 0)),
                  pl.BlockSpec((None, 1, D), lambda i, f: ((1 + i // tpb) * 6 + 5, 0, 0))],
        out_specs=pl.BlockSpec((tm, D), lambda i, f: (i, 0)),
        out_shape=jax.ShapeDtypeStruct((rows, D), F32), scratch_shapes=[pltpu.VMEM((tm, D), F32)], name="ffn",
        compiler_params=_cp(("parallel", "arbitrary"), _VMEM_LIMIT))(fin, w1, w3, w2, h, mod3, mod3)


MOE_TILE = 1024
MOE_CHUNK = 256


def _slots_kernel(rg_ref, tri_ref, slot_ref, slott_ref, cnt_ref):
    blk = tri_ref.shape[0]
    carry = jnp.zeros((1, LANES), F32)
    for i in range(rg_ref.shape[0] // blk):
        sel = rg_ref[i * blk:(i + 1) * blk, :] > 0.0
        self32 = jnp.where(sel, 1.0, 0.0)
        rank = _dot(tri_ref[...], self32.astype(BF16)) + carry
        slot = jnp.where(sel, rank, -1.0)
        slot_ref[i * blk:(i + 1) * blk, :] = slot
        slott_ref[:, i * blk:(i + 1) * blk] = slot.T
        carry = carry + jnp.sum(self32, axis=0, keepdims=True)
    cnt_ref[...] = jnp.broadcast_to(carry, cnt_ref.shape)


def _moe_slots(rgate, tm):
    rows = rgate.shape[0]
    blk = 256
    tri = jnp.asarray(np.tril(np.ones((blk, blk), np.float32), -1), BF16)
    return pl.pallas_call(
        _slots_kernel, grid=(rows // tm,),
        in_specs=[pl.BlockSpec((tm, LANES), lambda i: (i, 0)), pl.BlockSpec((blk, blk), lambda i: (0, 0))],
        out_specs=[pl.BlockSpec((tm, LANES), lambda i: (i, 0)), pl.BlockSpec((LANES, tm), lambda i: (0, i)),
                   pl.BlockSpec((8, LANES), lambda i: (i, 0))],
        out_shape=[jax.ShapeDtypeStruct((rows, LANES), F32), jax.ShapeDtypeStruct((LANES, rows), F32),
                   jax.ShapeDtypeStruct((rows // tm * 8, LANES), F32)], name="moeslots",
        compiler_params=_cp(("parallel",)))(rgate, tri)


def _moe_kernel(nch_ref, x_ref, slott_ref, slot_ref, rg_ref, w1_ref, w3_ref, w2_ref, h_ref, gl_ref, o_ref,
                xe_ref, ye_ref):
    i, e, f = pl.program_id(0), pl.program_id(1), pl.program_id(2)
    CH = MOE_CHUNK
    nch = nch_ref[i * N_EXPERTS + e]

    @pl.when(jnp.logical_and(e == 0, f == 0))
    def _():
        o_ref[...] = h_ref[...]

    def expert(c):
        xe = xe_ref[c]
        a = (_silu(_dot(xe, w1_ref[...])) * _dot(xe, w3_ref[...])).astype(BF16)
        return _dot(a, w2_ref[...])

    @pl.when(f == 0)
    def _():
        def chunk(c, carry):
            want = lax.broadcasted_iota(jnp.int32, (CH, 1), 0).astype(F32) + (c * CH).astype(F32)
            pick = jnp.where(slott_ref[pl.ds(e, 1), :] == want, 1.0, 0.0).astype(BF16)
            xe_ref[c] = _dot(pick, x_ref[...]).astype(BF16)
            ye_ref[c] = expert(c)
            return carry

        lax.fori_loop(0, nch, chunk, 0)

    @pl.when(f == 1)
    def _():
        mine = lax.broadcasted_iota(jnp.int32, (1, LANES), 1) == e
        scol = jnp.sum(jnp.where(mine, slot_ref[...], 0.0), axis=1, keepdims=True)
        scale = jnp.sum(jnp.where(mine, rg_ref[...], 0.0), axis=1, keepdims=True) * gl_ref[...]

        def chunk(c, carry):
            ye = (ye_ref[c] + expert(c)).astype(BF16)
            want = lax.broadcasted_iota(jnp.int32, (1, CH), 1).astype(F32) + (c * CH).astype(F32)
            put = jnp.where(scol == want, 1.0, 0.0).astype(BF16)
            o_ref[...] += scale * _dot(put, ye)
            return carry

        lax.fori_loop(0, nch, chunk, 0)


def _moe(fin, rgate, w1, w3, w2, h, mod3, SEQ):
    rows = h.shape[0]
    tm = min(SEQ, MOE_TILE)
    ff = w1.shape[2]
    tf = ff // 2
    tiles_per_batch = SEQ // tm
    nt = rows // tm
    slot, slott, cnt = _moe_slots(rgate, tm)
    counts = cnt.reshape(nt, 8, LANES)[:, 0, :N_EXPERTS].astype(jnp.int32)
    nch = ((counts + MOE_CHUNK - 1) // MOE_CHUNK).reshape(nt * N_EXPERTS)
    grid_spec = pltpu.PrefetchScalarGridSpec(
        num_scalar_prefetch=1, grid=(nt, N_EXPERTS, 2),
        in_specs=[pl.BlockSpec((tm, D), lambda i, e, f, n: (i, 0)),
                  pl.BlockSpec((8, tm), lambda i, e, f, n: (0, i)),
                  pl.BlockSpec((tm, LANES), lambda i, e, f, n: (i, 0)),
                  pl.BlockSpec((tm, LANES), lambda i, e, f, n: (i, 0)),
                  pl.BlockSpec((None, D, tf), lambda i, e, f, n: (e, 0, f)),
                  pl.BlockSpec((None, D, tf), lambda i, e, f, n: (e, 0, f)),
                  pl.BlockSpec((None, tf, D), lambda i, e, f, n: (e, f, 0)),
                  pl.BlockSpec((tm, D), lambda i, e, f, n: (i, 0)),
                  pl.BlockSpec((None, 1, D), lambda i, e, f, n: ((1 + i // tiles_per_batch) * 6 + 5, 0, 0))],
        out_specs=pl.BlockSpec((tm, D), lambda i, e, f, n: (i, 0)),
        scratch_shapes=[pltpu.VMEM((tm // MOE_CHUNK, MOE_CHUNK, D), BF16),
                        pltpu.VMEM((tm // MOE_CHUNK, MOE_CHUNK, D), F32)])
    return pl.pallas_call(
        _moe_kernel, grid_spec=grid_spec, out_shape=jax.ShapeDtypeStruct((rows, D), F32), name="moe",
        compiler_params=_cp(("parallel", "arbitrary", "arbitrary"), _VMEM_LIMIT))(
            nch, fin, slott, slot, rgate, w1, w3, w2, h, mod3)


def _regroup_cols(w):
    hy, ssd, da, gate = 3072, 2592, 3072, 3072
    parts = [w[..., :hy], w[..., hy + ssd + da:hy + ssd + da + gate], w[..., hy + ssd:hy + ssd + da],
             w[..., hy:hy + ssd]]
    pad = jnp.zeros(w.shape[:-1] + (_PCOLS - hy - ssd - da - gate,), w.dtype)
    return jnp.concatenate(parts + [pad], axis=-1)


def kernel(x, c, ctx, c_ctx, w_mod, b_mod, norm1_g, norm2_g, w_in, hy_conv_w, hy_conv_b, hy_w1, hy_b1, hy_w2, hy_b2,
           hy_w3, hy_b3, hy_w4, hy_freq, hy_bias, ssd_conv_w, ssd_conv_b, ssd_dt_bias, ssd_a_log, ssd_d, ssd_norm_g,
           da_q_norm, da_k_norm, da_lambda, da_subln_g, w_branch, w_out, ffn_w1, ffn_w3, ffn_w2, router_w, moe_w1,
           moe_w3, moe_w2):
    B, SEQ, _ = x.shape
    CTX = ctx.shape[1]
    assert CTX == ROW_TILE and SEQ % ROW_TILE == 0 and B % 2 == 0 and B + 1 <= 8
    T = CTX + SEQ
    tpb = T // ROW_TILE
    depth = w_in.shape[0]
    R = B * T

    h = jnp.concatenate([ctx, x], axis=1).reshape(R, D)
    cvec = jnp.zeros((8, D), F32).at[0].set(c_ctx).at[1:1 + B].set(c)
    cos_t, sin_t = _rope_tables(SEQ, CTX)
    tabs_l, tabs_c = _fft_tables(SEQ), _fft_tables(CTX)
    nb = D // LANES

    for i in range(depth):
        need_ctx = i < depth - 1
        mod = _mod(cvec, w_mod[i], b_mod[i][None])
        mod3 = mod.reshape(8 * 6, 1, D)
        xn = _normmod(h, norm1_g[i][None], mod3, tpb)
        p2 = _matmul(xn, _regroup_cols(w_in[i].astype(BF16)), T // 4, 1024, BF16)
        p3 = p2.reshape(B, T, _PCOLS)

        cw, cb = hy_conv_w[i], hy_conv_b[i][None]
        hy_args = (hy_w1[i], hy_b1[i], hy_w2[i], hy_b2[i], hy_w3[i], hy_b3[i], hy_w4[i], hy_freq[i])
        kh = _hy_kfft(_hy_filters(SEQ, *hy_args), hy_bias[i], tabs_l, SEQ)
        z1 = _hy_conv(p3, 0, p3, nb, cw, cb, kh, 0, tabs_l, SEQ, CTX, SEQ, 0, True)
        y_hy = _hy_conv(z1, 0, p3, 2 * nb, cw, cb, kh, 1, tabs_l, SEQ, CTX, SEQ, CTX, False)
        if need_ctx:
            khc = _hy_kfft(_hy_filters(CTX, *hy_args), hy_bias[i], tabs_c, CTX)
            z1c = _hy_conv(p3, 0, p3, nb, cw, cb, khc, 0, tabs_c, CTX, 0, CTX, 0, True)
            y_hy = _hy_conv(z1c, 0, p3, 2 * nb, cw, cb, khc, 1, tabs_c, CTX, 0, CTX, 0, False, alias_into=y_hy)

        xbc = _ssd_pre(p3, ssd_conv_w[i], ssd_conv_b[i][None], CTX)
        dtb = jnp.pad(ssd_dt_bias[i].reshape(1, 2 * SSD_HEADS), ((0, 0), (0, LANES - 2 * SSD_HEADS)))
        arow = jnp.pad(-jnp.exp(ssd_a_log[i].astype(F32)).reshape(1, 2 * SSD_HEADS),
                       ((0, 0), (0, LANES - 2 * SSD_HEADS)))
        y_rev = _ssd_scan(xbc, p3, dtb, arow, True, CTX)
        dskip = jnp.repeat(ssd_d[i].astype(F32), SSD_HEADDIM)[None]
        y_ssd = _ssd_scan(xbc, p3, dtb, arow, False, CTX, fin=(y_rev, dskip, ssd_norm_g[i][None]))

        lam_init = 0.8 - 0.6 * math.exp(-0.3 * (i + 1))
        gq = jnp.tile(da_q_norm[i], D // DA_HEAD_DIM)[None]
        gk = jnp.tile(da_k_norm[i], D // DA_HEAD_DIM)[None]
        qn, kn, vt = _da_pre(p2, gq, gk, cos_t, sin_t, tpb)
        score_bound = 8.0 * jnp.max(jnp.abs(da_q_norm[i])) * jnp.max(jnp.abs(da_k_norm[i]))
        safe = (score_bound <= ATTN_SAFE_SCORE).astype(jnp.int32).reshape(1)
        y_da = _da_attn(safe, da_lambda[i], qn.reshape(B, T, D), kn.reshape(B, T, D), vt, da_subln_g[i][:, None],
                        lam_init, CTX, need_ctx)

        wb, wo = w_branch[i].astype(BF16), w_out[i].astype(BF16)
        flat = lambda a: a.reshape(-1, D)
        j = i // 2
        if i % 2 == 0:
            h, fin = _merge(flat(y_hy), flat(y_ssd), flat(y_da), p2, h, wb, wo, mod3, norm2_g[i][None], tpb,
                            latent_only=not need_ctx)
            assert need_ctx, "dense FFN layers are expected to carry context rows"
            h = _ffn(fin, ffn_w1[j].astype(BF16), ffn_w3[j].astype(BF16), ffn_w2[j].astype(BF16), h, mod3, T, CTX)
        else:
            assert not need_ctx, "expert layers are expected to be latent only"
            rw = jnp.pad(router_w[j], ((0, 0), (0, LANES - N_EXPERTS)))
            h, fin, rgate = _merge(flat(y_hy), flat(y_ssd), flat(y_da), p2, h, wb, wo, mod3, norm2_g[i][None], tpb,
                                   latent_only=True, router_w=rw)
            h = _moe(fin, rgate, moe_w1[j].astype(BF16), moe_w3[j].astype(BF16), moe_w2[j].astype(BF16), h, mod3,
                     SEQ)
    return h.reshape(B, SEQ, D)
```

```python
import math
from functools import partial

import numpy as np
import jax
import jax.numpy as jnp
from jax import lax
from jax.experimental import pallas as pl
from jax.experimental.pallas import tpu as pltpu

F32, BF16 = jnp.float32, jnp.bfloat16
HIGHEST = lax.Precision.HIGHEST

D = 1024
EPS = 1e-6
GRID_W = 64
ROW_TILE = 256
LANES = 128
SSD_CHUNK = 128
SSD_HEADS = 16
SSD_HEADDIM = 64
DA_HEADS = 8
DA_HEAD_DIM = 64
N_EXPERTS = 8
HY_BANDS = 16
HY_EMB = 2 * HY_BANDS + 1
HY_FFN = 64

_PCOL_HY = 0
_PCOL_GATE = 3072
_PCOL_DA = 6144
_PCOL_Z = 9216
_PCOL_XBC = 10240
_PCOL_DT = 11776
_PCOLS = 12288
_VMEM_LIMIT = 56 * 1024 * 1024


def _cp(sem, vmem=None):
    return pltpu.CompilerParams(dimension_semantics=sem, vmem_limit_bytes=vmem)


def _dot(a, b, prec=None):
    return jnp.dot(a, b, preferred_element_type=F32, precision=prec)


def _silu(x):
    return x * jax.nn.sigmoid(x)


def _mod_kernel(c_ref, w_ref, b_ref, o_ref):
    o_ref[...] = _dot(_silu(c_ref[...]), w_ref[...], HIGHEST) + b_ref[...]


def _mod(cvec, w, b):
    return pl.pallas_call(
        _mod_kernel, grid=(6,),
        in_specs=[pl.BlockSpec((8, D), lambda j: (0, 0)), pl.BlockSpec((D, D), lambda j: (0, j)),
                  pl.BlockSpec((1, D), lambda j: (0, j))],
        out_specs=pl.BlockSpec((8, D), lambda j: (0, j)),
        out_shape=jax.ShapeDtypeStruct((8, 6 * D), F32), name="mod")(cvec, w, b)


def _modrow(t, tpb):
    return jnp.where(t % tpb == 0, 0, 1 + t // tpb)


def _mod_spec(k, tpb):
    return pl.BlockSpec((None, 1, D), lambda t: (_modrow(t, tpb) * 6 + k, 0, 0))


def _normmod_kernel(h_ref, g_ref, sh_ref, sc_ref, o_ref):
    x = h_ref[...]
    y = x * lax.rsqrt(jnp.mean(x * x, axis=-1, keepdims=True) + EPS) * g_ref[...]
    o_ref[...] = (y * (1.0 + sc_ref[...]) + sh_ref[...]).astype(o_ref.dtype)


def _normmod(h, g, mod3, tpb):
    rows = h.shape[0]
    return pl.pallas_call(
        _normmod_kernel, grid=(rows // ROW_TILE,),
        in_specs=[pl.BlockSpec((ROW_TILE, D), lambda t: (t, 0)), pl.BlockSpec((1, D), lambda t: (0, 0)),
                  _mod_spec(0, tpb), _mod_spec(1, tpb)],
        out_specs=pl.BlockSpec((ROW_TILE, D), lambda t: (t, 0)),
        out_shape=jax.ShapeDtypeStruct((rows, D), BF16), name="normmod",
        compiler_params=_cp(("parallel",)))(h, g, mod3, mod3)


def _mm_kernel(a_ref, b_ref, o_ref):
    o_ref[...] = _dot(a_ref[...], b_ref[...]).astype(o_ref.dtype)


def _matmul(a, b, tm, tn, out_dtype):
    m, k = a.shape
    n = b.shape[1]
    return pl.pallas_call(
        _mm_kernel, grid=(n // tn, m // tm),
        in_specs=[pl.BlockSpec((tm, k), lambda j, i: (i, 0)), pl.BlockSpec((k, tn), lambda j, i: (0, j))],
        out_specs=pl.BlockSpec((tm, tn), lambda j, i: (i, j)),
        out_shape=jax.ShapeDtypeStruct((m, n), out_dtype), name="inproj",
        compiler_params=_cp(("parallel", "parallel"), _VMEM_LIMIT))(a, b)


def _hyfilt_kernel(f_ref, w1, b1, w2, b2, w3, b3, w4, fr_ref, dl_ref, o_ref, h_ref, win_ref, *, shift):
    @pl.when(pl.program_id(1) == 0)
    def _():
        f = f_ref[...]
        fr = fr_ref[...]
        h = jnp.sin(fr[0:1] * (_dot(f, w1[...], HIGHEST) + b1[...]))
        h = jnp.sin(fr[1:2] * (_dot(h, w2[...], HIGHEST) + b2[...]))
        h = jnp.sin(fr[2:3] * (_dot(h, w3[...], HIGHEST) + b3[...]))
        h_ref[...] = h.astype(BF16)
        win_ref[...] = jnp.exp(-f[:, 0:1] * dl_ref[...]) + shift

    o_ref[...] = _dot(h_ref[...], w4[...]) * win_ref[...]


def _hy_filters(L, w1, b1, w2, b2, w3, b3, w4, freq):
    t = jnp.linspace(0.0, 1.0, L, dtype=F32)[:, None]
    w = 2.0 * math.pi * jnp.arange(L, dtype=F32)[:, None] / L
    f = jnp.linspace(1e-4, HY_BANDS - 1, HY_BANDS, dtype=F32)[None, :]
    feats = jnp.concatenate([t, jnp.cos(f * w), -jnp.sin(f * w)], axis=-1)
    feats = jnp.pad(feats, ((0, 0), (0, HY_FFN - HY_EMB)))
    w1p = jnp.pad(w1, ((0, HY_FFN - HY_EMB), (0, 0)))
    max_decay = math.log(1e-2) / 0.3
    min_decay = math.log(1e-2) / 1.5
    deltas = jnp.abs(jnp.linspace(min_decay, max_decay, D, dtype=F32))[None, :]
    tl = min(L, 512)
    full = lambda r, c: pl.BlockSpec((r, c), lambda i, j: (0, 0))
    return pl.pallas_call(
        partial(_hyfilt_kernel, shift=0.05), grid=(L // tl, 4),
        in_specs=[pl.BlockSpec((tl, HY_FFN), lambda i, j: (i, 0)), full(HY_FFN, HY_FFN), full(1, HY_FFN),
                  full(HY_FFN, HY_FFN), full(1, HY_FFN), full(HY_FFN, HY_FFN), full(1, HY_FFN),
                  pl.BlockSpec((HY_FFN, D), lambda i, j: (0, j)), full(3, HY_FFN), full(1, D)],
        out_specs=pl.BlockSpec((tl, D), lambda i, j: (i, j)),
        out_shape=jax.ShapeDtypeStruct((L, 4 * D), F32), name="hyfilt",
        scratch_shapes=[pltpu.VMEM((tl, HY_FFN), BF16), pltpu.VMEM((tl, D), F32)],
        compiler_params=_cp(("parallel", "arbitrary")))(
            feats, w1p, b1[None], w2, b2[None], w3, b3[None], w4.astype(BF16), freq, deltas)


def _fft_split(L):
    n = 2 * L
    n2 = 128 if L >= 2048 else n // 32
    return n // n2, n2


def _fft_tables(L):
    n1_, n2_ = _fft_split(L)
    n, h1 = 2 * L, n1_ // 2
    n2 = np.arange(n2_)[:, None, None]
    k1 = np.arange(n1_)[None, :, None]
    n1 = np.arange(h1)[None, None, :]
    g = np.exp(-2j * np.pi * (n2 * k1 / n + n1 * k1 / n1_))
    ma = np.concatenate([np.concatenate([g.real, -g.imag], 2), np.concatenate([g.imag, g.real], 2)], 1)
    kk = np.arange(n2_)
    f = np.exp(-2j * np.pi * np.outer(kk, kk) / n2_)
    mc = np.block([[f.real, -f.imag], [f.imag, f.real]])
    mci = np.block([[f.real, f.imag], [-f.imag, f.real]])
    gi = np.exp(2j * np.pi * (n2 * np.arange(n1_)[None, None, :] / n
                              + np.arange(h1)[None, :, None] * np.arange(n1_)[None, None, :] / n1_)) / n
    mai = np.concatenate([np.concatenate([gi.real, -gi.imag], 2), np.concatenate([gi.imag, gi.real], 2)], 1)
    cast = lambda a: jnp.asarray(a, F32).astype(BF16)
    return cast(ma), cast(ma[:, :, :h1]), cast(mc), cast(mci), cast(mai)


def _kfft_kernel(kf_ref, kb_ref, bias_ref, mar_ref, mc_ref, o_ref, af_ref, ab_ref, *, N1, N2):
    H1, PA = N1 // 2, 2 * N1 + 8

    def stage_a(n2, c):
        for src, dst in ((kf_ref, af_ref), (kb_ref, ab_ref)):
            slab = src[pl.ds(n2, H1, stride=N2), :].astype(BF16)
            dst[pl.ds(pl.multiple_of(n2 * PA, 8), 2 * N1), :] = _dot(mar_ref[n2], slab)
        return c

    lax.fori_loop(0, N2, stage_a, 0, unroll=8)
    bias = bias_ref[...]

    def stage_c(k1, c):
        def rows(ref):
            return jnp.concatenate([ref[pl.ds(k1, N2, stride=PA), :], ref[pl.ds(N1 + k1, N2, stride=PA), :]], 0)
        x = _dot(mc_ref[...], jnp.concatenate([rows(af_ref), rows(ab_ref)], 1).astype(BF16))
        xf, xb = x[:, :LANES], x[:, LANES:]
        o_ref[k1] = jnp.concatenate([xf[:N2] + xb[:N2] + bias, xf[N2:] - xb[N2:]], 0).astype(o_ref.dtype)
        return c

    lax.fori_loop(0, N1, stage_c, 0, unroll=16)


def _hy_kfft(k4, bias, tabs, L):
    N1, N2 = _fft_split(L)
    _, mar, mc, _, _ = tabs
    nb = D // LANES
    return pl.pallas_call(
        partial(_kfft_kernel, N1=N1, N2=N2), grid=(2, nb),
        in_specs=[pl.BlockSpec((L, LANES), lambda o, j: (0, o * 2 * nb + j)),
                  pl.BlockSpec((L, LANES), lambda o, j: (0, o * 2 * nb + nb + j)),
                  pl.BlockSpec((None, 1, LANES), lambda o, j: (o, 0, j)),
                  pl.BlockSpec(mar.shape, lambda o, j: (0, 0, 0)), pl.BlockSpec(mc.shape, lambda o, j: (0, 0))],
        out_specs=pl.BlockSpec((None, N1, 2 * N2, LANES), lambda o, j: (o, 0, 0, j)),
        out_shape=jax.ShapeDtypeStruct((2, N1, 2 * N2, D), BF16),
        scratch_shapes=[pltpu.VMEM((N2 * (2 * N1 + 8), LANES), F32)] * 2, name="hykfft",
        compiler_params=_cp(("parallel", "parallel"), _VMEM_LIMIT))(k4, k4, bias[:, None, :], mar, mc)


def _hyconv_kernel(sig_ref, gate_ref, cws_ref, cbs_ref, cwg_ref, cbg_ref, kh_ref, ma_ref, mc_ref, mci_ref,
                   mai_ref, *rest, L, N1, N2, in_row0, out_row0, sig_conv, aliased):
    if aliased:
        rest = rest[1:]
    o_ref, sig0, sig1, gat0, gat1, a_ref = rest
    H1, PA = N1 // 2, 2 * N1 + 8
    NP, PADR = N2 + 8, 16
    sigs, gats = (sig0, sig1), (gat0, gat1)

    def stage(ref, nat, bb, r0):
        nat[0:PADR, :] = jnp.zeros((PADR, LANES), F32)
        nat[PADR + H1 * NP:PADR + H1 * NP + 8, :] = jnp.zeros((8, LANES), F32)
        for n1 in range(H1):
            nat[PADR + n1 * NP:PADR + n1 * NP + N2, :] = ref[bb, r0 + n1 * N2:r0 + (n1 + 1) * N2, :].astype(F32)

    def slab(nat, n2, w, b):
        u = nat[pl.ds(PADR + n2, H1, stride=NP), :]
        if w is None:
            return u
        prev = jnp.where(n2 == 0, PADR - NP + N2 - 1, PADR + n2 - 1)
        nxt = jnp.where(n2 == N2 - 1, PADR + NP, PADR + n2 + 1)
        return (w[0:1] * nat[pl.ds(prev, H1, stride=NP), :] + w[1:2] * u
                + w[2:3] * nat[pl.ds(nxt, H1, stride=NP), :] + b)

    for bb in range(2):
        stage(sig_ref, sigs[bb], bb, in_row0 if sig_conv else 0)
        stage(gate_ref, gats[bb], bb, in_row0)
    ws, bs = (cws_ref[...], cbs_ref[...]) if sig_conv else (None, None)
    wg, bg = cwg_ref[...], cbg_ref[...]

    def stage_a(n2, c):
        st = jnp.concatenate([slab(sigs[0], n2, ws, bs), slab(sigs[1], n2, ws, bs)], 0).astype(BF16)
        a_ref[pl.ds(pl.multiple_of(n2 * PA, 8), 2 * N1), :] = _dot(ma_ref[n2], st)
        return c

    lax.fori_loop(0, N2, stage_a, 0, unroll=16)

    def stage_c(i, c):
        k1 = 2 * i

        def rows(k):
            return jnp.concatenate([a_ref[pl.ds(k, N2, stride=PA), :], a_ref[pl.ds(N1 + k, N2, stride=PA), :]], 0)

        x = _dot(mc_ref[...], jnp.concatenate([rows(k1), rows(k1 + 1)], 1).astype(BF16))
        kh = jnp.concatenate([kh_ref[k1], kh_ref[k1 + 1]], 1).astype(F32)
        xr, xi, kr, ki = x[:N2], x[N2:], kh[:N2], kh[N2:]
        y = jnp.concatenate([xr * kr - xi * ki, xr * ki + xi * kr], 0)
        cc = _dot(mci_ref[...], y.astype(BF16))
        for j in range(2):
            a_ref[pl.ds(k1 + j, N2, stride=PA), :] = cc[:N2, j * LANES:(j + 1) * LANES]
            a_ref[pl.ds(N1 + k1 + j, N2, stride=PA), :] = cc[N2:, j * LANES:(j + 1) * LANES]
        return c

    lax.fori_loop(0, N1 // 2, stage_c, 0, unroll=16)

    def stage_ai(n2, c):
        st = a_ref[pl.ds(pl.multiple_of(n2 * PA, 8), 2 * N1), :]
        g = jnp.concatenate([slab(gats[0], n2, wg, bg), slab(gats[1], n2, wg, bg)], 0)
        r = _dot(mai_ref[n2], st.astype(BF16)) * g
        sig0[pl.ds(PADR + n2, H1, stride=NP), :] = r[:H1]
        sig1[pl.ds(PADR + n2, H1, stride=NP), :] = r[H1:]
        return c

    lax.fori_loop(0, N2, stage_ai, 0, unroll=16)

    for bb in range(2):
        if out_row0:
            o_ref[bb, 0:out_row0, :] = jnp.zeros((out_row0, LANES), o_ref.dtype)
        for n1 in range(H1):
            o_ref[bb, out_row0 + n1 * N2:out_row0 + (n1 + 1) * N2, :] = (
                sigs[bb][PADR + n1 * NP:PADR + n1 * NP + N2, :].astype(o_ref.dtype))


def _hy_conv(sig, sig_col, gate, gate_col, cw, cb, kh, order, tabs, L, in_row0, out_rows, out_row0, sig_conv,
             alias_into=None):
    N1, N2 = _fft_split(L)
    ma, _, mc, mci, mai = tabs
    H1 = N1 // 2
    B = gate.shape[0]
    nb = D // LANES
    scol = sig_col if sig_conv else 0
    cspec = lambda r, c0: pl.BlockSpec((r, LANES), lambda j, q: (0, c0 + j))
    const = lambda a: pl.BlockSpec(a.shape, lambda j, q: (0,) * a.ndim)
    in_specs = [pl.BlockSpec((2, sig.shape[1], LANES), lambda j, q: (q, 0, sig_col + j)),
                pl.BlockSpec((2, gate.shape[1], LANES), lambda j, q: (q, 0, gate_col + j)),
                cspec(3, scol), cspec(1, scol), cspec(3, gate_col), cspec(1, gate_col),
                pl.BlockSpec((None, N1, 2 * N2, LANES), lambda j, q: (order, 0, 0, j)),
                const(ma), const(mc), const(mci), const(mai)]
    args = [sig, gate, cw, cb, cw, cb, kh, ma, mc, mci, mai]
    aliases = {}
    if alias_into is not None:
        in_specs.append(pl.BlockSpec(memory_space=pl.ANY))
        args.append(alias_into)
        aliases = {len(args) - 1: 0}
        out_total = alias_into.shape[1]
    else:
        out_total = out_row0 + out_rows
    return pl.pallas_call(
        partial(_hyconv_kernel, L=L, N1=N1, N2=N2, in_row0=in_row0, out_row0=out_row0, sig_conv=sig_conv,
                aliased=alias_into is not None),
        grid=(nb, B // 2), in_specs=in_specs,
        out_specs=pl.BlockSpec((2, out_row0 + out_rows, LANES), lambda j, q: (q, 0, j)),
        out_shape=jax.ShapeDtypeStruct((B, out_total, D), BF16),
        scratch_shapes=[pltpu.VMEM((H1 * (N2 + 8) + 24, LANES), F32)] * 4
        + [pltpu.VMEM((N2 * (2 * N1 + 8), LANES), F32)],
        input_output_aliases=aliases, name="hyconv",
        compiler_params=_cp(("parallel", "parallel"), _VMEM_LIMIT))(*args)


def _ssdpre_kernel(x_ref, w_ref, b_ref, o_ref, nat, *, T, CTX):
    W = x_ref.shape[1]
    nat[0:8, :] = jnp.zeros((8, W), F32)
    nat[8 + T:16 + T, :] = jnp.zeros((8, W), F32)
    for i in range(T // ROW_TILE):
        nat[8 + i * ROW_TILE:8 + (i + 1) * ROW_TILE, :] = x_ref[i * ROW_TILE:(i + 1) * ROW_TILE, :].astype(F32)
    w = w_ref[...]
    b = b_ref[...]
    row = lax.broadcasted_iota(jnp.int32, (ROW_TILE, 1), 0)
    for i in range(T // ROW_TILE):
        r0 = 8 + i * ROW_TILE
        xm = nat[r0 - 1:r0 - 1 + ROW_TILE, :]
        xp = nat[r0 + 1:r0 + 1 + ROW_TILE, :]
        if i * ROW_TILE == CTX:
            xm = jnp.where(row == 0, 0.0, xm)
        if (i + 1) * ROW_TILE == CTX:
            xp = jnp.where(row == ROW_TILE - 1, 0.0, xp)
        u = w[0:1] * xm + w[1:2] * nat[r0:r0 + ROW_TILE, :] + w[2:3] * xp + b
        o_ref[i * ROW_TILE:(i + 1) * ROW_TILE, :] = _silu(u).astype(o_ref.dtype)


def _ssd_pre(p3, cw, cb, CTX):
    B, T, _ = p3.shape
    W = 256
    nblk = cw.shape[1] // W
    return pl.pallas_call(
        partial(_ssdpre_kernel, T=T, CTX=CTX), grid=(B, nblk),
        in_specs=[pl.BlockSpec((None, T, W), lambda b, j: (b, 0, _PCOL_XBC // W + j)),
                  pl.BlockSpec((3, W), lambda b, j: (0, j)), pl.BlockSpec((1, W), lambda b, j: (0, j))],
        out_specs=pl.BlockSpec((None, T, W), lambda b, j: (b, 0, j)),
        out_shape=jax.ShapeDtypeStruct((B, T, cw.shape[1]), BF16),
        scratch_shapes=[pltpu.VMEM((T + 16, W), F32)], name="ssdpre",
        compiler_params=_cp(("parallel", "parallel")))(p3, cw, cb)


def _ssdscan_kernel(xbc_ref, dt_ref, dtb_ref, a_ref, tri_ref, *rest, rev):
    if rev:
        o_ref, s_ref = rest
    else:
        z_ref, yr_ref, dsk_ref, ng_ref, o_ref, s_ref, ybuf = rest
    Q = SSD_CHUNK

    @pl.when(pl.program_id(1) == 0)
    def _():
        s_ref[...] = jnp.zeros(s_ref.shape, F32)

    x = xbc_ref[...]
    xs, Bm, Cm = x[:, :D], x[:, D:D + 2 * LANES], x[:, D + 2 * LANES:]
    dtr = dt_ref[...].astype(F32) + dtb_ref[...]
    dt = jnp.maximum(dtr, 0.0) + jnp.log(1.0 + jnp.exp(-jnp.abs(dtr)))
    tri = tri_ref[...]
    mask = tri > 0.5
    cs = _dot(tri, dt * a_ref[...], HIGHEST)
    csT = cs.T
    dtT = dt.T
    last = 0 if rev else Q - 1
    col0 = SSD_HEADS if rev else 0
    lane_lo = lax.broadcasted_iota(jnp.int32, (1, LANES), 1) < SSD_HEADDIM
    for g in range(2):
        Bg = Bm[:, g * LANES:(g + 1) * LANES]
        Cg = Cm[:, g * LANES:(g + 1) * LANES]
        CB = lax.dot_general(Cg, Bg, (((1,), (1,)), ((), ())), preferred_element_type=F32)
        BgT = Bg.astype(F32).T
        Cgf = Cg.astype(F32)
        for pr in range(4):
            hp = g * 4 + pr
            xs_pair = xs[:, hp * LANES:(hp + 1) * LANES]
            Sp = s_ref[hp]
            rhs = jnp.concatenate([xs_pair, Sp.astype(BF16)], 0)
            ys, sts = [], []
            for hh in range(2):
                col = col0 + 2 * hp + hh
                csl = jnp.broadcast_to(cs[:, col:col + 1], (Q, Q))
                csr = csT[col:col + 1, :]
                dtr_row = dtT[col:col + 1, :]
                Lm = jnp.exp(jnp.where(mask, csl - csr, -1e30))
                lhs = jnp.concatenate([CB * Lm * dtr_row, Cgf * jnp.exp(csl)], 1).astype(BF16)
                ys.append(_dot(lhs, rhs))
                tot = csT[col:col + 1, last:last + 1]
                w_row = jnp.exp(tot - csr) * dtr_row
                st = _dot((BgT * w_row).astype(BF16), xs_pair)
                sts.append(jnp.exp(tot) * Sp + st)
            y_pair = jnp.where(lane_lo, ys[0], ys[1])
            s_ref[hp] = jnp.where(lane_lo, sts[0], sts[1])
            sl = slice(hp * LANES, (hp + 1) * LANES)
            if rev:
                o_ref[:, sl] = y_pair.astype(o_ref.dtype)
            else:
                ybuf[:, sl] = y_pair + yr_ref[:, sl].astype(F32) + xs_pair.astype(F32) * dsk_ref[:, sl]
    if not rev:
        y = ybuf[...] * _silu(z_ref[...].astype(F32))
        half = D // 2
        parts = []
        for g in range(2):
            yg = y[:, g * half:(g + 1) * half]
            parts.append(yg * lax.rsqrt(jnp.mean(yg * yg, axis=-1, keepdims=True) + EPS))
        o_ref[...] = (jnp.concatenate(parts, 1) * ng_ref[...]).astype(o_ref.dtype)


def _ssd_scan(xbc, p3, dtb, arow, rev, CTX, fin=None):
    B, T, _ = xbc.shape
    Q = SSD_CHUNK
    nch, nc = T // Q, CTX // Q
    if rev:
        chunk = lambda s: jnp.where(s < nc, nc - 1 - s, nch - 1 + nc - s)
        tri = jnp.asarray(np.triu(np.ones((Q, Q), np.float32)))
    else:
        chunk = lambda s: s
        tri = jnp.asarray(np.tril(np.ones((Q, Q), np.float32)))
    row = lambda w: pl.BlockSpec((1, w), lambda b, s: (0, 0))
    in_specs = [pl.BlockSpec((None, Q, xbc.shape[2]), lambda b, s: (b, chunk(s), 0)),
                pl.BlockSpec((None, Q, LANES), lambda b, s: (b, chunk(s), _PCOL_DT // LANES)),
                row(LANES), row(LANES), pl.BlockSpec((Q, Q), lambda b, s: (0, 0))]
    args = [xbc, p3, dtb, arow, tri]
    scratch = [pltpu.VMEM((SSD_HEADS // 2, LANES, LANES), F32)]
    if not rev:
        yrev, dskip, ng = fin
        in_specs += [pl.BlockSpec((None, Q, D), lambda b, s: (b, s, _PCOL_Z // D)),
                     pl.BlockSpec((None, Q, D), lambda b, s: (b, s, 0)), row(D), row(D)]
        args += [p3, yrev, dskip, ng]
        scratch.append(pltpu.VMEM((Q, D), F32))
    return pl.pallas_call(
        partial(_ssdscan_kernel, rev=rev), grid=(B, nch), in_specs=in_specs,
        out_specs=pl.BlockSpec((None, Q, D), lambda b, s: (b, chunk(s), 0)),
        out_shape=jax.ShapeDtypeStruct((B, T, D), BF16), scratch_shapes=scratch,
        name="ssdrev" if rev else "ssdfwd",
        compiler_params=_cp(("parallel", "arbitrary")))(*args)


def _dapre_kernel(q_ref, k_ref, v_ref, gq_ref, gk_ref, cos_ref, sin_ref, ones_ref, qo_ref, ko_ref, vt_ref):
    ones = ones_ref[...]
    cosf = jnp.concatenate([cos_ref[...]] * (D // LANES), 1)
    sinf = jnp.concatenate([sin_ref[...]] * (D // LANES), 1)
    lane = lax.broadcasted_iota(jnp.int32, (1, D), 1)
    first_half = (lane % 32) < 16

    def norm_rope(x_ref, g_ref, scale):
        x = x_ref[...].astype(F32)
        sq = x * x
        hi = sq.astype(BF16)
        lo = (sq - hi.astype(F32)).astype(BF16)
        parts = []
        for blk in range(D // 256):
            sl = slice(blk * 256, (blk + 1) * 256)
            parts.append(_dot(hi[:, sl], ones) + _dot(lo[:, sl], ones))
        ss = jnp.concatenate(parts, 1)
        y = x * lax.rsqrt(ss * (1.0 / DA_HEAD_DIM) + EPS) * g_ref[...]
        partner = jnp.where(first_half, pltpu.roll(y, D - 16, 1), pltpu.roll(y, 16, 1))
        return ((y * cosf + partner * sinf) * scale).astype(BF16)

    qo_ref[...] = norm_rope(q_ref, gq_ref, LOG2E * DA_HEAD_DIM ** -0.5)
    ko_ref[...] = norm_rope(k_ref, gk_ref, 1.0)
    vt_ref[...] = v_ref[...].astype(F32).T.astype(BF16)


def _rope_tables(L, CTX):
    rows = L // GRID_W
    r = np.repeat(np.arange(rows), GRID_W)
    c = np.tile(np.arange(GRID_W), rows)
    nf = DA_HEAD_DIM // 4
    inv = jnp.asarray(10000.0, F32) ** (-jnp.arange(nf, dtype=F32) / nf)
    ang = jnp.stack([jnp.asarray(r), jnp.asarray(c)], -1).astype(F32)[:, :, None] * inv
    cos, sin = jnp.cos(ang), jnp.sin(ang)
    cos64 = jnp.stack([cos, cos], 2).reshape(L, DA_HEAD_DIM)
    sin64 = jnp.stack([-sin, sin], 2).reshape(L, DA_HEAD_DIM)
    cos_t = jnp.concatenate([jnp.ones((CTX, DA_HEAD_DIM), F32), cos64], 0)
    sin_t = jnp.concatenate([jnp.zeros((CTX, DA_HEAD_DIM), F32), sin64], 0)
    return jnp.tile(cos_t, (1, 2)), jnp.tile(sin_t, (1, 2))


def _da_pre(p2, gq, gk, cos_t, sin_t, tpb):
    rows = p2.shape[0]
    ones = jnp.asarray(np.kron(np.eye(256 // DA_HEAD_DIM), np.ones((DA_HEAD_DIM, DA_HEAD_DIM))), BF16)
    tab = pl.BlockSpec((ROW_TILE, LANES), lambda t: (t % tpb, 0))
    row = pl.BlockSpec((1, D), lambda t: (0, 0))
    out = pl.BlockSpec((ROW_TILE, D), lambda t: (t, 0))
    seg = lambda k: pl.BlockSpec((ROW_TILE, D), lambda t: (t, _PCOL_DA // D + k))
    return pl.pallas_call(
        _dapre_kernel, grid=(rows // ROW_TILE,),
        in_specs=[seg(0), seg(1), seg(2), row, row, tab, tab, pl.BlockSpec((256, 256), lambda t: (0, 0))],
        out_specs=[out, out, pl.BlockSpec((None, D, ROW_TILE), lambda t: (t // tpb, 0, t % tpb))],
        out_shape=[jax.ShapeDtypeStruct((rows, D), BF16)] * 2
        + [jax.ShapeDtypeStruct((rows // ROW_TILE // tpb, D, tpb * ROW_TILE), BF16)], name="dapre",
        compiler_params=_cp(("parallel",)))(p2, p2, p2, gq, gk, cos_t, sin_t, ones)


ATTN_KEY_BLOCK = 1024
ATTN_SAFE_SCORE = 30.0
LOG2E = 1.4426950408889634


def _attn_kernel(safe_ref, lam_ref, q_ref, k_ref, vt_ref, g_ref, o_ref, *, lam_init, ctx_rows, ctx_tile):
    lv = lam_ref[...]
    lam = (jnp.exp(jnp.sum(lv[0:1] * lv[1:2], keepdims=True)) - jnp.exp(jnp.sum(lv[2:3] * lv[3:4], keepdims=True))
           + lam_init)
    q = q_ref[...]
    tq = q.shape[0]
    lane_lo = lax.broadcasted_iota(jnp.int32, (1, LANES), 1) < DA_HEAD_DIM
    qcat = jnp.concatenate([jnp.where(lane_lo, q, jnp.zeros_like(q)), jnp.where(lane_lo, jnp.zeros_like(q), q)], 0)

    nt = (((1,), (1,)), ((), ()))

    def finish(ot):
        o = ot[:, :tq] - lam * ot[:, tq:]
        y = o * lax.rsqrt(jnp.mean(o * o, axis=0, keepdims=True) + EPS) * g_ref[...] * (1.0 - lam_init)
        o_ref[...] = y.T.astype(o_ref.dtype)

    def attend_bounded(nkeys):
        l = jnp.zeros((1, 2 * tq), F32)
        acc = jnp.zeros((LANES, 2 * tq), F32)
        blocks = [(k0, min(k0 + ATTN_KEY_BLOCK, nkeys)) for k0 in range(0, nkeys, ATTN_KEY_BLOCK)]
        scores = lambda b: lax.dot_general(k_ref[b[0]:b[1], :], qcat, nt, preferred_element_type=F32)
        st = scores(blocks[0])
        for j, (k0, k1) in enumerate(blocks):
            st_next = scores(blocks[j + 1]) if j + 1 < len(blocks) else None
            e = jnp.exp2(st)
            l = l + jnp.sum(e, axis=0, keepdims=True)
            acc = acc + _dot(vt_ref[:, k0:k1], e.astype(BF16))
            st = st_next
        finish(acc / l)

    def attend_shifted(nkeys):
        st = lax.dot_general(k_ref[0:nkeys, :], qcat, nt, preferred_element_type=F32)
        e = jnp.exp2(st - jnp.max(st, axis=0, keepdims=True))
        finish(_dot(vt_ref[:, 0:nkeys], e.astype(BF16)) / jnp.sum(e, axis=0, keepdims=True))

    def attend(nkeys):
        @pl.when(safe_ref[0] == 1)
        def _():
            attend_bounded(nkeys)

        @pl.when(safe_ref[0] != 1)
        def _():
            attend_shifted(nkeys)

    if ctx_tile:
        @pl.when(pl.program_id(2) == 0)
        def _():
            attend(ctx_rows)

        @pl.when(pl.program_id(2) > 0)
        def _():
            attend(k_ref.shape[0])
    else:
        attend(k_ref.shape[0])


def _da_attn(safe, lam4, qn, kn, vt, subg_col, lam_init, CTX, with_ctx):
    B, T, _ = qn.shape
    tq = ROW_TILE
    q0 = 0 if with_ctx else CTX // tq
    nq = T // tq - q0
    grid_spec = pltpu.PrefetchScalarGridSpec(
        num_scalar_prefetch=1, grid=(B, DA_HEADS, nq),
        in_specs=[pl.BlockSpec((4, DA_HEAD_DIM), lambda b, h, i, s: (0, 0)),
                  pl.BlockSpec((None, tq, LANES), lambda b, h, i, s: (b, q0 + i, h)),
                  pl.BlockSpec((None, T, LANES), lambda b, h, i, s: (b, 0, h)),
                  pl.BlockSpec((None, LANES, T), lambda b, h, i, s: (b, h, 0)),
                  pl.BlockSpec((LANES, 1), lambda b, h, i, s: (0, 0))],
        out_specs=pl.BlockSpec((None, tq, LANES), lambda b, h, i, s: (b, i, h)))
    return pl.pallas_call(
        partial(_attn_kernel, lam_init=lam_init, ctx_rows=CTX, ctx_tile=with_ctx), grid_spec=grid_spec,
        out_shape=jax.ShapeDtypeStruct((B, nq * tq, D), BF16), name="daattn",
        compiler_params=_cp(("parallel", "parallel", "parallel"), _VMEM_LIMIT))(safe, lam4, qn, kn, vt, subg_col)


def _merge_kernel(yh_ref, ys_ref, yd_ref, g0_ref, g1_ref, g2_ref, h_ref, wb_ref, wo_ref, gate_ref, n2_ref, sh_ref,
                  sc_ref, *rest, route):
    acc = None
    for i, (y_ref, gl_ref) in enumerate(((yh_ref, g0_ref), (ys_ref, g1_ref), (yd_ref, g2_ref))):
        term = jax.nn.sigmoid(gl_ref[...].astype(F32)) * _dot(y_ref[...], wb_ref[i])
        acc = term if acc is None else acc + term
    hn = h_ref[...] + gate_ref[...] * _dot(acc.astype(BF16), wo_ref[...])
    xn = hn * lax.rsqrt(jnp.mean(hn * hn, axis=-1, keepdims=True) + EPS) * n2_ref[...]
    fin = xn * (1.0 + sc_ref[...]) + sh_ref[...]
    if not route:
        hout_ref, fin_ref = rest
    else:
        rw_ref, hout_ref, fin_ref, rg_ref = rest
        rw = rw_ref[...]
        fin_hi, rw_hi = fin.astype(BF16), rw.astype(BF16)
        fin_lo, rw_lo = (fin - fin_hi.astype(F32)).astype(BF16), (rw - rw_hi.astype(F32)).astype(BF16)
        logits = _dot(fin_hi, rw_hi) + _dot(fin_lo, rw_hi) + _dot(fin_hi, rw_lo)
        lane = lax.broadcasted_iota(jnp.int32, logits.shape, 1)
        logits = jnp.where(lane < N_EXPERTS, logits, -jnp.inf)
        m1 = jnp.max(logits, axis=1, keepdims=True)
        i1 = jnp.min(jnp.where(logits == m1, lane, LANES), axis=1, keepdims=True)
        rest_l = jnp.where(lane == i1, -jnp.inf, logits)
        m2 = jnp.max(rest_l, axis=1, keepdims=True)
        i2 = jnp.min(jnp.where(rest_l == m2, lane, LANES), axis=1, keepdims=True)
        e2 = jnp.exp(m2 - m1)
        rg_ref[...] = jnp.where(lane == i1, 1.0 / (1.0 + e2), 0.0) + jnp.where(lane == i2, e2 / (1.0 + e2), 0.0)
    hout_ref[...] = hn
    fin_ref[...] = fin.astype(fin_ref.dtype)


def _merge(yh, ys, yd, p2, h, wb, wo, mod3, n2g, tpb, latent_only, router_w=None):
    rows = h.shape[0]
    if latent_only:
        nt = rows // ROW_TILE // tpb * (tpb - 1)
        src = lambda t: t // (tpb - 1) * tpb + 1 + t % (tpb - 1)
    else:
        nt = rows // ROW_TILE
        src = lambda t: t
    tile = lambda cb: pl.BlockSpec((ROW_TILE, D), lambda t: (src(t), cb))
    mspec = lambda k: pl.BlockSpec((None, 1, D), lambda t: (_modrow(src(t), tpb) * 6 + k, 0, 0))
    in_specs = [tile(0), tile(0), pl.BlockSpec((ROW_TILE, D), lambda t: (t, 0)) if latent_only else tile(0),
                tile(_PCOL_GATE // D), tile(_PCOL_GATE // D + 1),
                tile(_PCOL_GATE // D + 2), tile(0), pl.BlockSpec((3, D, D), lambda t: (0, 0, 0)),
                pl.BlockSpec((D, D), lambda t: (0, 0)), mspec(2), pl.BlockSpec((1, D), lambda t: (0, 0)),
                mspec(3), mspec(4)]
    args = [yh, ys, yd, p2, p2, p2, h, wb, wo, mod3, n2g, mod3, mod3]
    otile = pl.BlockSpec((ROW_TILE, D), lambda t: (t, 0))
    out_specs = [otile, otile]
    out_shape = [jax.ShapeDtypeStruct((nt * ROW_TILE, D), F32), jax.ShapeDtypeStruct((nt * ROW_TILE, D), BF16)]
    if router_w is not None:
        in_specs.append(pl.BlockSpec((D, LANES), lambda t: (0, 0)))
        args.append(router_w)
        out_specs.append(pl.BlockSpec((ROW_TILE, LANES), lambda t: (t, 0)))
        out_shape.append(jax.ShapeDtypeStruct((nt * ROW_TILE, LANES), F32))
    return pl.pallas_call(
        partial(_merge_kernel, route=router_w is not None), grid=(nt,), in_specs=in_specs, out_specs=out_specs,
        out_shape=out_shape, name="merge", compiler_params=_cp(("parallel",), _VMEM_LIMIT))(*args)


def _ffn_kernel(x_ref, w1_ref, w3_ref, w2_ref, h_ref, gc_ref, gl_ref, o_ref, acc, *, ctx_rows, tiles_per_batch):
    f = pl.program_id(1)
    x = x_ref[...]
    a = (_silu(_dot(x, w1_ref[...])) * _dot(x, w3_ref[...])).astype(BF16)
    part = _dot(a, w2_ref[...])

    @pl.when(f == 0)
    def _():
        acc[...] = part

    @pl.when(f > 0)
    def _():
        acc[...] += part

    @pl.when(f == pl.num_programs(1) - 1)
    def _():
        tm = x_ref.shape[0]
        r = (pl.program_id(0) % tiles_per_batch) * tm + lax.broadcasted_iota(jnp.int32, (tm, 1), 0)
        gate = jnp.where(r < ctx_rows, gc_ref[...], gl_ref[...])
        o_ref[...] = h_ref[...] + gate * acc[...]


def _ffn(fin, w1, w3, w2, h, mod3, T, CTX):
    rows = h.shape[0]
    tpb = 4
    tm = T // tpb
    ff = w1.shape[1]
    tf = 1024
    return pl.pallas_call(
        partial(_ffn_kernel, ctx_rows=CTX, tiles_per_batch=tpb), grid=(rows // tm, ff // tf),
        in_specs=[pl.BlockSpec((tm, D), lambda i, f: (i, 0)), pl.BlockSpec((D, tf), lambda i, f: (0, f)),
                  pl.BlockSpec((D, tf), lambda i, f: (0, f)), pl.BlockSpec((tf, D), lambda i, f: (f, 0)),
                  pl.BlockSpec((tm, D), lambda i, f: (i, 0)),
                  pl.BlockSpec((None, 1, D), lambda i, f: (5, 0, 0)),
                  pl.BlockSpec((None, 1, D), lambda i, f: ((1 + i // tpb) * 6 + 5, 0, 0))],
        out_specs=pl.BlockSpec((tm, D), lambda i, f: (i, 0)),
        out_shape=jax.ShapeDtypeStruct((rows, D), F32), scratch_shapes=[pltpu.VMEM((tm, D), F32)], name="ffn",
        compiler_params=_cp(("parallel", "arbitrary"), _VMEM_LIMIT))(fin, w1, w3, w2, h, mod3, mod3)


MOE_TILE = 1024
MOE_CHUNK = 256


def _slots_kernel(rg_ref, tri_ref, slot_ref, slott_ref, cnt_ref):
    blk = tri_ref.shape[0]
    carry = jnp.zeros((1, LANES), F32)
    for i in range(rg_ref.shape[0] // blk):
        sel = rg_ref[i * blk:(i + 1) * blk, :] > 0.0
        self32 = jnp.where(sel, 1.0, 0.0)
        rank = _dot(tri_ref[...], self32.astype(BF16)) + carry
        slot = jnp.where(sel, rank, -1.0)
        slot_ref[i * blk:(i + 1) * blk, :] = slot
        slott_ref[:, i * blk:(i + 1) * blk] = slot.T
        carry = carry + jnp.sum(self32, axis=0, keepdims=True)
    cnt_ref[...] = jnp.broadcast_to(carry, cnt_ref.shape)


def _moe_slots(rgate, tm):
    rows = rgate.shape[0]
    blk = 256
    tri = jnp.asarray(np.tril(np.ones((blk, blk), np.float32), -1), BF16)
    return pl.pallas_call(
        _slots_kernel, grid=(rows // tm,),
        in_specs=[pl.BlockSpec((tm, LANES), lambda i: (i, 0)), pl.BlockSpec((blk, blk), lambda i: (0, 0))],
        out_specs=[pl.BlockSpec((tm, LANES), lambda i: (i, 0)), pl.BlockSpec((LANES, tm), lambda i: (0, i)),
                   pl.BlockSpec((8, LANES), lambda i: (i, 0))],
        out_shape=[jax.ShapeDtypeStruct((rows, LANES), F32), jax.ShapeDtypeStruct((LANES, rows), F32),
                   jax.ShapeDtypeStruct((rows // tm * 8, LANES), F32)], name="moeslots",
        compiler_params=_cp(("parallel",)))(rgate, tri)


def _moe_kernel(nch_ref, x_ref, slott_ref, slot_ref, rg_ref, w1_ref, w3_ref, w2_ref, h_ref, gl_ref, o_ref,
                xe_ref, ye_ref):
    i, e, f = pl.program_id(0), pl.program_id(1), pl.program_id(2)
    CH = MOE_CHUNK
    nch = nch_ref[i * N_EXPERTS + e]

    @pl.when(jnp.logical_and(e == 0, f == 0))
    def _():
        o_ref[...] = h_ref[...]

    def expert(c):
        xe = xe_ref[c]
        a = (_silu(_dot(xe, w1_ref[...])) * _dot(xe, w3_ref[...])).astype(BF16)
        return _dot(a, w2_ref[...])

    @pl.when(f == 0)
    def _():
        def chunk(c, carry):
            want = lax.broadcasted_iota(jnp.int32, (CH, 1), 0).astype(F32) + (c * CH).astype(F32)
            pick = jnp.where(slott_ref[pl.ds(e, 1), :] == want, 1.0, 0.0).astype(BF16)
            xe_ref[c] = _dot(pick, x_ref[...]).astype(BF16)
            ye_ref[c] = expert(c)
            return carry

        lax.fori_loop(0, nch, chunk, 0)

    @pl.when(f == 1)
    def _():
        mine = lax.broadcasted_iota(jnp.int32, (1, LANES), 1) == e
        scol = jnp.sum(jnp.where(mine, slot_ref[...], 0.0), axis=1, keepdims=True)
        scale = jnp.sum(jnp.where(mine, rg_ref[...], 0.0), axis=1, keepdims=True) * gl_ref[...]

        def chunk(c, carry):
            ye = (ye_ref[c] + expert(c)).astype(BF16)
            want = lax.broadcasted_iota(jnp.int32, (1, CH), 1).astype(F32) + (c * CH).astype(F32)
            put = jnp.where(scol == want, 1.0, 0.0).astype(BF16)
            o_ref[...] += scale * _dot(put, ye)
            return carry

        lax.fori_loop(0, nch, chunk, 0)


def _moe(fin, rgate, w1, w3, w2, h, mod3, SEQ):
    rows = h.shape[0]
    tm = min(SEQ, MOE_TILE)
    ff = w1.shape[2]
    tf = ff // 2
    tiles_per_batch = SEQ // tm
    nt = rows // tm
    slot, slott, cnt = _moe_slots(rgate, tm)
    counts = cnt.reshape(nt, 8, LANES)[:, 0, :N_EXPERTS].astype(jnp.int32)
    nch = ((counts + MOE_CHUNK - 1) // MOE_CHUNK).reshape(nt * N_EXPERTS)
    grid_spec = pltpu.PrefetchScalarGridSpec(
        num_scalar_prefetch=1, grid=(nt, N_EXPERTS, 2),
        in_specs=[pl.BlockSpec((tm, D), lambda i, e, f, n: (i, 0)),
                  pl.BlockSpec((8, tm), lambda i, e, f, n: (0, i)),
                  pl.BlockSpec((tm, LANES), lambda i, e, f, n: (i, 0)),
                  pl.BlockSpec((tm, LANES), lambda i, e, f, n: (i, 0)),
                  pl.BlockSpec((None, D, tf), lambda i, e, f, n: (e, 0, f)),
                  pl.BlockSpec((None, D, tf), lambda i, e, f, n: (e, 0, f)),
                  pl.BlockSpec((None, tf, D), lambda i, e, f, n: (e, f, 0)),
                  pl.BlockSpec((tm, D), lambda i, e, f, n: (i, 0)),
                  pl.BlockSpec((None, 1, D), lambda i, e, f, n: ((1 + i // tiles_per_batch) * 6 + 5, 0, 0))],
        out_specs=pl.BlockSpec((tm, D), lambda i, e, f, n: (i, 0)),
        scratch_shapes=[pltpu.VMEM((tm // MOE_CHUNK, MOE_CHUNK, D), BF16),
                        pltpu.VMEM((tm // MOE_CHUNK, MOE_CHUNK, D), F32)])
    return pl.pallas_call(
        _moe_kernel, grid_spec=grid_spec, out_shape=jax.ShapeDtypeStruct((rows, D), F32), name="moe",
        compiler_params=_cp(("parallel", "arbitrary", "arbitrary"), _VMEM_LIMIT))(
            nch, fin, slott, slot, rgate, w1, w3, w2, h, mod3)


def _regroup_cols(w):
    hy, ssd, da, gate = 3072, 2592, 3072, 3072
    parts = [w[..., :hy], w[..., hy + ssd + da:hy + ssd + da + gate], w[..., hy + ssd:hy + ssd + da],
             w[..., hy:hy + ssd]]
    pad = jnp.zeros(w.shape[:-1] + (_PCOLS - hy - ssd - da - gate,), w.dtype)
    return jnp.concatenate(parts + [pad], axis=-1)


def kernel(x, c, ctx, c_ctx, w_mod, b_mod, norm1_g, norm2_g, w_in, hy_conv_w, hy_conv_b, hy_w1, hy_b1, hy_w2, hy_b2,
           hy_w3, hy_b3, hy_w4, hy_freq, hy_bias, ssd_conv_w, ssd_conv_b, ssd_dt_bias, ssd_a_log, ssd_d, ssd_norm_g,
           da_q_norm, da_k_norm, da_lambda, da_subln_g, w_branch, w_out, ffn_w1, ffn_w3, ffn_w2, router_w, moe_w1,
           moe_w3, moe_w2):
    B, SEQ, _ = x.shape
    CTX = ctx.shape[1]
    assert CTX == ROW_TILE and SEQ % ROW_TILE == 0 and B % 2 == 0 and B + 1 <= 8
    T = CTX + SEQ
    tpb = T // ROW_TILE
    depth = w_in.shape[0]
    R = B * T

    h = jnp.concatenate([ctx, x], axis=1).reshape(R, D)
    cvec = jnp.zeros((8, D), F32).at[0].set(c_ctx).at[1:1 + B].set(c)
    cos_t, sin_t = _rope_tables(SEQ, CTX)
    tabs_l, tabs_c = _fft_tables(SEQ), _fft_tables(CTX)
    nb = D // LANES

    for i in range(depth):
        need_ctx = i < depth - 1
        mod = _mod(cvec, w_mod[i], b_mod[i][None])
        mod3 = mod.reshape(8 * 6, 1, D)
        xn = _normmod(h, norm1_g[i][None], mod3, tpb)
        p2 = _matmul(xn, _regroup_cols(w_in[i].astype(BF16)), T // 2, 2048, BF16)
        p3 = p2.reshape(B, T, _PCOLS)

        cw, cb = hy_conv_w[i], hy_conv_b[i][None]
        hy_args = (hy_w1[i], hy_b1[i], hy_w2[i], hy_b2[i], hy_w3[i], hy_b3[i], hy_w4[i], hy_freq[i])
        kh = _hy_kfft(_hy_filters(SEQ, *hy_args), hy_bias[i], tabs_l, SEQ)
        z1 = _hy_conv(p3, 0, p3, nb, cw, cb, kh, 0, tabs_l, SEQ, CTX, SEQ, 0, True)
        y_hy = _hy_conv(z1, 0, p3, 2 * nb, cw, cb, kh, 1, tabs_l, SEQ, CTX, SEQ, CTX, False)
        if need_ctx:
            khc = _hy_kfft(_hy_filters(CTX, *hy_args), hy_bias[i], tabs_c, CTX)
            z1c = _hy_conv(p3, 0, p3, nb, cw, cb, khc, 0, tabs_c, CTX, 0, CTX, 0, True)
            y_hy = _hy_conv(z1c, 0, p3, 2 * nb, cw, cb, khc, 1, tabs_c, CTX, 0, CTX, 0, False, alias_into=y_hy)

        xbc = _ssd_pre(p3, ssd_conv_w[i], ssd_conv_b[i][None], CTX)
        dtb = jnp.pad(ssd_dt_bias[i].reshape(1, 2 * SSD_HEADS), ((0, 0), (0, LANES - 2 * SSD_HEADS)))
        arow = jnp.pad(-jnp.exp(ssd_a_log[i].astype(F32)).reshape(1, 2 * SSD_HEADS),
                       ((0, 0), (0, LANES - 2 * SSD_HEADS)))
        y_rev = _ssd_scan(xbc, p3, dtb, arow, True, CTX)
        dskip = jnp.repeat(ssd_d[i].astype(F32), SSD_HEADDIM)[None]
        y_ssd = _ssd_scan(xbc, p3, dtb, arow, False, CTX, fin=(y_rev, dskip, ssd_norm_g[i][None]))

        lam_init = 0.8 - 0.6 * math.exp(-0.3 * (i + 1))
        gq = jnp.tile(da_q_norm[i], D // DA_HEAD_DIM)[None]
        gk = jnp.tile(da_k_norm[i], D // DA_HEAD_DIM)[None]
        qn, kn, vt = _da_pre(p2, gq, gk, cos_t, sin_t, tpb)
        score_bound = 8.0 * jnp.max(jnp.abs(da_q_norm[i])) * jnp.max(jnp.abs(da_k_norm[i]))
        safe = (score_bound <= ATTN_SAFE_SCORE).astype(jnp.int32).reshape(1)
        y_da = _da_attn(safe, da_lambda[i], qn.reshape(B, T, D), kn.reshape(B, T, D), vt, da_subln_g[i][:, None],
                        lam_init, CTX, need_ctx)

        wb, wo = w_branch[i].astype(BF16), w_out[i].astype(BF16)
        flat = lambda a: a.reshape(-1, D)
        j = i // 2
        if i % 2 == 0:
            h, fin = _merge(flat(y_hy), flat(y_ssd), flat(y_da), p2, h, wb, wo, mod3, norm2_g[i][None], tpb,
                            latent_only=not need_ctx)
            assert need_ctx, "dense FFN layers are expected to carry context rows"
            h = _ffn(fin, ffn_w1[j].astype(BF16), ffn_w3[j].astype(BF16), ffn_w2[j].astype(BF16), h, mod3, T, CTX)
        else:
            assert not need_ctx, "expert layers are expected to be latent only"
            rw = jnp.pad(router_w[j], ((0, 0), (0, LANES - N_EXPERTS)))
            h, fin, rgate = _merge(flat(y_hy), flat(y_ssd), flat(y_da), p2, h, wb, wo, mod3, norm2_g[i][None], tpb,
                                   latent_only=True, router_w=rw)
            h = _moe(fin, rgate, moe_w1[j].astype(BF16), moe_w3[j].astype(BF16), moe_w2[j].astype(BF16), h, mod3,
                     SEQ)
    return h.reshape(B, SEQ, D)
```

```python
import math
from functools import partial

import numpy as np
import jax
import jax.numpy as jnp
from jax import lax
from jax.experimental import pallas as pl
from jax.experimental.pallas import tpu as pltpu

F32, BF16 = jnp.float32, jnp.bfloat16
HIGHEST = lax.Precision.HIGHEST

D = 1024
EPS = 1e-6
GRID_W = 64
ROW_TILE = 256
LANES = 128
SSD_CHUNK = 128
SSD_HEADS = 16
SSD_HEADDIM = 64
DA_HEADS = 8
DA_HEAD_DIM = 64
N_EXPERTS = 8
HY_BANDS = 16
HY_EMB = 2 * HY_BANDS + 1
HY_FFN = 64

_PCOL_HY = 0
_PCOL_GATE = 3072
_PCOL_DA = 6144
_PCOL_Z = 9216
_PCOL_XBC = 10240
_PCOL_DT = 11776
_PCOLS = 12288
_VMEM_LIMIT = 56 * 1024 * 1024


def _cp(sem, vmem=None):
    return pltpu.CompilerParams(dimension_semantics=sem, vmem_limit_bytes=vmem)


def _dot(a, b, prec=None):
    return jnp.dot(a, b, preferred_element_type=F32, precision=prec)


def _silu(x):
    return x * jax.nn.sigmoid(x)


def _mod_kernel(c_ref, w_ref, b_ref, o_ref):
    o_ref[...] = _dot(_silu(c_ref[...]), w_ref[...], HIGHEST) + b_ref[...]


def _mod(cvec, w, b):
    return pl.pallas_call(
        _mod_kernel, grid=(6,),
        in_specs=[pl.BlockSpec((8, D), lambda j: (0, 0)), pl.BlockSpec((D, D), lambda j: (0, j)),
                  pl.BlockSpec((1, D), lambda j: (0, j))],
        out_specs=pl.BlockSpec((8, D), lambda j: (0, j)),
        out_shape=jax.ShapeDtypeStruct((8, 6 * D), F32), name="mod")(cvec, w, b)


def _modrow(t, tpb):
    return jnp.where(t % tpb == 0, 0, 1 + t // tpb)


def _mod_spec(k, tpb):
    return pl.BlockSpec((None, 1, D), lambda t: (_modrow(t, tpb) * 6 + k, 0, 0))


def _normmod_kernel(h_ref, g_ref, shc_ref, scc_ref, shl_ref, scl_ref, o_ref, *, ctx_rows, tiles_per_batch):
    x = h_ref[...]
    tm = x.shape[0]
    r = (pl.program_id(0) % tiles_per_batch) * tm + lax.broadcasted_iota(jnp.int32, (tm, 1), 0)
    is_ctx = r < ctx_rows
    y = x * lax.rsqrt(jnp.mean(x * x, axis=-1, keepdims=True) + EPS) * g_ref[...]
    o_ref[...] = (y * (1.0 + jnp.where(is_ctx, scc_ref[...], scl_ref[...]))
                  + jnp.where(is_ctx, shc_ref[...], shl_ref[...])).astype(o_ref.dtype)


def _normmod(h, g, mod3, T, CTX):
    rows = h.shape[0]
    tpb = 4
    tm = T // tpb
    ctx = lambda k: pl.BlockSpec((None, 1, D), lambda t: (k, 0, 0))
    lat = lambda k: pl.BlockSpec((None, 1, D), lambda t: ((1 + t // tpb) * 6 + k, 0, 0))
    return pl.pallas_call(
        partial(_normmod_kernel, ctx_rows=CTX, tiles_per_batch=tpb), grid=(rows // tm,),
        in_specs=[pl.BlockSpec((tm, D), lambda t: (t, 0)), pl.BlockSpec((1, D), lambda t: (0, 0)),
                  ctx(0), ctx(1), lat(0), lat(1)],
        out_specs=pl.BlockSpec((tm, D), lambda t: (t, 0)),
        out_shape=jax.ShapeDtypeStruct((rows, D), BF16), name="normmod",
        compiler_params=_cp(("parallel",)))(h, g, mod3, mod3, mod3, mod3)


def _mm_kernel(a_ref, b_ref, o_ref):
    o_ref[...] = _dot(a_ref[...], b_ref[...]).astype(o_ref.dtype)


def _matmul(a, b, tm, tn, out_dtype):
    m, k = a.shape
    n = b.shape[1]
    return pl.pallas_call(
        _mm_kernel, grid=(n // tn, m // tm),
        in_specs=[pl.BlockSpec((tm, k), lambda j, i: (i, 0)), pl.BlockSpec((k, tn), lambda j, i: (0, j))],
        out_specs=pl.BlockSpec((tm, tn), lambda j, i: (i, j)),
        out_shape=jax.ShapeDtypeStruct((m, n), out_dtype), name="inproj",
        compiler_params=_cp(("parallel", "parallel"), _VMEM_LIMIT))(a, b)


def _hyfilt_kernel(f_ref, w1, b1, w2, b2, w3, b3, w4, fr_ref, dl_ref, o_ref, h_ref, win_ref, *, shift):
    @pl.when(pl.program_id(1) == 0)
    def _():
        f = f_ref[...]
        fr = fr_ref[...]
        h = jnp.sin(fr[0:1] * (_dot(f, w1[...], HIGHEST) + b1[...]))
        h = jnp.sin(fr[1:2] * (_dot(h, w2[...], HIGHEST) + b2[...]))
        h = jnp.sin(fr[2:3] * (_dot(h, w3[...], HIGHEST) + b3[...]))
        h_ref[...] = h.astype(BF16)
        win_ref[...] = jnp.exp(-f[:, 0:1] * dl_ref[...]) + shift

    o_ref[...] = _dot(h_ref[...], w4[...]) * win_ref[...]


def _hy_filters(L, w1, b1, w2, b2, w3, b3, w4, freq):
    t = jnp.linspace(0.0, 1.0, L, dtype=F32)[:, None]
    w = 2.0 * math.pi * jnp.arange(L, dtype=F32)[:, None] / L
    f = jnp.linspace(1e-4, HY_BANDS - 1, HY_BANDS, dtype=F32)[None, :]
    feats = jnp.concatenate([t, jnp.cos(f * w), -jnp.sin(f * w)], axis=-1)
    feats = jnp.pad(feats, ((0, 0), (0, HY_FFN - HY_EMB)))
    w1p = jnp.pad(w1, ((0, HY_FFN - HY_EMB), (0, 0)))
    max_decay = math.log(1e-2) / 0.3
    min_decay = math.log(1e-2) / 1.5
    deltas = jnp.abs(jnp.linspace(min_decay, max_decay, D, dtype=F32))[None, :]
    tl = min(L, 512)
    full = lambda r, c: pl.BlockSpec((r, c), lambda i, j: (0, 0))
    return pl.pallas_call(
        partial(_hyfilt_kernel, shift=0.05), grid=(L // tl, 4),
        in_specs=[pl.BlockSpec((tl, HY_FFN), lambda i, j: (i, 0)), full(HY_FFN, HY_FFN), full(1, HY_FFN),
                  full(HY_FFN, HY_FFN), full(1, HY_FFN), full(HY_FFN, HY_FFN), full(1, HY_FFN),
                  pl.BlockSpec((HY_FFN, D), lambda i, j: (0, j)), full(3, HY_FFN), full(1, D)],
        out_specs=pl.BlockSpec((tl, D), lambda i, j: (i, j)),
        out_shape=jax.ShapeDtypeStruct((L, 4 * D), F32), name="hyfilt",
        scratch_shapes=[pltpu.VMEM((tl, HY_FFN), BF16), pltpu.VMEM((tl, D), F32)],
        compiler_params=_cp(("parallel", "arbitrary")))(
            feats, w1p, b1[None], w2, b2[None], w3, b3[None], w4.astype(BF16), freq, deltas)


def _fft_split(L):
    n = 2 * L
    n2 = 128 if L >= 2048 else n // 32
    return n // n2, n2


def _fft_tables(L):
    n1_, n2_ = _fft_split(L)
    n, h1 = 2 * L, n1_ // 2
    n2 = np.arange(n2_)[:, None, None]
    k1 = np.arange(n1_)[None, :, None]
    n1 = np.arange(h1)[None, None, :]
    g = np.exp(-2j * np.pi * (n2 * k1 / n + n1 * k1 / n1_))
    ma = np.concatenate([np.concatenate([g.real, -g.imag], 2), np.concatenate([g.imag, g.real], 2)], 1)
    kk = np.arange(n2_)
    f = np.exp(-2j * np.pi * np.outer(kk, kk) / n2_)
    mc = np.block([[f.real, -f.imag], [f.imag, f.real]])
    mci = np.block([[f.real, f.imag], [-f.imag, f.real]])
    gi = np.exp(2j * np.pi * (n2 * np.arange(n1_)[None, None, :] / n
                              + np.arange(h1)[None, :, None] * np.arange(n1_)[None, None, :] / n1_)) / n
    mai = np.concatenate([np.concatenate([gi.real, -gi.imag], 2), np.concatenate([gi.imag, gi.real], 2)], 1)
    cast = lambda a: jnp.asarray(a, F32).astype(BF16)
    return cast(ma), cast(ma[:, :, :h1]), cast(mc), cast(mci), cast(mai)


def _kfft_kernel(kf_ref, kb_ref, bias_ref, mar_ref, mc_ref, o_ref, af_ref, ab_ref, *, N1, N2):
    H1, PA = N1 // 2, 2 * N1 + 8

    def stage_a(n2, c):
        for src, dst in ((kf_ref, af_ref), (kb_ref, ab_ref)):
            slab = src[pl.ds(n2, H1, stride=N2), :].astype(BF16)
            dst[pl.ds(pl.multiple_of(n2 * PA, 8), 2 * N1), :] = _dot(mar_ref[n2], slab)
        return c

    lax.fori_loop(0, N2, stage_a, 0, unroll=8)
    bias = bias_ref[...]

    def stage_c(k1, c):
        def rows(ref):
            return jnp.concatenate([ref[pl.ds(k1, N2, stride=PA), :], ref[pl.ds(N1 + k1, N2, stride=PA), :]], 0)
        x = _dot(mc_ref[...], jnp.concatenate([rows(af_ref), rows(ab_ref)], 1).astype(BF16))
        xf, xb = x[:, :LANES], x[:, LANES:]
        o_ref[k1] = jnp.concatenate([xf[:N2] + xb[:N2] + bias, xf[N2:] - xb[N2:]], 0).astype(o_ref.dtype)
        return c

    lax.fori_loop(0, N1, stage_c, 0, unroll=16)


def _hy_kfft(k4, bias, tabs, L):
    N1, N2 = _fft_split(L)
    _, mar, mc, _, _ = tabs
    nb = D // LANES
    return pl.pallas_call(
        partial(_kfft_kernel, N1=N1, N2=N2), grid=(2, nb),
        in_specs=[pl.BlockSpec((L, LANES), lambda o, j: (0, o * 2 * nb + j)),
                  pl.BlockSpec((L, LANES), lambda o, j: (0, o * 2 * nb + nb + j)),
                  pl.BlockSpec((None, 1, LANES), lambda o, j: (o, 0, j)),
                  pl.BlockSpec(mar.shape, lambda o, j: (0, 0, 0)), pl.BlockSpec(mc.shape, lambda o, j: (0, 0))],
        out_specs=pl.BlockSpec((None, N1, 2 * N2, LANES), lambda o, j: (o, 0, 0, j)),
        out_shape=jax.ShapeDtypeStruct((2, N1, 2 * N2, D), BF16),
        scratch_shapes=[pltpu.VMEM((N2 * (2 * N1 + 8), LANES), F32)] * 2, name="hykfft",
        compiler_params=_cp(("parallel", "parallel"), _VMEM_LIMIT))(k4, k4, bias[:, None, :], mar, mc)


def _hyconv_kernel(sig_ref, gate_ref, cws_ref, cbs_ref, cwg_ref, cbg_ref, kh_ref, ma_ref, mc_ref, mci_ref,
                   mai_ref, *rest, L, N1, N2, in_row0, out_row0, sig_conv, aliased):
    if aliased:
        rest = rest[1:]
    o_ref, sig0, sig1, gat0, gat1, a_ref = rest
    H1, PA = N1 // 2, 2 * N1 + 8
    NP, PADR = N2 + 8, 16
    sigs, gats = (sig0, sig1), (gat0, gat1)

    def stage(ref, nat, bb, r0):
        nat[0:PADR, :] = jnp.zeros((PADR, LANES), F32)
        nat[PADR + H1 * NP:PADR + H1 * NP + 8, :] = jnp.zeros((8, LANES), F32)
        for n1 in range(H1):
            nat[PADR + n1 * NP:PADR + n1 * NP + N2, :] = ref[bb, r0 + n1 * N2:r0 + (n1 + 1) * N2, :].astype(F32)

    def slab(nat, n2, w, b):
        u = nat[pl.ds(PADR + n2, H1, stride=NP), :]
        if w is None:
            return u
        prev = jnp.where(n2 == 0, PADR - NP + N2 - 1, PADR + n2 - 1)
        nxt = jnp.where(n2 == N2 - 1, PADR + NP, PADR + n2 + 1)
        return (w[0:1] * nat[pl.ds(prev, H1, stride=NP), :] + w[1:2] * u
                + w[2:3] * nat[pl.ds(nxt, H1, stride=NP), :] + b)

    for bb in range(2):
        stage(sig_ref, sigs[bb], bb, in_row0 if sig_conv else 0)
        stage(gate_ref, gats[bb], bb, in_row0)
    ws, bs = (cws_ref[...], cbs_ref[...]) if sig_conv else (None, None)
    wg, bg = cwg_ref[...], cbg_ref[...]

    def stage_a(n2, c):
        st = jnp.concatenate([slab(sigs[0], n2, ws, bs), slab(sigs[1], n2, ws, bs)], 0).astype(BF16)
        a_ref[pl.ds(pl.multiple_of(n2 * PA, 8), 2 * N1), :] = _dot(ma_ref[n2], st)
        return c

    lax.fori_loop(0, N2, stage_a, 0, unroll=16)

    def stage_c(i, c):
        k1 = 2 * i

        def rows(k):
            return jnp.concatenate([a_ref[pl.ds(k, N2, stride=PA), :], a_ref[pl.ds(N1 + k, N2, stride=PA), :]], 0)

        x = _dot(mc_ref[...], jnp.concatenate([rows(k1), rows(k1 + 1)], 1).astype(BF16))
        kh = jnp.concatenate([kh_ref[k1], kh_ref[k1 + 1]], 1).astype(F32)
        xr, xi, kr, ki = x[:N2], x[N2:], kh[:N2], kh[N2:]
        y = jnp.concatenate([xr * kr - xi * ki, xr * ki + xi * kr], 0)
        cc = _dot(mci_ref[...], y.astype(BF16))
        for j in range(2):
            a_ref[pl.ds(k1 + j, N2, stride=PA), :] = cc[:N2, j * LANES:(j + 1) * LANES]
            a_ref[pl.ds(N1 + k1 + j, N2, stride=PA), :] = cc[N2:, j * LANES:(j + 1) * LANES]
        return c

    lax.fori_loop(0, N1 // 2, stage_c, 0, unroll=16)

    def stage_ai(n2, c):
        st = a_ref[pl.ds(pl.multiple_of(n2 * PA, 8), 2 * N1), :]
        g = jnp.concatenate([slab(gats[0], n2, wg, bg), slab(gats[1], n2, wg, bg)], 0)
        r = _dot(mai_ref[n2], st.astype(BF16)) * g
        sig0[pl.ds(PADR + n2, H1, stride=NP), :] = r[:H1]
        sig1[pl.ds(PADR + n2, H1, stride=NP), :] = r[H1:]
        return c

    lax.fori_loop(0, N2, stage_ai, 0, unroll=16)

    for bb in range(2):
        if out_row0:
            o_ref[bb, 0:out_row0, :] = jnp.zeros((out_row0, LANES), o_ref.dtype)
        for n1 in range(H1):
            o_ref[bb, out_row0 + n1 * N2:out_row0 + (n1 + 1) * N2, :] = (
                sigs[bb][PADR + n1 * NP:PADR + n1 * NP + N2, :].astype(o_ref.dtype))


def _hy_conv(sig, sig_col, gate, gate_col, cw, cb, kh, order, tabs, L, in_row0, out_rows, out_row0, sig_conv,
             alias_into=None):
    N1, N2 = _fft_split(L)
    ma, _, mc, mci, mai = tabs
    H1 = N1 // 2
    B = gate.shape[0]
    nb = D // LANES
    scol = sig_col if sig_conv else 0
    cspec = lambda r, c0: pl.BlockSpec((r, LANES), lambda j, q: (0, c0 + j))
    const = lambda a: pl.BlockSpec(a.shape, lambda j, q: (0,) * a.ndim)
    in_specs = [pl.BlockSpec((2, sig.shape[1], LANES), lambda j, q: (q, 0, sig_col + j)),
                pl.BlockSpec((2, gate.shape[1], LANES), lambda j, q: (q, 0, gate_col + j)),
                cspec(3, scol), cspec(1, scol), cspec(3, gate_col), cspec(1, gate_col),
                pl.BlockSpec((None, N1, 2 * N2, LANES), lambda j, q: (order, 0, 0, j)),
                const(ma), const(mc), const(mci), const(mai)]
    args = [sig, gate, cw, cb, cw, cb, kh, ma, mc, mci, mai]
    aliases = {}
    if alias_into is not None:
        in_specs.append(pl.BlockSpec(memory_space=pl.ANY))
        args.append(alias_into)
        aliases = {len(args) - 1: 0}
        out_total = alias_into.shape[1]
    else:
        out_total = out_row0 + out_rows
    return pl.pallas_call(
        partial(_hyconv_kernel, L=L, N1=N1, N2=N2, in_row0=in_row0, out_row0=out_row0, sig_conv=sig_conv,
                aliased=alias_into is not None),
        grid=(nb, B // 2), in_specs=in_specs,
        out_specs=pl.BlockSpec((2, out_row0 + out_rows, LANES), lambda j, q: (q, 0, j)),
        out_shape=jax.ShapeDtypeStruct((B, out_total, D), BF16),
        scratch_shapes=[pltpu.VMEM((H1 * (N2 + 8) + 24, LANES), F32)] * 4
        + [pltpu.VMEM((N2 * (2 * N1 + 8), LANES), F32)],
        input_output_aliases=aliases, name="hyconv",
        compiler_params=_cp(("parallel", "parallel"), _VMEM_LIMIT))(*args)


def _ssdpre_kernel(x_ref, w_ref, b_ref, o_ref, nat, *, T, CTX):
    W = x_ref.shape[1]
    nat[0:8, :] = jnp.zeros((8, W), F32)
    nat[8 + T:16 + T, :] = jnp.zeros((8, W), F32)
    for i in range(T // ROW_TILE):
        nat[8 + i * ROW_TILE:8 + (i + 1) * ROW_TILE, :] = x_ref[i * ROW_TILE:(i + 1) * ROW_TILE, :].astype(F32)
    w = w_ref[...]
    b = b_ref[...]
    row = lax.broadcasted_iota(jnp.int32, (ROW_TILE, 1), 0)
    for i in range(T // ROW_TILE):
        r0 = 8 + i * ROW_TILE
        xm = nat[r0 - 1:r0 - 1 + ROW_TILE, :]
        xp = nat[r0 + 1:r0 + 1 + ROW_TILE, :]
        if i * ROW_TILE == CTX:
            xm = jnp.where(row == 0, 0.0, xm)
        if (i + 1) * ROW_TILE == CTX:
            xp = jnp.where(row == ROW_TILE - 1, 0.0, xp)
        u = w[0:1] * xm + w[1:2] * nat[r0:r0 + ROW_TILE, :] + w[2:3] * xp + b
        o_ref[i * ROW_TILE:(i + 1) * ROW_TILE, :] = _silu(u).astype(o_ref.dtype)


def _ssd_pre(p3, cw, cb, CTX):
    B, T, _ = p3.shape
    W = 256
    nblk = cw.shape[1] // W
    return pl.pallas_call(
        partial(_ssdpre_kernel, T=T, CTX=CTX), grid=(B, nblk),
        in_specs=[pl.BlockSpec((None, T, W), lambda b, j: (b, 0, _PCOL_XBC // W + j)),
                  pl.BlockSpec((3, W), lambda b, j: (0, j)), pl.BlockSpec((1, W), lambda b, j: (0, j))],
        out_specs=pl.BlockSpec((None, T, W), lambda b, j: (b, 0, j)),
        out_shape=jax.ShapeDtypeStruct((B, T, cw.shape[1]), BF16),
        scratch_shapes=[pltpu.VMEM((T + 16, W), F32)], name="ssdpre",
        compiler_params=_cp(("parallel", "parallel")))(p3, cw, cb)


def _ssdscan_kernel(xbc_ref, dt_ref, dtb_ref, a_ref, tri_ref, *rest, rev):
    if rev:
        o_ref, s_ref = rest
    else:
        z_ref, yr_ref, dsk_ref, ng_ref, o_ref, s_ref, ybuf = rest
    Q = SSD_CHUNK

    @pl.when(pl.program_id(1) == 0)
    def _():
        s_ref[...] = jnp.zeros(s_ref.shape, F32)

    x = xbc_ref[...]
    xs, Bm, Cm = x[:, :D], x[:, D:D + 2 * LANES], x[:, D + 2 * LANES:]
    dtr = dt_ref[...].astype(F32) + dtb_ref[...]
    dt = jnp.maximum(dtr, 0.0) + jnp.log(1.0 + jnp.exp(-jnp.abs(dtr)))
    tri = tri_ref[...]
    mask = tri > 0.5
    cs = _dot(tri, dt * a_ref[...], HIGHEST)
    csT = cs.T
    dtT = dt.T
    last = 0 if rev else Q - 1
    col0 = SSD_HEADS if rev else 0
    lane_lo = lax.broadcasted_iota(jnp.int32, (1, LANES), 1) < SSD_HEADDIM
    for g in range(2):
        Bg = Bm[:, g * LANES:(g + 1) * LANES]
        Cg = Cm[:, g * LANES:(g + 1) * LANES]
        CB = lax.dot_general(Cg, Bg, (((1,), (1,)), ((), ())), preferred_element_type=F32)
        BgT = Bg.astype(F32).T
        Cgf = Cg.astype(F32)
        for pr in range(4):
            hp = g * 4 + pr
            xs_pair = xs[:, hp * LANES:(hp + 1) * LANES]
            Sp = s_ref[hp]
            rhs = jnp.concatenate([xs_pair, Sp.astype(BF16)], 0)
            ys, sts = [], []
            for hh in range(2):
                col = col0 + 2 * hp + hh
                csl = jnp.broadcast_to(cs[:, col:col + 1], (Q, Q))
                csr = csT[col:col + 1, :]
                dtr_row = dtT[col:col + 1, :]
                Lm = jnp.exp(jnp.where(mask, csl - csr, -1e30))
                lhs = jnp.concatenate([CB * Lm * dtr_row, Cgf * jnp.exp(csl)], 1).astype(BF16)
                ys.append(_dot(lhs, rhs))
                tot = csT[col:col + 1, last:last + 1]
                w_row = jnp.exp(tot - csr) * dtr_row
                st = _dot((BgT * w_row).astype(BF16), xs_pair)
                sts.append(jnp.exp(tot) * Sp + st)
            y_pair = jnp.where(lane_lo, ys[0], ys[1])
            s_ref[hp] = jnp.where(lane_lo, sts[0], sts[1])
            sl = slice(hp * LANES, (hp + 1) * LANES)
            if rev:
                o_ref[:, sl] = y_pair.astype(o_ref.dtype)
            else:
                ybuf[:, sl] = y_pair + yr_ref[:, sl].astype(F32) + xs_pair.astype(F32) * dsk_ref[:, sl]
    if not rev:
        y = ybuf[...] * _silu(z_ref[...].astype(F32))
        half = D // 2
        parts = []
        for g in range(2):
            yg = y[:, g * half:(g + 1) * half]
            parts.append(yg * lax.rsqrt(jnp.mean(yg * yg, axis=-1, keepdims=True) + EPS))
        o_ref[...] = (jnp.concatenate(parts, 1) * ng_ref[...]).astype(o_ref.dtype)


def _ssd_scan(xbc, p3, dtb, arow, rev, CTX, fin=None):
    B, T, _ = xbc.shape
    Q = SSD_CHUNK
    nch, nc = T // Q, CTX // Q
    if rev:
        chunk = lambda s: jnp.where(s < nc, nc - 1 - s, nch - 1 + nc - s)
        tri = jnp.asarray(np.triu(np.ones((Q, Q), np.float32)))
    else:
        chunk = lambda s: s
        tri = jnp.asarray(np.tril(np.ones((Q, Q), np.float32)))
    row = lambda w: pl.BlockSpec((1, w), lambda b, s: (0, 0))
    in_specs = [pl.BlockSpec((None, Q, xbc.shape[2]), lambda b, s: (b, chunk(s), 0)),
                pl.BlockSpec((None, Q, LANES), lambda b, s: (b, chunk(s), _PCOL_DT // LANES)),
                row(LANES), row(LANES), pl.BlockSpec((Q, Q), lambda b, s: (0, 0))]
    args = [xbc, p3, dtb, arow, tri]
    scratch = [pltpu.VMEM((SSD_HEADS // 2, LANES, LANES), F32)]
    if not rev:
        yrev, dskip, ng = fin
        in_specs += [pl.BlockSpec((None, Q, D), lambda b, s: (b, s, _PCOL_Z // D)),
                     pl.BlockSpec((None, Q, D), lambda b, s: (b, s, 0)), row(D), row(D)]
        args += [p3, yrev, dskip, ng]
        scratch.append(pltpu.VMEM((Q, D), F32))
    return pl.pallas_call(
        partial(_ssdscan_kernel, rev=rev), grid=(B, nch), in_specs=in_specs,
        out_specs=pl.BlockSpec((None, Q, D), lambda b, s: (b, chunk(s), 0)),
        out_shape=jax.ShapeDtypeStruct((B, T, D), BF16), scratch_shapes=scratch,
        name="ssdrev" if rev else "ssdfwd",
        compiler_params=_cp(("parallel", "arbitrary")))(*args)


def _dapre_kernel(q_ref, k_ref, v_ref, gq_ref, gk_ref, cos_ref, sin_ref, ones_ref, qo_ref, ko_ref, vt_ref):
    ones = ones_ref[...]
    cosf = jnp.concatenate([cos_ref[...]] * (D // LANES), 1)
    sinf = jnp.concatenate([sin_ref[...]] * (D // LANES), 1)
    lane = lax.broadcasted_iota(jnp.int32, (1, D), 1)
    first_half = (lane % 32) < 16

    def norm_rope(x_ref, g_ref, scale):
        x = x_ref[...].astype(F32)
        sq = x * x
        hi = sq.astype(BF16)
        lo = (sq - hi.astype(F32)).astype(BF16)
        parts = []
        for blk in range(D // 256):
            sl = slice(blk * 256, (blk + 1) * 256)
            parts.append(_dot(hi[:, sl], ones) + _dot(lo[:, sl], ones))
        ss = jnp.concatenate(parts, 1)
        y = x * lax.rsqrt(ss * (1.0 / DA_HEAD_DIM) + EPS) * g_ref[...]
        partner = jnp.where(first_half, pltpu.roll(y, D - 16, 1), pltpu.roll(y, 16, 1))
        return ((y * cosf + partner * sinf) * scale).astype(BF16)

    qo_ref[...] = norm_rope(q_ref, gq_ref, LOG2E * DA_HEAD_DIM ** -0.5)
    ko_ref[...] = norm_rope(k_ref, gk_ref, 1.0)
    vt_ref[...] = v_ref[...].astype(F32).T.astype(BF16)


def _rope_tables(L, CTX):
    rows = L // GRID_W
    r = np.repeat(np.arange(rows), GRID_W)
    c = np.tile(np.arange(GRID_W), rows)
    nf = DA_HEAD_DIM // 4
    inv = jnp.asarray(10000.0, F32) ** (-jnp.arange(nf, dtype=F32) / nf)
    ang = jnp.stack([jnp.asarray(r), jnp.asarray(c)], -1).astype(F32)[:, :, None] * inv
    cos, sin = jnp.cos(ang), jnp.sin(ang)
    cos64 = jnp.stack([cos, cos], 2).reshape(L, DA_HEAD_DIM)
    sin64 = jnp.stack([-sin, sin], 2).reshape(L, DA_HEAD_DIM)
    cos_t = jnp.concatenate([jnp.ones((CTX, DA_HEAD_DIM), F32), cos64], 0)
    sin_t = jnp.concatenate([jnp.zeros((CTX, DA_HEAD_DIM), F32), sin64], 0)
    return jnp.tile(cos_t, (1, 2)), jnp.tile(sin_t, (1, 2))


def _da_pre(p2, gq, gk, cos_t, sin_t, tpb):
    rows = p2.shape[0]
    ones = jnp.asarray(np.kron(np.eye(256 // DA_HEAD_DIM), np.ones((DA_HEAD_DIM, DA_HEAD_DIM))), BF16)
    tab = pl.BlockSpec((ROW_TILE, LANES), lambda t: (t % tpb, 0))
    row = pl.BlockSpec((1, D), lambda t: (0, 0))
    out = pl.BlockSpec((ROW_TILE, D), lambda t: (t, 0))
    seg = lambda k: pl.BlockSpec((ROW_TILE, D), lambda t: (t, _PCOL_DA // D + k))
    return pl.pallas_call(
        _dapre_kernel, grid=(rows // ROW_TILE,),
        in_specs=[seg(0), seg(1), seg(2), row, row, tab, tab, pl.BlockSpec((256, 256), lambda t: (0, 0))],
        out_specs=[out, out, pl.BlockSpec((None, D, ROW_TILE), lambda t: (t // tpb, 0, t % tpb))],
        out_shape=[jax.ShapeDtypeStruct((rows, D), BF16)] * 2
        + [jax.ShapeDtypeStruct((rows // ROW_TILE // tpb, D, tpb * ROW_TILE), BF16)], name="dapre",
        compiler_params=_cp(("parallel",)))(p2, p2, p2, gq, gk, cos_t, sin_t, ones)


ATTN_KEY_BLOCK = 1024
ATTN_SAFE_SCORE = 30.0
LOG2E = 1.4426950408889634


def _attn_kernel(safe_ref, lam_ref, *refs, lam_init):
    lv = lam_ref[...]
    lam = (jnp.exp(jnp.sum(lv[0:1] * lv[1:2], keepdims=True)) - jnp.exp(jnp.sum(lv[2:3] * lv[3:4], keepdims=True))
           + lam_init)
    *q_refs, k_ref, vt_ref, g_ref, o_ref = refs
    q = jnp.concatenate([r[...] for r in q_refs], 0)
    tq = q.shape[0]
    lane_lo = lax.broadcasted_iota(jnp.int32, (1, LANES), 1) < DA_HEAD_DIM
    qcat = jnp.concatenate([jnp.where(lane_lo, q, jnp.zeros_like(q)), jnp.where(lane_lo, jnp.zeros_like(q), q)], 0)

    nt = (((1,), (1,)), ((), ()))

    def finish(ot):
        o = ot[:, :tq] - lam * ot[:, tq:]
        y = o * lax.rsqrt(jnp.mean(o * o, axis=0, keepdims=True) + EPS) * g_ref[...] * (1.0 - lam_init)
        o_ref[...] = y.T.astype(o_ref.dtype)

    def attend(shifted):
        nkeys = k_ref.shape[0]
        blocks = [(k0, min(k0 + ATTN_KEY_BLOCK, nkeys)) for k0 in range(0, nkeys, ATTN_KEY_BLOCK)]
        scores = lambda b: lax.dot_general(k_ref[b[0]:b[1], :], qcat, nt, preferred_element_type=F32)
        if shifted:
            m = jnp.full((1, 2 * tq), -jnp.inf, F32)
            for blk in blocks:
                m = jnp.maximum(m, jnp.max(scores(blk), axis=0, keepdims=True))
        l = jnp.zeros((1, 2 * tq), F32)
        acc = jnp.zeros((LANES, 2 * tq), F32)
        st = scores(blocks[0])
        for j, (k0, k1) in enumerate(blocks):
            st_next = scores(blocks[j + 1]) if j + 1 < len(blocks) else None
            e = jnp.exp2(st - m) if shifted else jnp.exp2(st)
            l = l + jnp.sum(e, axis=0, keepdims=True)
            acc = acc + _dot(vt_ref[:, k0:k1], e.astype(BF16))
            st = st_next
        finish(acc / l)

    @pl.when(safe_ref[0] == 1)
    def _():
        attend(False)

    @pl.when(safe_ref[0] != 1)
    def _():
        attend(True)


def _da_attn(safe, lam4, qn, kn, vt, subg_col, lam_init, CTX, context):
    B, T, _ = qn.shape
    tq = ROW_TILE
    if context:
        nq, nkeys, out_rows = 1, CTX, CTX
        q_specs = [pl.BlockSpec((None, tq, LANES), lambda b, h, i, s: (b, 0, h))]
    else:
        nq, nkeys, out_rows = (T - CTX) // (2 * tq), T, T - CTX
        q0 = CTX // tq
        q_specs = [pl.BlockSpec((None, tq, LANES), lambda b, h, i, s: (b, q0 + 2 * i, h)),
                   pl.BlockSpec((None, tq, LANES), lambda b, h, i, s: (b, q0 + 2 * i + 1, h))]
    grid_spec = pltpu.PrefetchScalarGridSpec(
        num_scalar_prefetch=1, grid=(B, DA_HEADS, nq),
        in_specs=[pl.BlockSpec((4, DA_HEAD_DIM), lambda b, h, i, s: (0, 0))] + q_specs + [
            pl.BlockSpec((None, nkeys, LANES), lambda b, h, i, s: (b, 0, h)),
            pl.BlockSpec((None, LANES, nkeys), lambda b, h, i, s: (b, h, 0)),
            pl.BlockSpec((LANES, 1), lambda b, h, i, s: (0, 0))],
        out_specs=pl.BlockSpec((None, len(q_specs) * tq, LANES), lambda b, h, i, s: (b, i, h)))
    return pl.pallas_call(
        partial(_attn_kernel, lam_init=lam_init), grid_spec=grid_spec,
        out_shape=jax.ShapeDtypeStruct((B, out_rows, D), BF16), name="daattn",
        compiler_params=_cp(("parallel", "parallel", "parallel"), _VMEM_LIMIT))(
            safe, lam4, *([qn] * len(q_specs)), kn, vt, subg_col)


def _merge_kernel(yh_ref, ys_ref, yd_ref, ydc_ref, g0_ref, g1_ref, g2_ref, h_ref, wb_ref, wo_ref, gate_ref, n2_ref,
                  sh_ref, sc_ref, *rest, route, ctx_every):
    yd = yd_ref[...]
    if ctx_every:
        yd = jnp.where(pl.program_id(0) % ctx_every == 0, ydc_ref[...], yd)
    acc = None
    for i, (y, gl_ref) in enumerate(((yh_ref[...], g0_ref), (ys_ref[...], g1_ref), (yd, g2_ref))):
        term = jax.nn.sigmoid(gl_ref[...].astype(F32)) * _dot(y, wb_ref[i])
        acc = term if acc is None else acc + term
    hn = h_ref[...] + gate_ref[...] * _dot(acc.astype(BF16), wo_ref[...])
    xn = hn * lax.rsqrt(jnp.mean(hn * hn, axis=-1, keepdims=True) + EPS) * n2_ref[...]
    fin = xn * (1.0 + sc_ref[...]) + sh_ref[...]
    if not route:
        hout_ref, fin_ref = rest
    else:
        rw_ref, hout_ref, fin_ref, rg_ref = rest
        rw = rw_ref[...]
        fin_hi, rw_hi = fin.astype(BF16), rw.astype(BF16)
        fin_lo, rw_lo = (fin - fin_hi.astype(F32)).astype(BF16), (rw - rw_hi.astype(F32)).astype(BF16)
        logits = _dot(fin_hi, rw_hi) + _dot(fin_lo, rw_hi) + _dot(fin_hi, rw_lo)
        lane = lax.broadcasted_iota(jnp.int32, logits.shape, 1)
        logits = jnp.where(lane < N_EXPERTS, logits, -jnp.inf)
        m1 = jnp.max(logits, axis=1, keepdims=True)
        i1 = jnp.min(jnp.where(logits == m1, lane, LANES), axis=1, keepdims=True)
        rest_l = jnp.where(lane == i1, -jnp.inf, logits)
        m2 = jnp.max(rest_l, axis=1, keepdims=True)
        i2 = jnp.min(jnp.where(rest_l == m2, lane, LANES), axis=1, keepdims=True)
        e2 = jnp.exp(m2 - m1)
        rg_ref[...] = jnp.where(lane == i1, 1.0 / (1.0 + e2), 0.0) + jnp.where(lane == i2, e2 / (1.0 + e2), 0.0)
    hout_ref[...] = hn
    fin_ref[...] = fin.astype(fin_ref.dtype)


def _merge(yh, ys, yd, ydc, p2, h, wb, wo, mod3, n2g, tpb, latent_only, router_w=None):
    rows = h.shape[0]
    if latent_only:
        nt = rows // ROW_TILE // tpb * (tpb - 1)
        src = lambda t: t // (tpb - 1) * tpb + 1 + t % (tpb - 1)
    else:
        nt = rows // ROW_TILE
        src = lambda t: t
    tile = lambda cb: pl.BlockSpec((ROW_TILE, D), lambda t: (src(t), cb))
    mspec = lambda k: pl.BlockSpec((None, 1, D), lambda t: (_modrow(src(t), tpb) * 6 + k, 0, 0))
    if latent_only:
        yd_specs = [pl.BlockSpec((ROW_TILE, D), lambda t: (t, 0)), pl.BlockSpec((ROW_TILE, D), lambda t: (0, 0))]
    else:
        yd_specs = [pl.BlockSpec((ROW_TILE, D), lambda t: (t // tpb * (tpb - 1) + jnp.maximum(t % tpb - 1, 0), 0)),
                    pl.BlockSpec((ROW_TILE, D), lambda t: (t // tpb, 0))]
    in_specs = [tile(0), tile(0)] + yd_specs + [
                tile(_PCOL_GATE // D), tile(_PCOL_GATE // D + 1),
                tile(_PCOL_GATE // D + 2), tile(0), pl.BlockSpec((3, D, D), lambda t: (0, 0, 0)),
                pl.BlockSpec((D, D), lambda t: (0, 0)), mspec(2), pl.BlockSpec((1, D), lambda t: (0, 0)),
                mspec(3), mspec(4)]
    args = [yh, ys, yd, ydc, p2, p2, p2, h, wb, wo, mod3, n2g, mod3, mod3]
    otile = pl.BlockSpec((ROW_TILE, D), lambda t: (t, 0))
    out_specs = [otile, otile]
    out_shape = [jax.ShapeDtypeStruct((nt * ROW_TILE, D), F32), jax.ShapeDtypeStruct((nt * ROW_TILE, D), BF16)]
    if router_w is not None:
        in_specs.append(pl.BlockSpec((D, LANES), lambda t: (0, 0)))
        args.append(router_w)
        out_specs.append(pl.BlockSpec((ROW_TILE, LANES), lambda t: (t, 0)))
        out_shape.append(jax.ShapeDtypeStruct((nt * ROW_TILE, LANES), F32))
    return pl.pallas_call(
        partial(_merge_kernel, route=router_w is not None, ctx_every=0 if latent_only else tpb), grid=(nt,), in_specs=in_specs, out_specs=out_specs,
        out_shape=out_shape, name="merge", compiler_params=_cp(("parallel",), _VMEM_LIMIT))(*args)


def _ffn_kernel(x_ref, w1_ref, w3_ref, w2_ref, h_ref, gc_ref, gl_ref, o_ref, acc, *, ctx_rows, tiles_per_batch):
    f = pl.program_id(1)
    x = x_ref[...]
    a = (_silu(_dot(x, w1_ref[...])) * _dot(x, w3_ref[...])).astype(BF16)
    part = _dot(a, w2_ref[...])

    @pl.when(f == 0)
    def _():
        acc[...] = part

    @pl.when(f > 0)
    def _():
        acc[...] += part

    @pl.when(f == pl.num_programs(1) - 1)
    def _():
        tm = x_ref.shape[0]
        r = (pl.program_id(0) % tiles_per_batch) * tm + lax.broadcasted_iota(jnp.int32, (tm, 1), 0)
        gate = jnp.where(r < ctx_rows, gc_ref[...], gl_ref[...])
        o_ref[...] = h_ref[...] + gate * acc[...]


def _ffn(fin, w1, w3, w2, h, mod3, T, CTX):
    rows = h.shape[0]
    tpb = 4
    tm = T // tpb
    ff = w1.shape[1]
    tf = 1024
    return pl.pallas_call(
        partial(_ffn_kernel, ctx_rows=CTX, tiles_per_batch=tpb), grid=(rows // tm, ff // tf),
        in_specs=[pl.BlockSpec((tm, D), lambda i, f: (i, 0)), pl.BlockSpec((D, tf), lambda i, f: (0, f)),
                  pl.BlockSpec((D, tf), lambda i, f: (0, f)), pl.BlockSpec((tf, D), lambda i, f: (f, 0)),
                  pl.BlockSpec((tm, D), lambda i, f: (i, 0)),
                  pl.BlockSpec((None, 1, D), lambda i, f: (5, 0, 0)),
                  pl.BlockSpec((None, 1, D), lambda i, f: ((1 + i // tpb) * 6 + 5, 0, 0))],
        out_specs=pl.BlockSpec((tm, D), lambda i, f: (i, 0)),
        out_shape=jax.ShapeDtypeStruct((rows, D), F32), scratch_shapes=[pltpu.VMEM((tm, D), F32)], name="ffn",
        compiler_params=_cp(("parallel", "arbitrary"), _VMEM_LIMIT))(fin, w1, w3, w2, h, mod3, mod3)


MOE_TILE = 1024
MOE_CHUNK = 256


def _slots_kernel(rg_ref, tri_ref, slot_ref, slott_ref, cnt_ref):
    blk = tri_ref.shape[0]
    carry = jnp.zeros((1, LANES), F32)
    for i in range(rg_ref.shape[0] // blk):
        sel = rg_ref[i * blk:(i + 1) * blk, :] > 0.0
        self32 = jnp.where(sel, 1.0, 0.0)
        rank = _dot(tri_ref[...], self32.astype(BF16)) + carry
        slot = jnp.where(sel, rank, -1.0)
        slot_ref[i * blk:(i + 1) * blk, :] = slot
        slott_ref[:, i * blk:(i + 1) * blk] = slot.T
        carry = carry + jnp.sum(self32, axis=0, keepdims=True)
    cnt_ref[...] = jnp.broadcast_to(carry, cnt_ref.shape)


def _moe_slots(rgate, tm):
    rows = rgate.shape[0]
    blk = 256
    tri = jnp.asarray(np.tril(np.ones((blk, blk), np.float32), -1), BF16)
    return pl.pallas_call(
        _slots_kernel, grid=(rows // tm,),
        in_specs=[pl.BlockSpec((tm, LANES), lambda i: (i, 0)), pl.BlockSpec((blk, blk), lambda i: (0, 0))],
        out_specs=[pl.BlockSpec((tm, LANES), lambda i: (i, 0)), pl.BlockSpec((LANES, tm), lambda i: (0, i)),
                   pl.BlockSpec((8, LANES), lambda i: (i, 0))],
        out_shape=[jax.ShapeDtypeStruct((rows, LANES), F32), jax.ShapeDtypeStruct((LANES, rows), F32),
                   jax.ShapeDtypeStruct((rows // tm * 8, LANES), F32)], name="moeslots",
        compiler_params=_cp(("parallel",)))(rgate, tri)


def _moe_kernel(nch_ref, x_ref, slott_ref, slot_ref, rg_ref, w1_ref, w3_ref, w2_ref, h_ref, gl_ref, o_ref,
                xe_ref, ye_ref):
    i, e, f = pl.program_id(0), pl.program_id(1), pl.program_id(2)
    CH = MOE_CHUNK
    nch = nch_ref[i * N_EXPERTS + e]

    @pl.when(jnp.logical_and(e == 0, f == 0))
    def _():
        o_ref[...] = h_ref[...]

    def expert(c):
        xe = xe_ref[c]
        a = (_silu(_dot(xe, w1_ref[...])) * _dot(xe, w3_ref[...])).astype(BF16)
        return _dot(a, w2_ref[...])

    @pl.when(f == 0)
    def _():
        def chunk(c, carry):
            want = lax.broadcasted_iota(jnp.int32, (CH, 1), 0).astype(F32) + (c * CH).astype(F32)
            pick = jnp.where(slott_ref[pl.ds(e, 1), :] == want, 1.0, 0.0).astype(BF16)
            xe_ref[c] = _dot(pick, x_ref[...]).astype(BF16)
            ye_ref[c] = expert(c)
            return carry

        lax.fori_loop(0, nch, chunk, 0)

    @pl.when(f == 1)
    def _():
        mine = lax.broadcasted_iota(jnp.int32, (1, LANES), 1) == e
        scol = jnp.sum(jnp.where(mine, slot_ref[...], 0.0), axis=1, keepdims=True)
        scale = jnp.sum(jnp.where(mine, rg_ref[...], 0.0), axis=1, keepdims=True) * gl_ref[...]

        def chunk(c, carry):
            ye = (ye_ref[c] + expert(c)).astype(BF16)
            want = lax.broadcasted_iota(jnp.int32, (1, CH), 1).astype(F32) + (c * CH).astype(F32)
            put = jnp.where(scol == want, 1.0, 0.0).astype(BF16)
            o_ref[...] += scale * _dot(put, ye)
            return carry

        lax.fori_loop(0, nch, chunk, 0)


def _moe(fin, rgate, w1, w3, w2, h, mod3, SEQ):
    rows = h.shape[0]
    tm = min(SEQ, MOE_TILE)
    ff = w1.shape[2]
    tf = ff // 2
    tiles_per_batch = SEQ // tm
    nt = rows // tm
    slot, slott, cnt = _moe_slots(rgate, tm)
    counts = cnt.reshape(nt, 8, LANES)[:, 0, :N_EXPERTS].astype(jnp.int32)
    nch = ((counts + MOE_CHUNK - 1) // MOE_CHUNK).reshape(nt * N_EXPERTS)
    grid_spec = pltpu.PrefetchScalarGridSpec(
        num_scalar_prefetch=1, grid=(nt, N_EXPERTS, 2),
        in_specs=[pl.BlockSpec((tm, D), lambda i, e, f, n: (i, 0)),
                  pl.BlockSpec((8, tm), lambda i, e, f, n: (0, i)),
                  pl.BlockSpec((tm, LANES), lambda i, e, f, n: (i, 0)),
                  pl.BlockSpec((tm, LANES), lambda i, e, f, n: (i, 0)),
                  pl.BlockSpec((None, D, tf), lambda i, e, f, n: (e, 0, f)),
                  pl.BlockSpec((None, D, tf), lambda i, e, f, n: (e, 0, f)),
                  pl.BlockSpec((None, tf, D), lambda i, e, f, n: (e, f, 0)),
                  pl.BlockSpec((tm, D), lambda i, e, f, n: (i, 0)),
                  pl.BlockSpec((None, 1, D), lambda i, e, f, n: ((1 + i // tiles_per_batch) * 6 + 5, 0, 0))],
        out_specs=pl.BlockSpec((tm, D), lambda i, e, f, n: (i, 0)),
        scratch_shapes=[pltpu.VMEM((tm // MOE_CHUNK, MOE_CHUNK, D), BF16),
                        pltpu.VMEM((tm // MOE_CHUNK, MOE_CHUNK, D), F32)])
    return pl.pallas_call(
        _moe_kernel, grid_spec=grid_spec, out_shape=jax.ShapeDtypeStruct((rows, D), F32), name="moe",
        compiler_params=_cp(("parallel", "arbitrary", "arbitrary"), _VMEM_LIMIT))(
            nch, fin, slott, slot, rgate, w1, w3, w2, h, mod3)


def _regroup_cols(w):
    hy, ssd, da, gate = 3072, 2592, 3072, 3072
    parts = [w[..., :hy], w[..., hy + ssd + da:hy + ssd + da + gate], w[..., hy + ssd:hy + ssd + da],
             w[..., hy:hy + ssd]]
    pad = jnp.zeros(w.shape[:-1] + (_PCOLS - hy - ssd - da - gate,), w.dtype)
    return jnp.concatenate(parts + [pad], axis=-1)


def kernel(x, c, ctx, c_ctx, w_mod, b_mod, norm1_g, norm2_g, w_in, hy_conv_w, hy_conv_b, hy_w1, hy_b1, hy_w2, hy_b2,
           hy_w3, hy_b3, hy_w4, hy_freq, hy_bias, ssd_conv_w, ssd_conv_b, ssd_dt_bias, ssd_a_log, ssd_d, ssd_norm_g,
           da_q_norm, da_k_norm, da_lambda, da_subln_g, w_branch, w_out, ffn_w1, ffn_w3, ffn_w2, router_w, moe_w1,
           moe_w3, moe_w2):
    B, SEQ, _ = x.shape
    CTX = ctx.shape[1]
    assert CTX == ROW_TILE and SEQ % ROW_TILE == 0 and B % 2 == 0 and B + 1 <= 8
    T = CTX + SEQ
    tpb = T // ROW_TILE
    depth = w_in.shape[0]
    R = B * T

    h = jnp.concatenate([ctx, x], axis=1).reshape(R, D)
    cvec = jnp.zeros((8, D), F32).at[0].set(c_ctx).at[1:1 + B].set(c)
    cos_t, sin_t = _rope_tables(SEQ, CTX)
    tabs_l, tabs_c = _fft_tables(SEQ), _fft_tables(CTX)
    nb = D // LANES

    for i in range(depth):
        need_ctx = i < depth - 1
        mod = _mod(cvec, w_mod[i], b_mod[i][None])
        mod3 = mod.reshape(8 * 6, 1, D)
        xn = _normmod(h, norm1_g[i][None], mod3, T, CTX)
        p2 = _matmul(xn, _regroup_cols(w_in[i].astype(BF16)), T // 2, 2048, BF16)
        p3 = p2.reshape(B, T, _PCOLS)

        cw, cb = hy_conv_w[i], hy_conv_b[i][None]
        hy_args = (hy_w1[i], hy_b1[i], hy_w2[i], hy_b2[i], hy_w3[i], hy_b3[i], hy_w4[i], hy_freq[i])
        kh = _hy_kfft(_hy_filters(SEQ, *hy_args), hy_bias[i], tabs_l, SEQ)
        z1 = _hy_conv(p3, 0, p3, nb, cw, cb, kh, 0, tabs_l, SEQ, CTX, SEQ, 0, True)
        y_hy = _hy_conv(z1, 0, p3, 2 * nb, cw, cb, kh, 1, tabs_l, SEQ, CTX, SEQ, CTX, False)
        if need_ctx:
            khc = _hy_kfft(_hy_filters(CTX, *hy_args), hy_bias[i], tabs_c, CTX)
            z1c = _hy_conv(p3, 0, p3, nb, cw, cb, khc, 0, tabs_c, CTX, 0, CTX, 0, True)
            y_hy = _hy_conv(z1c, 0, p3, 2 * nb, cw, cb, khc, 1, tabs_c, CTX, 0, CTX, 0, False, alias_into=y_hy)

        xbc = _ssd_pre(p3, ssd_conv_w[i], ssd_conv_b[i][None], CTX)
        dtb = jnp.pad(ssd_dt_bias[i].reshape(1, 2 * SSD_HEADS), ((0, 0), (0, LANES - 2 * SSD_HEADS)))
        arow = jnp.pad(-jnp.exp(ssd_a_log[i].astype(F32)).reshape(1, 2 * SSD_HEADS),
                       ((0, 0), (0, LANES - 2 * SSD_HEADS)))
        y_rev = _ssd_scan(xbc, p3, dtb, arow, True, CTX)
        dskip = jnp.repeat(ssd_d[i].astype(F32), SSD_HEADDIM)[None]
        y_ssd = _ssd_scan(xbc, p3, dtb, arow, False, CTX, fin=(y_rev, dskip, ssd_norm_g[i][None]))

        lam_init = 0.8 - 0.6 * math.exp(-0.3 * (i + 1))
        gq = jnp.tile(da_q_norm[i], D // DA_HEAD_DIM)[None]
        gk = jnp.tile(da_k_norm[i], D // DA_HEAD_DIM)[None]
        qn, kn, vt = _da_pre(p2, gq, gk, cos_t, sin_t, tpb)
        score_bound = 8.0 * jnp.max(jnp.abs(da_q_norm[i])) * jnp.max(jnp.abs(da_k_norm[i]))
        safe = (score_bound <= ATTN_SAFE_SCORE).astype(jnp.int32).reshape(1)
        da_args = (safe, da_lambda[i], qn.reshape(B, T, D), kn.reshape(B, T, D), vt, da_subln_g[i][:, None], lam_init,
                   CTX)
        y_da = _da_attn(*da_args, False)
        y_dac = _da_attn(*da_args, True) if need_ctx else y_da

        wb, wo = w_branch[i].astype(BF16), w_out[i].astype(BF16)
        flat = lambda a: a.reshape(-1, D)
        j = i // 2
        if i % 2 == 0:
            h, fin = _merge(flat(y_hy), flat(y_ssd), flat(y_da), flat(y_dac), p2, h, wb, wo, mod3, norm2_g[i][None],
                            tpb, latent_only=not need_ctx)
            assert need_ctx, "dense FFN layers are expected to carry context rows"
            h = _ffn(fin, ffn_w1[j].astype(BF16), ffn_w3[j].astype(BF16), ffn_w2[j].astype(BF16), h, mod3, T, CTX)
        else:
            assert not need_ctx, "expert layers are expected to be latent only"
            rw = jnp.pad(router_w[j], ((0, 0), (0, LANES - N_EXPERTS)))
            h, fin, rgate = _merge(flat(y_hy), flat(y_ssd), flat(y_da), flat(y_dac), p2, h, wb, wo, mod3,
                                   norm2_g[i][None], tpb, latent_only=True, router_w=rw)
            h = _moe(fin, rgate, moe_w1[j].astype(BF16), moe_w3[j].astype(BF16), moe_w2[j].astype(BF16), h, mod3,
                     SEQ)
    return h.reshape(B, SEQ, D)
```

```python
import math
from functools import partial

import numpy as np
import jax
import jax.numpy as jnp
from jax import lax
from jax.experimental import pallas as pl
from jax.experimental.pallas import tpu as pltpu

F32, BF16 = jnp.float32, jnp.bfloat16
HIGHEST = lax.Precision.HIGHEST

D = 1024
EPS = 1e-6
GRID_W = 64
ROW_TILE = 256
LANES = 128
SSD_CHUNK = 128
SSD_HEADS = 16
SSD_HEADDIM = 64
DA_HEADS = 8
DA_HEAD_DIM = 64
N_EXPERTS = 8
HY_BANDS = 16
HY_EMB = 2 * HY_BANDS + 1
HY_FFN = 64

_PCOL_HY = 0
_PCOL_GATE = 3072
_PCOL_DA = 6144
_PCOL_Z = 9216
_PCOL_XBC = 10240
_PCOL_DT = 11776
_PCOLS = 12288
_VMEM_LIMIT = 56 * 1024 * 1024


def _cp(sem, vmem=None):
    return pltpu.CompilerParams(dimension_semantics=sem, vmem_limit_bytes=vmem)


def _dot(a, b, prec=None):
    return jnp.dot(a, b, preferred_element_type=F32, precision=prec)


def _silu(x):
    return x * jax.nn.sigmoid(x)


def _mod_kernel(c_ref, w_ref, b_ref, o_ref):
    o_ref[...] = _dot(_silu(c_ref[...]), w_ref[...], HIGHEST) + b_ref[...]


def _mod(cvec, w, b):
    return pl.pallas_call(
        _mod_kernel, grid=(6,),
        in_specs=[pl.BlockSpec((8, D), lambda j: (0, 0)), pl.BlockSpec((D, D), lambda j: (0, j)),
                  pl.BlockSpec((1, D), lambda j: (0, j))],
        out_specs=pl.BlockSpec((8, D), lambda j: (0, j)),
        out_shape=jax.ShapeDtypeStruct((8, 6 * D), F32), name="mod")(cvec, w, b)


def _modrow(t, tpb):
    return jnp.where(t % tpb == 0, 0, 1 + t // tpb)


def _mod_spec(k, tpb):
    return pl.BlockSpec((None, 1, D), lambda t: (_modrow(t, tpb) * 6 + k, 0, 0))


def _normmod_kernel(h_ref, g_ref, shc_ref, scc_ref, shl_ref, scl_ref, o_ref, *, ctx_rows, tiles_per_batch):
    x = h_ref[...]
    tm = x.shape[0]
    r = (pl.program_id(0) % tiles_per_batch) * tm + lax.broadcasted_iota(jnp.int32, (tm, 1), 0)
    is_ctx = r < ctx_rows
    y = x * lax.rsqrt(jnp.mean(x * x, axis=-1, keepdims=True) + EPS) * g_ref[...]
    o_ref[...] = (y * (1.0 + jnp.where(is_ctx, scc_ref[...], scl_ref[...]))
                  + jnp.where(is_ctx, shc_ref[...], shl_ref[...])).astype(o_ref.dtype)


def _normmod(h, g, mod3, T, CTX):
    rows = h.shape[0]
    tpb = 4
    tm = T // tpb
    ctx = lambda k: pl.BlockSpec((None, 1, D), lambda t: (k, 0, 0))
    lat = lambda k: pl.BlockSpec((None, 1, D), lambda t: ((1 + t // tpb) * 6 + k, 0, 0))
    return pl.pallas_call(
        partial(_normmod_kernel, ctx_rows=CTX, tiles_per_batch=tpb), grid=(rows // tm,),
        in_specs=[pl.BlockSpec((tm, D), lambda t: (t, 0)), pl.BlockSpec((1, D), lambda t: (0, 0)),
                  ctx(0), ctx(1), lat(0), lat(1)],
        out_specs=pl.BlockSpec((tm, D), lambda t: (t, 0)),
        out_shape=jax.ShapeDtypeStruct((rows, D), BF16), name="normmod",
        compiler_params=_cp(("parallel",)))(h, g, mod3, mod3, mod3, mod3)


def _mm_kernel(a_ref, b_ref, o_ref):
    o_ref[...] = _dot(a_ref[...], b_ref[...]).astype(o_ref.dtype)


def _matmul(a, b, tm, tn, out_dtype):
    m, k = a.shape
    n = b.shape[1]
    return pl.pallas_call(
        _mm_kernel, grid=(n // tn, m // tm),
        in_specs=[pl.BlockSpec((tm, k), lambda j, i: (i, 0)), pl.BlockSpec((k, tn), lambda j, i: (0, j))],
        out_specs=pl.BlockSpec((tm, tn), lambda j, i: (i, j)),
        out_shape=jax.ShapeDtypeStruct((m, n), out_dtype), name="inproj",
        compiler_params=_cp(("parallel", "parallel"), _VMEM_LIMIT))(a, b)


def _hyfilt_kernel(f_ref, w1, b1, w2, b2, w3, b3, w4, fr_ref, dl_ref, o_ref, h_ref, win_ref, *, shift):
    @pl.when(pl.program_id(1) == 0)
    def _():
        f = f_ref[...]
        fr = fr_ref[...]
        h = jnp.sin(fr[0:1] * (_dot(f, w1[...], HIGHEST) + b1[...]))
        h = jnp.sin(fr[1:2] * (_dot(h, w2[...], HIGHEST) + b2[...]))
        h = jnp.sin(fr[2:3] * (_dot(h, w3[...], HIGHEST) + b3[...]))
        h_ref[...] = h.astype(BF16)
        win_ref[...] = jnp.exp(-f[:, 0:1] * dl_ref[...]) + shift

    o_ref[...] = _dot(h_ref[...], w4[...]) * win_ref[...]


def _hy_filters(L, w1, b1, w2, b2, w3, b3, w4, freq):
    t = jnp.linspace(0.0, 1.0, L, dtype=F32)[:, None]
    w = 2.0 * math.pi * jnp.arange(L, dtype=F32)[:, None] / L
    f = jnp.linspace(1e-4, HY_BANDS - 1, HY_BANDS, dtype=F32)[None, :]
    feats = jnp.concatenate([t, jnp.cos(f * w), -jnp.sin(f * w)], axis=-1)
    feats = jnp.pad(feats, ((0, 0), (0, HY_FFN - HY_EMB)))
    w1p = jnp.pad(w1, ((0, HY_FFN - HY_EMB), (0, 0)))
    max_decay = math.log(1e-2) / 0.3
    min_decay = math.log(1e-2) / 1.5
    deltas = jnp.abs(jnp.linspace(min_decay, max_decay, D, dtype=F32))[None, :]
    tl = min(L, 512)
    full = lambda r, c: pl.BlockSpec((r, c), lambda i, j: (0, 0))
    return pl.pallas_call(
        partial(_hyfilt_kernel, shift=0.05), grid=(L // tl, 4),
        in_specs=[pl.BlockSpec((tl, HY_FFN), lambda i, j: (i, 0)), full(HY_FFN, HY_FFN), full(1, HY_FFN),
                  full(HY_FFN, HY_FFN), full(1, HY_FFN), full(HY_FFN, HY_FFN), full(1, HY_FFN),
                  pl.BlockSpec((HY_FFN, D), lambda i, j: (0, j)), full(3, HY_FFN), full(1, D)],
        out_specs=pl.BlockSpec((tl, D), lambda i, j: (i, j)),
        out_shape=jax.ShapeDtypeStruct((L, 4 * D), F32), name="hyfilt",
        scratch_shapes=[pltpu.VMEM((tl, HY_FFN), BF16), pltpu.VMEM((tl, D), F32)],
        compiler_params=_cp(("parallel", "arbitrary")))(
            feats, w1p, b1[None], w2, b2[None], w3, b3[None], w4.astype(BF16), freq, deltas)


def _fft_split(L):
    n = 2 * L
    n2 = 128 if L >= 2048 else n // 32
    return n // n2, n2


def _fft_tables(L):
    n1_, n2_ = _fft_split(L)
    n, h1 = 2 * L, n1_ // 2
    n2 = np.arange(n2_)[:, None, None]
    k1 = np.arange(n1_)[None, :, None]
    n1 = np.arange(h1)[None, None, :]
    g = np.exp(-2j * np.pi * (n2 * k1 / n + n1 * k1 / n1_))
    ma = np.concatenate([np.concatenate([g.real, -g.imag], 2), np.concatenate([g.imag, g.real], 2)], 1)
    kk = np.arange(n2_)
    f = np.exp(-2j * np.pi * np.outer(kk, kk) / n2_)
    mc = np.block([[f.real, -f.imag], [f.imag, f.real]])
    mci = np.block([[f.real, f.imag], [-f.imag, f.real]])
    gi = np.exp(2j * np.pi * (n2 * np.arange(n1_)[None, None, :] / n
                              + np.arange(h1)[None, :, None] * np.arange(n1_)[None, None, :] / n1_)) / n
    mai = np.concatenate([np.concatenate([gi.real, -gi.imag], 2), np.concatenate([gi.imag, gi.real], 2)], 1)
    cast = lambda a: jnp.asarray(a, F32).astype(BF16)
    return cast(ma), cast(ma[:, :, :h1]), cast(mc), cast(mci), cast(mai)


def _kfft_kernel(kf_ref, kb_ref, bias_ref, mar_ref, mc_ref, o_ref, af_ref, ab_ref, *, N1, N2):
    H1, PA = N1 // 2, 2 * N1 + 8

    def stage_a(n2, c):
        for src, dst in ((kf_ref, af_ref), (kb_ref, ab_ref)):
            slab = src[pl.ds(n2, H1, stride=N2), :].astype(BF16)
            dst[pl.ds(pl.multiple_of(n2 * PA, 8), 2 * N1), :] = _dot(mar_ref[n2], slab)
        return c

    lax.fori_loop(0, N2, stage_a, 0, unroll=8)
    bias = bias_ref[...]

    def stage_c(k1, c):
        def rows(ref):
            return jnp.concatenate([ref[pl.ds(k1, N2, stride=PA), :], ref[pl.ds(N1 + k1, N2, stride=PA), :]], 0)
        x = _dot(mc_ref[...], jnp.concatenate([rows(af_ref), rows(ab_ref)], 1).astype(BF16))
        xf, xb = x[:, :LANES], x[:, LANES:]
        o_ref[k1] = jnp.concatenate([xf[:N2] + xb[:N2] + bias, xf[N2:] - xb[N2:]], 0).astype(o_ref.dtype)
        return c

    lax.fori_loop(0, N1, stage_c, 0, unroll=16)


def _hy_kfft(k4, bias, tabs, L):
    N1, N2 = _fft_split(L)
    _, mar, mc, _, _ = tabs
    nb = D // LANES
    return pl.pallas_call(
        partial(_kfft_kernel, N1=N1, N2=N2), grid=(2, nb),
        in_specs=[pl.BlockSpec((L, LANES), lambda o, j: (0, o * 2 * nb + j)),
                  pl.BlockSpec((L, LANES), lambda o, j: (0, o * 2 * nb + nb + j)),
                  pl.BlockSpec((None, 1, LANES), lambda o, j: (o, 0, j)),
                  pl.BlockSpec(mar.shape, lambda o, j: (0, 0, 0)), pl.BlockSpec(mc.shape, lambda o, j: (0, 0))],
        out_specs=pl.BlockSpec((None, N1, 2 * N2, LANES), lambda o, j: (o, 0, 0, j)),
        out_shape=jax.ShapeDtypeStruct((2, N1, 2 * N2, D), BF16),
        scratch_shapes=[pltpu.VMEM((N2 * (2 * N1 + 8), LANES), F32)] * 2, name="hykfft",
        compiler_params=_cp(("parallel", "parallel"), _VMEM_LIMIT))(k4, k4, bias[:, None, :], mar, mc)


def _hyconv_kernel(sig_ref, gate_ref, cws_ref, cbs_ref, cwg_ref, cbg_ref, kh_ref, ma_ref, mc_ref, mci_ref,
                   mai_ref, *rest, L, N1, N2, in_row0, out_row0, sig_conv, aliased):
    if aliased:
        rest = rest[1:]
    o_ref, sig0, sig1, gat0, gat1, a_ref = rest
    H1, PA = N1 // 2, 2 * N1 + 8
    NP, PADR = N2 + 8, 16
    sigs, gats = (sig0, sig1), (gat0, gat1)

    def stage(ref, nat, bb, r0):
        nat[0:PADR, :] = jnp.zeros((PADR, LANES), F32)
        nat[PADR + H1 * NP:PADR + H1 * NP + 8, :] = jnp.zeros((8, LANES), F32)
        for n1 in range(H1):
            nat[PADR + n1 * NP:PADR + n1 * NP + N2, :] = ref[bb, r0 + n1 * N2:r0 + (n1 + 1) * N2, :].astype(F32)

    def slab(nat, n2, w, b):
        u = nat[pl.ds(PADR + n2, H1, stride=NP), :]
        if w is None:
            return u
        prev = jnp.where(n2 == 0, PADR - NP + N2 - 1, PADR + n2 - 1)
        nxt = jnp.where(n2 == N2 - 1, PADR + NP, PADR + n2 + 1)
        return (w[0:1] * nat[pl.ds(prev, H1, stride=NP), :] + w[1:2] * u
                + w[2:3] * nat[pl.ds(nxt, H1, stride=NP), :] + b)

    for bb in range(2):
        stage(sig_ref, sigs[bb], bb, in_row0 if sig_conv else 0)
        stage(gate_ref, gats[bb], bb, in_row0)
    ws, bs = (cws_ref[...], cbs_ref[...]) if sig_conv else (None, None)
    wg, bg = cwg_ref[...], cbg_ref[...]

    def stage_a(n2, c):
        st = jnp.concatenate([slab(sigs[0], n2, ws, bs), slab(sigs[1], n2, ws, bs)], 0).astype(BF16)
        a_ref[pl.ds(pl.multiple_of(n2 * PA, 8), 2 * N1), :] = _dot(ma_ref[n2], st)
        return c

    lax.fori_loop(0, N2, stage_a, 0, unroll=16)

    def stage_c(i, c):
        k1 = 2 * i

        def rows(k):
            return jnp.concatenate([a_ref[pl.ds(k, N2, stride=PA), :], a_ref[pl.ds(N1 + k, N2, stride=PA), :]], 0)

        x = _dot(mc_ref[...], jnp.concatenate([rows(k1), rows(k1 + 1)], 1).astype(BF16))
        kh = jnp.concatenate([kh_ref[k1], kh_ref[k1 + 1]], 1).astype(F32)
        xr, xi, kr, ki = x[:N2], x[N2:], kh[:N2], kh[N2:]
        y = jnp.concatenate([xr * kr - xi * ki, xr * ki + xi * kr], 0)
        cc = _dot(mci_ref[...], y.astype(BF16))
        for j in range(2):
            a_ref[pl.ds(k1 + j, N2, stride=PA), :] = cc[:N2, j * LANES:(j + 1) * LANES]
            a_ref[pl.ds(N1 + k1 + j, N2, stride=PA), :] = cc[N2:, j * LANES:(j + 1) * LANES]
        return c

    lax.fori_loop(0, N1 // 2, stage_c, 0, unroll=16)

    def stage_ai(n2, c):
        st = a_ref[pl.ds(pl.multiple_of(n2 * PA, 8), 2 * N1), :]
        g = jnp.concatenate([slab(gats[0], n2, wg, bg), slab(gats[1], n2, wg, bg)], 0)
        r = _dot(mai_ref[n2], st.astype(BF16)) * g
        sig0[pl.ds(PADR + n2, H1, stride=NP), :] = r[:H1]
        sig1[pl.ds(PADR + n2, H1, stride=NP), :] = r[H1:]
        return c

    lax.fori_loop(0, N2, stage_ai, 0, unroll=16)

    for bb in range(2):
        if out_row0:
            o_ref[bb, 0:out_row0, :] = jnp.zeros((out_row0, LANES), o_ref.dtype)
        for n1 in range(H1):
            o_ref[bb, out_row0 + n1 * N2:out_row0 + (n1 + 1) * N2, :] = (
                sigs[bb][PADR + n1 * NP:PADR + n1 * NP + N2, :].astype(o_ref.dtype))


def _hy_conv(sig, sig_col, gate, gate_col, cw, cb, kh, order, tabs, L, in_row0, out_rows, out_row0, sig_conv,
             alias_into=None):
    N1, N2 = _fft_split(L)
    ma, _, mc, mci, mai = tabs
    H1 = N1 // 2
    B = gate.shape[0]
    nb = D // LANES
    scol = sig_col if sig_conv else 0
    cspec = lambda r, c0: pl.BlockSpec((r, LANES), lambda j, q: (0, c0 + j))
    const = lambda a: pl.BlockSpec(a.shape, lambda j, q: (0,) * a.ndim)
    in_specs = [pl.BlockSpec((2, sig.shape[1], LANES), lambda j, q: (q, 0, sig_col + j)),
                pl.BlockSpec((2, gate.shape[1], LANES), lambda j, q: (q, 0, gate_col + j)),
                cspec(3, scol), cspec(1, scol), cspec(3, gate_col), cspec(1, gate_col),
                pl.BlockSpec((None, N1, 2 * N2, LANES), lambda j, q: (order, 0, 0, j)),
                const(ma), const(mc), const(mci), const(mai)]
    args = [sig, gate, cw, cb, cw, cb, kh, ma, mc, mci, mai]
    aliases = {}
    if alias_into is not None:
        in_specs.append(pl.BlockSpec(memory_space=pl.ANY))
        args.append(alias_into)
        aliases = {len(args) - 1: 0}
        out_total = alias_into.shape[1]
    else:
        out_total = out_row0 + out_rows
    return pl.pallas_call(
        partial(_hyconv_kernel, L=L, N1=N1, N2=N2, in_row0=in_row0, out_row0=out_row0, sig_conv=sig_conv,
                aliased=alias_into is not None),
        grid=(nb, B // 2), in_specs=in_specs,
        out_specs=pl.BlockSpec((2, out_row0 + out_rows, LANES), lambda j, q: (q, 0, j)),
        out_shape=jax.ShapeDtypeStruct((B, out_total, D), BF16),
        scratch_shapes=[pltpu.VMEM((H1 * (N2 + 8) + 24, LANES), F32)] * 4
        + [pltpu.VMEM((N2 * (2 * N1 + 8), LANES), F32)],
        input_output_aliases=aliases, name="hyconv",
        compiler_params=_cp(("parallel", "parallel"), _VMEM_LIMIT))(*args)


def _ssdpre_kernel(x_ref, w_ref, b_ref, o_ref, nat, *, T, CTX):
    W = x_ref.shape[1]
    nat[0:8, :] = jnp.zeros((8, W), F32)
    nat[8 + T:16 + T, :] = jnp.zeros((8, W), F32)
    for i in range(T // ROW_TILE):
        nat[8 + i * ROW_TILE:8 + (i + 1) * ROW_TILE, :] = x_ref[i * ROW_TILE:(i + 1) * ROW_TILE, :].astype(F32)
    w = w_ref[...]
    b = b_ref[...]
    row = lax.broadcasted_iota(jnp.int32, (ROW_TILE, 1), 0)
    for i in range(T // ROW_TILE):
        r0 = 8 + i * ROW_TILE
        xm = nat[r0 - 1:r0 - 1 + ROW_TILE, :]
        xp = nat[r0 + 1:r0 + 1 + ROW_TILE, :]
        if i * ROW_TILE == CTX:
            xm = jnp.where(row == 0, 0.0, xm)
        if (i + 1) * ROW_TILE == CTX:
            xp = jnp.where(row == ROW_TILE - 1, 0.0, xp)
        u = w[0:1] * xm + w[1:2] * nat[r0:r0 + ROW_TILE, :] + w[2:3] * xp + b
        o_ref[i * ROW_TILE:(i + 1) * ROW_TILE, :] = _silu(u).astype(o_ref.dtype)


def _ssd_pre(p3, cw, cb, CTX):
    B, T, _ = p3.shape
    W = 256
    nblk = cw.shape[1] // W
    return pl.pallas_call(
        partial(_ssdpre_kernel, T=T, CTX=CTX), grid=(B, nblk),
        in_specs=[pl.BlockSpec((None, T, W), lambda b, j: (b, 0, _PCOL_XBC // W + j)),
                  pl.BlockSpec((3, W), lambda b, j: (0, j)), pl.BlockSpec((1, W), lambda b, j: (0, j))],
        out_specs=pl.BlockSpec((None, T, W), lambda b, j: (b, 0, j)),
        out_shape=jax.ShapeDtypeStruct((B, T, cw.shape[1]), BF16),
        scratch_shapes=[pltpu.VMEM((T + 16, W), F32)], name="ssdpre",
        compiler_params=_cp(("parallel", "parallel")))(p3, cw, cb)


def _ssdscan_kernel(xbc_ref, dt_ref, dtb_ref, a_ref, tri_ref, *rest, rev):
    if rev:
        o_ref, s_ref = rest
    else:
        z_ref, yr_ref, dsk_ref, ng_ref, o_ref, s_ref, ybuf = rest
    Q = SSD_CHUNK

    @pl.when(pl.program_id(1) == 0)
    def _():
        s_ref[...] = jnp.zeros(s_ref.shape, F32)

    x = xbc_ref[...]
    xs, Bm, Cm = x[:, :D], x[:, D:D + 2 * LANES], x[:, D + 2 * LANES:]
    dtr = dt_ref[...].astype(F32) + dtb_ref[...]
    dt = jnp.maximum(dtr, 0.0) + jnp.log(1.0 + jnp.exp(-jnp.abs(dtr)))
    tri = tri_ref[...]
    mask = tri > 0.5
    cs = _dot(tri, dt * a_ref[...], HIGHEST)
    csT = cs.T
    dtT = dt.T
    last = 0 if rev else Q - 1
    col0 = SSD_HEADS if rev else 0
    lane_lo = lax.broadcasted_iota(jnp.int32, (1, LANES), 1) < SSD_HEADDIM
    for g in range(2):
        Bg = Bm[:, g * LANES:(g + 1) * LANES]
        Cg = Cm[:, g * LANES:(g + 1) * LANES]
        CB = lax.dot_general(Cg, Bg, (((1,), (1,)), ((), ())), preferred_element_type=F32)
        BgT = Bg.astype(F32).T
        Cgf = Cg.astype(F32)
        for pr in range(4):
            hp = g * 4 + pr
            xs_pair = xs[:, hp * LANES:(hp + 1) * LANES]
            Sp = s_ref[hp]
            rhs = jnp.concatenate([xs_pair, Sp.astype(BF16)], 0)
            ys, sts = [], []
            for hh in range(2):
                col = col0 + 2 * hp + hh
                csl = jnp.broadcast_to(cs[:, col:col + 1], (Q, Q))
                csr = csT[col:col + 1, :]
                dtr_row = dtT[col:col + 1, :]
                Lm = jnp.exp(jnp.where(mask, csl - csr, -1e30))
                lhs = jnp.concatenate([CB * Lm * dtr_row, Cgf * jnp.exp(csl)], 1).astype(BF16)
                ys.append(_dot(lhs, rhs))
                tot = csT[col:col + 1, last:last + 1]
                w_row = jnp.exp(tot - csr) * dtr_row
                st = _dot((BgT * w_row).astype(BF16), xs_pair)
                sts.append(jnp.exp(tot) * Sp + st)
            y_pair = jnp.where(lane_lo, ys[0], ys[1])
            s_ref[hp] = jnp.where(lane_lo, sts[0], sts[1])
            sl = slice(hp * LANES, (hp + 1) * LANES)
            if rev:
                o_ref[:, sl] = y_pair.astype(o_ref.dtype)
            else:
                ybuf[:, sl] = y_pair + yr_ref[:, sl].astype(F32) + xs_pair.astype(F32) * dsk_ref[:, sl]
    if not rev:
        y = ybuf[...] * _silu(z_ref[...].astype(F32))
        half = D // 2
        parts = []
        for g in range(2):
            yg = y[:, g * half:(g + 1) * half]
            parts.append(yg * lax.rsqrt(jnp.mean(yg * yg, axis=-1, keepdims=True) + EPS))
        o_ref[...] = (jnp.concatenate(parts, 1) * ng_ref[...]).astype(o_ref.dtype)


def _ssd_scan(xbc, p3, dtb, arow, rev, CTX, fin=None):
    B, T, _ = xbc.shape
    Q = SSD_CHUNK
    nch, nc = T // Q, CTX // Q
    if rev:
        chunk = lambda s: jnp.where(s < nc, nc - 1 - s, nch - 1 + nc - s)
        tri = jnp.asarray(np.triu(np.ones((Q, Q), np.float32)))
    else:
        chunk = lambda s: s
        tri = jnp.asarray(np.tril(np.ones((Q, Q), np.float32)))
    row = lambda w: pl.BlockSpec((1, w), lambda b, s: (0, 0))
    in_specs = [pl.BlockSpec((None, Q, xbc.shape[2]), lambda b, s: (b, chunk(s), 0)),
                pl.BlockSpec((None, Q, LANES), lambda b, s: (b, chunk(s), _PCOL_DT // LANES)),
                row(LANES), row(LANES), pl.BlockSpec((Q, Q), lambda b, s: (0, 0))]
    args = [xbc, p3, dtb, arow, tri]
    scratch = [pltpu.VMEM((SSD_HEADS // 2, LANES, LANES), F32)]
    if not rev:
        yrev, dskip, ng = fin
        in_specs += [pl.BlockSpec((None, Q, D), lambda b, s: (b, s, _PCOL_Z // D)),
                     pl.BlockSpec((None, Q, D), lambda b, s: (b, s, 0)), row(D), row(D)]
        args += [p3, yrev, dskip, ng]
        scratch.append(pltpu.VMEM((Q, D), F32))
    return pl.pallas_call(
        partial(_ssdscan_kernel, rev=rev), grid=(B, nch), in_specs=in_specs,
        out_specs=pl.BlockSpec((None, Q, D), lambda b, s: (b, chunk(s), 0)),
        out_shape=jax.ShapeDtypeStruct((B, T, D), BF16), scratch_shapes=scratch,
        name="ssdrev" if rev else "ssdfwd",
        compiler_params=_cp(("parallel", "arbitrary")))(*args)


def _dapre_kernel(q_ref, k_ref, v_ref, gq_ref, gk_ref, cos_ref, sin_ref, ones_ref, qo_ref, ko_ref, vt_ref):
    ones = ones_ref[...]
    cosf = jnp.concatenate([cos_ref[...]] * (D // LANES), 1)
    sinf = jnp.concatenate([sin_ref[...]] * (D // LANES), 1)
    lane = lax.broadcasted_iota(jnp.int32, (1, D), 1)
    first_half = (lane % 32) < 16

    def norm_rope(x_ref, g_ref, scale):
        x = x_ref[...].astype(F32)
        sq = x * x
        hi = sq.astype(BF16)
        lo = (sq - hi.astype(F32)).astype(BF16)
        parts = []
        for blk in range(D // 256):
            sl = slice(blk * 256, (blk + 1) * 256)
            parts.append(_dot(hi[:, sl], ones) + _dot(lo[:, sl], ones))
        ss = jnp.concatenate(parts, 1)
        y = x * lax.rsqrt(ss * (1.0 / DA_HEAD_DIM) + EPS) * g_ref[...]
        partner = jnp.where(first_half, pltpu.roll(y, D - 16, 1), pltpu.roll(y, 16, 1))
        return ((y * cosf + partner * sinf) * scale).astype(BF16)

    qo_ref[...] = norm_rope(q_ref, gq_ref, LOG2E * DA_HEAD_DIM ** -0.5)
    ko_ref[...] = norm_rope(k_ref, gk_ref, 1.0)
    vt_ref[...] = v_ref[...].astype(F32).T.astype(BF16)


def _rope_tables(L, CTX):
    rows = L // GRID_W
    r = np.repeat(np.arange(rows), GRID_W)
    c = np.tile(np.arange(GRID_W), rows)
    nf = DA_HEAD_DIM // 4
    inv = jnp.asarray(10000.0, F32) ** (-jnp.arange(nf, dtype=F32) / nf)
    ang = jnp.stack([jnp.asarray(r), jnp.asarray(c)], -1).astype(F32)[:, :, None] * inv
    cos, sin = jnp.cos(ang), jnp.sin(ang)
    cos64 = jnp.stack([cos, cos], 2).reshape(L, DA_HEAD_DIM)
    sin64 = jnp.stack([-sin, sin], 2).reshape(L, DA_HEAD_DIM)
    cos_t = jnp.concatenate([jnp.ones((CTX, DA_HEAD_DIM), F32), cos64], 0)
    sin_t = jnp.concatenate([jnp.zeros((CTX, DA_HEAD_DIM), F32), sin64], 0)
    return jnp.tile(cos_t, (1, 2)), jnp.tile(sin_t, (1, 2))


def _da_pre(p2, gq, gk, cos_t, sin_t, tpb):
    rows = p2.shape[0]
    ones = jnp.asarray(np.kron(np.eye(256 // DA_HEAD_DIM), np.ones((DA_HEAD_DIM, DA_HEAD_DIM))), BF16)
    tab = pl.BlockSpec((ROW_TILE, LANES), lambda t: (t % tpb, 0))
    row = pl.BlockSpec((1, D), lambda t: (0, 0))
    out = pl.BlockSpec((ROW_TILE, D), lambda t: (t, 0))
    seg = lambda k: pl.BlockSpec((ROW_TILE, D), lambda t: (t, _PCOL_DA // D + k))
    return pl.pallas_call(
        _dapre_kernel, grid=(rows // ROW_TILE,),
        in_specs=[seg(0), seg(1), seg(2), row, row, tab, tab, pl.BlockSpec((256, 256), lambda t: (0, 0))],
        out_specs=[out, out, pl.BlockSpec((None, D, ROW_TILE), lambda t: (t // tpb, 0, t % tpb))],
        out_shape=[jax.ShapeDtypeStruct((rows, D), BF16)] * 2
        + [jax.ShapeDtypeStruct((rows // ROW_TILE // tpb, D, tpb * ROW_TILE), BF16)], name="dapre",
        compiler_params=_cp(("parallel",)))(p2, p2, p2, gq, gk, cos_t, sin_t, ones)


ATTN_KEY_BLOCK = 1024
ATTN_SAFE_SCORE = 30.0
LOG2E = 1.4426950408889634


def _attn_kernel(safe_ref, lam_ref, *refs, lam_init):
    lv = lam_ref[...]
    lam = (jnp.exp(jnp.sum(lv[0:1] * lv[1:2], keepdims=True)) - jnp.exp(jnp.sum(lv[2:3] * lv[3:4], keepdims=True))
           + lam_init)
    *q_refs, k_ref, vt_ref, g_ref, o_ref = refs
    q = jnp.concatenate([r[...] for r in q_refs], 0)
    tq = q.shape[0]
    lane_lo = lax.broadcasted_iota(jnp.int32, (1, LANES), 1) < DA_HEAD_DIM
    qcat = jnp.concatenate([jnp.where(lane_lo, q, jnp.zeros_like(q)), jnp.where(lane_lo, jnp.zeros_like(q), q)], 0)

    nt = (((1,), (1,)), ((), ()))

    def finish(ot):
        o = ot[:, :tq] - lam * ot[:, tq:]
        y = o * lax.rsqrt(jnp.mean(o * o, axis=0, keepdims=True) + EPS) * g_ref[...] * (1.0 - lam_init)
        o_ref[...] = y.T.astype(o_ref.dtype)

    def attend(shifted):
        nkeys = k_ref.shape[0]
        blocks = [(k0, min(k0 + ATTN_KEY_BLOCK, nkeys)) for k0 in range(0, nkeys, ATTN_KEY_BLOCK)]
        scores = lambda b: lax.dot_general(k_ref[b[0]:b[1], :], qcat, nt, preferred_element_type=F32)
        if shifted:
            m = jnp.full((1, 2 * tq), -jnp.inf, F32)
            for blk in blocks:
                m = jnp.maximum(m, jnp.max(scores(blk), axis=0, keepdims=True))
        l = jnp.zeros((1, 2 * tq), F32)
        acc = jnp.zeros((LANES, 2 * tq), F32)
        st = scores(blocks[0])
        for j, (k0, k1) in enumerate(blocks):
            st_next = scores(blocks[j + 1]) if j + 1 < len(blocks) else None
            e = jnp.exp2(st - m) if shifted else jnp.exp2(st)
            l = l + jnp.sum(e, axis=0, keepdims=True)
            acc = acc + _dot(vt_ref[:, k0:k1], e.astype(BF16))
            st = st_next
        finish(acc / l)

    @pl.when(safe_ref[0] == 1)
    def _():
        attend(False)

    @pl.when(safe_ref[0] != 1)
    def _():
        attend(True)


def _da_attn(safe, lam4, qn, kn, vt, subg_col, lam_init, CTX, context):
    B, T, _ = qn.shape
    tq = ROW_TILE
    if context:
        nq, nkeys, out_rows = 1, CTX, CTX
        q_specs = [pl.BlockSpec((None, tq, LANES), lambda b, h, i, s: (b, 0, h))]
    else:
        per_step = 4 if (T - CTX) % (4 * tq) == 0 else 2
        nq, nkeys, out_rows = (T - CTX) // (per_step * tq), T, T - CTX
        q0 = CTX // tq
        q_specs = [pl.BlockSpec((None, tq, LANES), lambda b, h, i, s, j=j: (b, q0 + per_step * i + j, h))
                   for j in range(per_step)]
    grid_spec = pltpu.PrefetchScalarGridSpec(
        num_scalar_prefetch=1, grid=(B, DA_HEADS, nq),
        in_specs=[pl.BlockSpec((4, DA_HEAD_DIM), lambda b, h, i, s: (0, 0))] + q_specs + [
            pl.BlockSpec((None, nkeys, LANES), lambda b, h, i, s: (b, 0, h)),
            pl.BlockSpec((None, LANES, nkeys), lambda b, h, i, s: (b, h, 0)),
            pl.BlockSpec((LANES, 1), lambda b, h, i, s: (0, 0))],
        out_specs=pl.BlockSpec((None, len(q_specs) * tq, LANES), lambda b, h, i, s: (b, i, h)))
    return pl.pallas_call(
        partial(_attn_kernel, lam_init=lam_init), grid_spec=grid_spec,
        out_shape=jax.ShapeDtypeStruct((B, out_rows, D), BF16), name="daattn",
        compiler_params=_cp(("parallel", "parallel", "parallel"), _VMEM_LIMIT))(
            safe, lam4, *([qn] * len(q_specs)), kn, vt, subg_col)


def _merge_kernel(yh_ref, ys_ref, yd_ref, ydc_ref, g0_ref, g1_ref, g2_ref, h_ref, wb_ref, wo_ref, gate_ref, n2_ref,
                  sh_ref, sc_ref, *rest, route, ctx_every):
    yd = yd_ref[...]
    if ctx_every:
        yd = jnp.where(pl.program_id(0) % ctx_every == 0, ydc_ref[...], yd)
    acc = None
    for i, (y, gl_ref) in enumerate(((yh_ref[...], g0_ref), (ys_ref[...], g1_ref), (yd, g2_ref))):
        term = jax.nn.sigmoid(gl_ref[...].astype(F32)) * _dot(y, wb_ref[i])
        acc = term if acc is None else acc + term
    hn = h_ref[...] + gate_ref[...] * _dot(acc.astype(BF16), wo_ref[...])
    xn = hn * lax.rsqrt(jnp.mean(hn * hn, axis=-1, keepdims=True) + EPS) * n2_ref[...]
    fin = xn * (1.0 + sc_ref[...]) + sh_ref[...]
    if not route:
        hout_ref, fin_ref = rest
    else:
        rw_ref, hout_ref, fin_ref, rg_ref = rest
        rw = rw_ref[...]
        fin_hi, rw_hi = fin.astype(BF16), rw.astype(BF16)
        fin_lo, rw_lo = (fin - fin_hi.astype(F32)).astype(BF16), (rw - rw_hi.astype(F32)).astype(BF16)
        logits = _dot(fin_hi, rw_hi) + _dot(fin_lo, rw_hi) + _dot(fin_hi, rw_lo)
        lane = lax.broadcasted_iota(jnp.int32, logits.shape, 1)
        logits = jnp.where(lane < N_EXPERTS, logits, -jnp.inf)
        m1 = jnp.max(logits, axis=1, keepdims=True)
        i1 = jnp.min(jnp.where(logits == m1, lane, LANES), axis=1, keepdims=True)
        rest_l = jnp.where(lane == i1, -jnp.inf, logits)
        m2 = jnp.max(rest_l, axis=1, keepdims=True)
        i2 = jnp.min(jnp.where(rest_l == m2, lane, LANES), axis=1, keepdims=True)
        e2 = jnp.exp(m2 - m1)
        rg_ref[...] = jnp.where(lane == i1, 1.0 / (1.0 + e2), 0.0) + jnp.where(lane == i2, e2 / (1.0 + e2), 0.0)
    hout_ref[...] = hn
    fin_ref[...] = fin.astype(fin_ref.dtype)


def _merge(yh, ys, yd, ydc, p2, h, wb, wo, mod3, n2g, tpb, latent_only, router_w=None):
    rows = h.shape[0]
    if latent_only:
        nt = rows // ROW_TILE // tpb * (tpb - 1)
        src = lambda t: t // (tpb - 1) * tpb + 1 + t % (tpb - 1)
    else:
        nt = rows // ROW_TILE
        src = lambda t: t
    tile = lambda cb: pl.BlockSpec((ROW_TILE, D), lambda t: (src(t), cb))
    mspec = lambda k: pl.BlockSpec((None, 1, D), lambda t: (_modrow(src(t), tpb) * 6 + k, 0, 0))
    if latent_only:
        yd_specs = [pl.BlockSpec((ROW_TILE, D), lambda t: (t, 0)), pl.BlockSpec((ROW_TILE, D), lambda t: (0, 0))]
    else:
        yd_specs = [pl.BlockSpec((ROW_TILE, D), lambda t: (t // tpb * (tpb - 1) + jnp.maximum(t % tpb - 1, 0), 0)),
                    pl.BlockSpec((ROW_TILE, D), lambda t: (t // tpb, 0))]
    in_specs = [tile(0), tile(0)] + yd_specs + [
                tile(_PCOL_GATE // D), tile(_PCOL_GATE // D + 1),
                tile(_PCOL_GATE // D + 2), tile(0), pl.BlockSpec((3, D, D), lambda t: (0, 0, 0)),
                pl.BlockSpec((D, D), lambda t: (0, 0)), mspec(2), pl.BlockSpec((1, D), lambda t: (0, 0)),
                mspec(3), mspec(4)]
    args = [yh, ys, yd, ydc, p2, p2, p2, h, wb, wo, mod3, n2g, mod3, mod3]
    otile = pl.BlockSpec((ROW_TILE, D), lambda t: (t, 0))
    out_specs = [otile, otile]
    out_shape = [jax.ShapeDtypeStruct((nt * ROW_TILE, D), F32), jax.ShapeDtypeStruct((nt * ROW_TILE, D), BF16)]
    if router_w is not None:
        in_specs.append(pl.BlockSpec((D, LANES), lambda t: (0, 0)))
        args.append(router_w)
        out_specs.append(pl.BlockSpec((ROW_TILE, LANES), lambda t: (t, 0)))
        out_shape.append(jax.ShapeDtypeStruct((nt * ROW_TILE, LANES), F32))
    return pl.pallas_call(
        partial(_merge_kernel, route=router_w is not None, ctx_every=0 if latent_only else tpb), grid=(nt,), in_specs=in_specs, out_specs=out_specs,
        out_shape=out_shape, name="merge", compiler_params=_cp(("parallel",), _VMEM_LIMIT))(*args)


def _ffn_kernel(x_ref, w1_ref, w3_ref, w2_ref, h_ref, gc_ref, gl_ref, o_ref, acc, *, ctx_rows, tiles_per_batch):
    f = pl.program_id(1)
    x = x_ref[...]
    a = (_silu(_dot(x, w1_ref[...])) * _dot(x, w3_ref[...])).astype(BF16)
    part = _dot(a, w2_ref[...])

    @pl.when(f == 0)
    def _():
        acc[...] = part

    @pl.when(f > 0)
    def _():
        acc[...] += part

    @pl.when(f == pl.num_programs(1) - 1)
    def _():
        tm = x_ref.shape[0]
        r = (pl.program_id(0) % tiles_per_batch) * tm + lax.broadcasted_iota(jnp.int32, (tm, 1), 0)
        gate = jnp.where(r < ctx_rows, gc_ref[...], gl_ref[...])
        o_ref[...] = h_ref[...] + gate * acc[...]


def _ffn(fin, w1, w3, w2, h, mod3, T, CTX):
    rows = h.shape[0]
    tpb = 4
    tm = T // tpb
    ff = w1.shape[1]
    tf = 1024
    return pl.pallas_call(
        partial(_ffn_kernel, ctx_rows=CTX, tiles_per_batch=tpb), grid=(rows // tm, ff // tf),
        in_specs=[pl.BlockSpec((tm, D), lambda i, f: (i, 0)), pl.BlockSpec((D, tf), lambda i, f: (0, f)),
                  pl.BlockSpec((D, tf), lambda i, f: (0, f)), pl.BlockSpec((tf, D), lambda i, f: (f, 0)),
                  pl.BlockSpec((tm, D), lambda i, f: (i, 0)),
                  pl.BlockSpec((None, 1, D), lambda i, f: (5, 0, 0)),
                  pl.BlockSpec((None, 1, D), lambda i, f: ((1 + i // tpb) * 6 + 5, 0, 0))],
        out_specs=pl.BlockSpec((tm, D), lambda i, f: (i, 0)),
        out_shape=jax.ShapeDtypeStruct((rows, D), F32), scratch_shapes=[pltpu.VMEM((tm, D), F32)], name="ffn",
        compiler_params=_cp(("parallel", "arbitrary"), _VMEM_LIMIT))(fin, w1, w3, w2, h, mod3, mod3)


MOE_TILE = 1024
MOE_CHUNK = 256


def _slots_kernel(rg_ref, tri_ref, slot_ref, slott_ref, cnt_ref):
    blk = tri_ref.shape[0]
    carry = jnp.zeros((1, LANES), F32)
    for i in range(rg_ref.shape[0] // blk):
        sel = rg_ref[i * blk:(i + 1) * blk, :] > 0.0
        self32 = jnp.where(sel, 1.0, 0.0)
        rank = _dot(tri_ref[...], self32.astype(BF16)) + carry
        slot = jnp.where(sel, rank, -1.0)
        slot_ref[i * blk:(i + 1) * blk, :] = slot
        slott_ref[:, i * blk:(i + 1) * blk] = slot.T
        carry = carry + jnp.sum(self32, axis=0, keepdims=True)
    cnt_ref[...] = jnp.broadcast_to(carry, cnt_ref.shape)


def _moe_slots(rgate, tm):
    rows = rgate.shape[0]
    blk = 256
    tri = jnp.asarray(np.tril(np.ones((blk, blk), np.float32), -1), BF16)
    return pl.pallas_call(
        _slots_kernel, grid=(rows // tm,),
        in_specs=[pl.BlockSpec((tm, LANES), lambda i: (i, 0)), pl.BlockSpec((blk, blk), lambda i: (0, 0))],
        out_specs=[pl.BlockSpec((tm, LANES), lambda i: (i, 0)), pl.BlockSpec((LANES, tm), lambda i: (0, i)),
                   pl.BlockSpec((8, LANES), lambda i: (i, 0))],
        out_shape=[jax.ShapeDtypeStruct((rows, LANES), F32), jax.ShapeDtypeStruct((LANES, rows), F32),
                   jax.ShapeDtypeStruct((rows // tm * 8, LANES), F32)], name="moeslots",
        compiler_params=_cp(("parallel",)))(rgate, tri)


def _moe_kernel(nfull_ref, half_ref, x_ref, slott_ref, slot_ref, rg_ref, w1_ref, w3_ref, w2_ref, h_ref, gl_ref, o_ref,
                xe_ref, ye_ref):
    i, e, f = pl.program_id(0), pl.program_id(1), pl.program_id(2)
    CH = MOE_CHUNK
    nfull = nfull_ref[i * N_EXPERTS + e]
    half = half_ref[i * N_EXPERTS + e]

    @pl.when(jnp.logical_and(e == 0, f == 0))
    def _():
        o_ref[...] = h_ref[...]

    def expert(r0, rows):
        xe = xe_ref[pl.ds(r0, rows), :]
        a = (_silu(_dot(xe, w1_ref[...])) * _dot(xe, w3_ref[...])).astype(BF16)
        return _dot(a, w2_ref[...])

    def gather(r0, rows):
        want = lax.broadcasted_iota(jnp.int32, (rows, 1), 0).astype(F32) + r0.astype(F32)
        pick = jnp.where(slott_ref[pl.ds(e, 1), :] == want, 1.0, 0.0).astype(BF16)
        xe_ref[pl.ds(r0, rows), :] = _dot(pick, x_ref[...]).astype(BF16)
        ye_ref[pl.ds(r0, rows), :] = expert(r0, rows)

    def for_chunks(body):
        def full(c, carry):
            body(pl.multiple_of(c * CH, CH), CH)
            return carry

        lax.fori_loop(0, nfull, full, 0)

        @pl.when(half == 1)
        def _():
            body(pl.multiple_of(nfull * CH, CH), CH // 2)

    @pl.when(f == 0)
    def _():
        for_chunks(gather)

    @pl.when(f == 1)
    def _():
        mine = lax.broadcasted_iota(jnp.int32, (1, LANES), 1) == e
        scol = jnp.sum(jnp.where(mine, slot_ref[...], 0.0), axis=1, keepdims=True)
        scale = jnp.sum(jnp.where(mine, rg_ref[...], 0.0), axis=1, keepdims=True) * gl_ref[...]

        def scatter(r0, rows):
            ye = (ye_ref[pl.ds(r0, rows), :] + expert(r0, rows)).astype(BF16)
            want = lax.broadcasted_iota(jnp.int32, (1, rows), 1).astype(F32) + r0.astype(F32)
            put = jnp.where(scol == want, 1.0, 0.0).astype(BF16)
            o_ref[...] += scale * _dot(put, ye)

        for_chunks(scatter)


def _moe(fin, rgate, w1, w3, w2, h, mod3, SEQ):
    rows = h.shape[0]
    tm = min(SEQ, MOE_TILE)
    ff = w1.shape[2]
    tf = ff // 2
    tiles_per_batch = SEQ // tm
    nt = rows // tm
    slot, slott, cnt = _moe_slots(rgate, tm)
    counts = cnt.reshape(nt, 8, LANES)[:, 0, :N_EXPERTS].astype(jnp.int32).reshape(nt * N_EXPERTS)
    rem = counts % MOE_CHUNK
    nfull = counts // MOE_CHUNK + (rem > MOE_CHUNK // 2).astype(jnp.int32)
    half = jnp.logical_and(rem > 0, rem <= MOE_CHUNK // 2).astype(jnp.int32)
    grid_spec = pltpu.PrefetchScalarGridSpec(
        num_scalar_prefetch=2, grid=(nt, N_EXPERTS, 2),
        in_specs=[pl.BlockSpec((tm, D), lambda i, e, f, n, hf: (i, 0)),
                  pl.BlockSpec((8, tm), lambda i, e, f, n, hf: (0, i)),
                  pl.BlockSpec((tm, LANES), lambda i, e, f, n, hf: (i, 0)),
                  pl.BlockSpec((tm, LANES), lambda i, e, f, n, hf: (i, 0)),
                  pl.BlockSpec((None, D, tf), lambda i, e, f, n, hf: (e, 0, f)),
                  pl.BlockSpec((None, D, tf), lambda i, e, f, n, hf: (e, 0, f)),
                  pl.BlockSpec((None, tf, D), lambda i, e, f, n, hf: (e, f, 0)),
                  pl.BlockSpec((tm, D), lambda i, e, f, n, hf: (i, 0)),
                  pl.BlockSpec((None, 1, D), lambda i, e, f, n, hf: ((1 + i // tiles_per_batch) * 6 + 5, 0, 0))],
        out_specs=pl.BlockSpec((tm, D), lambda i, e, f, n, hf: (i, 0)),
        scratch_shapes=[pltpu.VMEM((tm, D), BF16), pltpu.VMEM((tm, D), F32)])
    return pl.pallas_call(
        _moe_kernel, grid_spec=grid_spec, out_shape=jax.ShapeDtypeStruct((rows, D), F32), name="moe",
        compiler_params=_cp(("parallel", "arbitrary", "arbitrary"), _VMEM_LIMIT))(
            nfull, half, fin, slott, slot, rgate, w1, w3, w2, h, mod3)


def _regroup_cols(w):
    hy, ssd, da, gate = 3072, 2592, 3072, 3072
    parts = [w[..., :hy], w[..., hy + ssd + da:hy + ssd + da + gate], w[..., hy + ssd:hy + ssd + da],
             w[..., hy:hy + ssd]]
    pad = jnp.zeros(w.shape[:-1] + (_PCOLS - hy - ssd - da - gate,), w.dtype)
    return jnp.concatenate(parts + [pad], axis=-1)


def kernel(x, c, ctx, c_ctx, w_mod, b_mod, norm1_g, norm2_g, w_in, hy_conv_w, hy_conv_b, hy_w1, hy_b1, hy_w2, hy_b2,
           hy_w3, hy_b3, hy_w4, hy_freq, hy_bias, ssd_conv_w, ssd_conv_b, ssd_dt_bias, ssd_a_log, ssd_d, ssd_norm_g,
           da_q_norm, da_k_norm, da_lambda, da_subln_g, w_branch, w_out, ffn_w1, ffn_w3, ffn_w2, router_w, moe_w1,
           moe_w3, moe_w2):
    B, SEQ, _ = x.shape
    CTX = ctx.shape[1]
    assert CTX == ROW_TILE and SEQ % ROW_TILE == 0 and B % 2 == 0 and B + 1 <= 8
    T = CTX + SEQ
    tpb = T // ROW_TILE
    depth = w_in.shape[0]
    R = B * T

    h = jnp.concatenate([ctx, x], axis=1).reshape(R, D)
    cvec = jnp.zeros((8, D), F32).at[0].set(c_ctx).at[1:1 + B].set(c)
    cos_t, sin_t = _rope_tables(SEQ, CTX)
    tabs_l, tabs_c = _fft_tables(SEQ), _fft_tables(CTX)
    nb = D // LANES

    for i in range(depth):
        need_ctx = i < depth - 1
        mod = _mod(cvec, w_mod[i], b_mod[i][None])
        mod3 = mod.reshape(8 * 6, 1, D)
        xn = _normmod(h, norm1_g[i][None], mod3, T, CTX)
        p2 = _matmul(xn, _regroup_cols(w_in[i].astype(BF16)), T // 2, 2048, BF16)
        p3 = p2.reshape(B, T, _PCOLS)

        cw, cb = hy_conv_w[i], hy_conv_b[i][None]
        hy_args = (hy_w1[i], hy_b1[i], hy_w2[i], hy_b2[i], hy_w3[i], hy_b3[i], hy_w4[i], hy_freq[i])
        kh = _hy_kfft(_hy_filters(SEQ, *hy_args), hy_bias[i], tabs_l, SEQ)
        z1 = _hy_conv(p3, 0, p3, nb, cw, cb, kh, 0, tabs_l, SEQ, CTX, SEQ, 0, True)
        y_hy = _hy_conv(z1, 0, p3, 2 * nb, cw, cb, kh, 1, tabs_l, SEQ, CTX, SEQ, CTX, False)
        if need_ctx:
            khc = _hy_kfft(_hy_filters(CTX, *hy_args), hy_bias[i], tabs_c, CTX)
            z1c = _hy_conv(p3, 0, p3, nb, cw, cb, khc, 0, tabs_c, CTX, 0, CTX, 0, True)
            y_hy = _hy_conv(z1c, 0, p3, 2 * nb, cw, cb, khc, 1, tabs_c, CTX, 0, CTX, 0, False, alias_into=y_hy)

        xbc = _ssd_pre(p3, ssd_conv_w[i], ssd_conv_b[i][None], CTX)
        dtb = jnp.pad(ssd_dt_bias[i].reshape(1, 2 * SSD_HEADS), ((0, 0), (0, LANES - 2 * SSD_HEADS)))
        arow = jnp.pad(-jnp.exp(ssd_a_log[i].astype(F32)).reshape(1, 2 * SSD_HEADS),
                       ((0, 0), (0, LANES - 2 * SSD_HEADS)))
        y_rev = _ssd_scan(xbc, p3, dtb, arow, True, CTX)
        dskip = jnp.repeat(ssd_d[i].astype(F32), SSD_HEADDIM)[None]
        y_ssd = _ssd_scan(xbc, p3, dtb, arow, False, CTX, fin=(y_rev, dskip, ssd_norm_g[i][None]))

        lam_init = 0.8 - 0.6 * math.exp(-0.3 * (i + 1))
        gq = jnp.tile(da_q_norm[i], D // DA_HEAD_DIM)[None]
        gk = jnp.tile(da_k_norm[i], D // DA_HEAD_DIM)[None]
        qn, kn, vt = _da_pre(p2, gq, gk, cos_t, sin_t, tpb)
        score_bound = 8.0 * jnp.max(jnp.abs(da_q_norm[i])) * jnp.max(jnp.abs(da_k_norm[i]))
        safe = (score_bound <= ATTN_SAFE_SCORE).astype(jnp.int32).reshape(1)
        da_args = (safe, da_lambda[i], qn.reshape(B, T, D), kn.reshape(B, T, D), vt, da_subln_g[i][:, None], lam_init,
                   CTX)
        y_da = _da_attn(*da_args, False)
        y_dac = _da_attn(*da_args, True) if need_ctx else y_da

        wb, wo = w_branch[i].astype(BF16), w_out[i].astype(BF16)
        flat = lambda a: a.reshape(-1, D)
        j = i // 2
        if i % 2 == 0:
            h, fin = _merge(flat(y_hy), flat(y_ssd), flat(y_da), flat(y_dac), p2, h, wb, wo, mod3, norm2_g[i][None],
                            tpb, latent_only=not need_ctx)
            assert need_ctx, "dense FFN layers are expected to carry context rows"
            h = _ffn(fin, ffn_w1[j].astype(BF16), ffn_w3[j].astype(BF16), ffn_w2[j].astype(BF16), h, mod3, T, CTX)
        else:
            assert not need_ctx, "expert layers are expected to be latent only"
            rw = jnp.pad(router_w[j], ((0, 0), (0, LANES - N_EXPERTS)))
            h, fin, rgate = _merge(flat(y_hy), flat(y_ssd), flat(y_da), flat(y_dac), p2, h, wb, wo, mod3,
                                   norm2_g[i][None], tpb, latent_only=True, router_w=rw)
            h = _moe(fin, rgate, moe_w1[j].astype(BF16), moe_w3[j].astype(BF16), moe_w2[j].astype(BF16), h, mod3,
                     SEQ)
    return h.reshape(B, SEQ, D)
```

```python
import math
from functools import partial

import numpy as np
import jax
import jax.numpy as jnp
from jax import lax
from jax.experimental import pallas as pl
from jax.experimental.pallas import tpu as pltpu

F32, BF16 = jnp.float32, jnp.bfloat16
HIGHEST = lax.Precision.HIGHEST

D = 1024
EPS = 1e-6
GRID_W = 64
ROW_TILE = 256
LANES = 128
SSD_CHUNK = 128
SSD_HEADS = 16
SSD_HEADDIM = 64
DA_HEADS = 8
DA_HEAD_DIM = 64
N_EXPERTS = 8
HY_BANDS = 16
HY_EMB = 2 * HY_BANDS + 1
HY_FFN = 64

_PCOL_HY = 0
_PCOL_GATE = 3072
_PCOL_DA = 6144
_PCOL_Z = 9216
_PCOL_XBC = 10240
_PCOL_DT = 11776
_PCOLS = 12288
_VMEM_LIMIT = 56 * 1024 * 1024


def _cp(sem, vmem=None):
    return pltpu.CompilerParams(dimension_semantics=sem, vmem_limit_bytes=vmem)


def _dot(a, b, prec=None):
    return jnp.dot(a, b, preferred_element_type=F32, precision=prec)


def _silu(x):
    return x * jax.nn.sigmoid(x)


def _mod_kernel(c_ref, w_ref, b_ref, o_ref):
    o_ref[...] = _dot(_silu(c_ref[...]), w_ref[...], HIGHEST) + b_ref[...]


def _mod(cvec, w, b):
    return pl.pallas_call(
        _mod_kernel, grid=(6,),
        in_specs=[pl.BlockSpec((8, D), lambda j: (0, 0)), pl.BlockSpec((D, D), lambda j: (0, j)),
                  pl.BlockSpec((1, D), lambda j: (0, j))],
        out_specs=pl.BlockSpec((8, D), lambda j: (0, j)),
        out_shape=jax.ShapeDtypeStruct((8, 6 * D), F32), name="mod")(cvec, w, b)


def _modrow(t, tpb):
    return jnp.where(t % tpb == 0, 0, 1 + t // tpb)


def _mod_spec(k, tpb):
    return pl.BlockSpec((None, 1, D), lambda t: (_modrow(t, tpb) * 6 + k, 0, 0))


def _normmod_kernel(h_ref, g_ref, shc_ref, scc_ref, shl_ref, scl_ref, o_ref, *, ctx_rows, tiles_per_batch):
    x = h_ref[...]
    tm = x.shape[0]
    r = (pl.program_id(0) % tiles_per_batch) * tm + lax.broadcasted_iota(jnp.int32, (tm, 1), 0)
    is_ctx = r < ctx_rows
    y = x * lax.rsqrt(jnp.mean(x * x, axis=-1, keepdims=True) + EPS) * g_ref[...]
    o_ref[...] = (y * (1.0 + jnp.where(is_ctx, scc_ref[...], scl_ref[...]))
                  + jnp.where(is_ctx, shc_ref[...], shl_ref[...])).astype(o_ref.dtype)


def _normmod(h, g, mod3, T, CTX):
    rows = h.shape[0]
    tpb = 4
    tm = T // tpb
    ctx = lambda k: pl.BlockSpec((None, 1, D), lambda t: (k, 0, 0))
    lat = lambda k: pl.BlockSpec((None, 1, D), lambda t: ((1 + t // tpb) * 6 + k, 0, 0))
    return pl.pallas_call(
        partial(_normmod_kernel, ctx_rows=CTX, tiles_per_batch=tpb), grid=(rows // tm,),
        in_specs=[pl.BlockSpec((tm, D), lambda t: (t, 0)), pl.BlockSpec((1, D), lambda t: (0, 0)),
                  ctx(0), ctx(1), lat(0), lat(1)],
        out_specs=pl.BlockSpec((tm, D), lambda t: (t, 0)),
        out_shape=jax.ShapeDtypeStruct((rows, D), BF16), name="normmod",
        compiler_params=_cp(("parallel",)))(h, g, mod3, mod3, mod3, mod3)


def _mm_kernel(a_ref, b_ref, o_ref):
    o_ref[...] = _dot(a_ref[...], b_ref[...]).astype(o_ref.dtype)


def _matmul(a, b, tm, tn, out_dtype):
    m, k = a.shape
    n = b.shape[1]
    return pl.pallas_call(
        _mm_kernel, grid=(n // tn, m // tm),
        in_specs=[pl.BlockSpec((tm, k), lambda j, i: (i, 0)), pl.BlockSpec((k, tn), lambda j, i: (0, j))],
        out_specs=pl.BlockSpec((tm, tn), lambda j, i: (i, j)),
        out_shape=jax.ShapeDtypeStruct((m, n), out_dtype), name="inproj",
        compiler_params=_cp(("parallel", "parallel"), _VMEM_LIMIT))(a, b)


def _hyfilt_kernel(f_ref, w1, b1, w2, b2, w3, b3, w4, fr_ref, dl_ref, o_ref, h_ref, win_ref, *, shift):
    @pl.when(pl.program_id(1) == 0)
    def _():
        f = f_ref[...]
        fr = fr_ref[...]
        h = jnp.sin(fr[0:1] * (_dot(f, w1[...], HIGHEST) + b1[...]))
        h = jnp.sin(fr[1:2] * (_dot(h, w2[...], HIGHEST) + b2[...]))
        h = jnp.sin(fr[2:3] * (_dot(h, w3[...], HIGHEST) + b3[...]))
        h_ref[...] = h.astype(BF16)
        win_ref[...] = jnp.exp(-f[:, 0:1] * dl_ref[...]) + shift

    o_ref[...] = _dot(h_ref[...], w4[...]) * win_ref[...]


def _hy_filters(L, w1, b1, w2, b2, w3, b3, w4, freq):
    t = jnp.linspace(0.0, 1.0, L, dtype=F32)[:, None]
    w = 2.0 * math.pi * jnp.arange(L, dtype=F32)[:, None] / L
    f = jnp.linspace(1e-4, HY_BANDS - 1, HY_BANDS, dtype=F32)[None, :]
    feats = jnp.concatenate([t, jnp.cos(f * w), -jnp.sin(f * w)], axis=-1)
    feats = jnp.pad(feats, ((0, 0), (0, HY_FFN - HY_EMB)))
    w1p = jnp.pad(w1, ((0, HY_FFN - HY_EMB), (0, 0)))
    max_decay = math.log(1e-2) / 0.3
    min_decay = math.log(1e-2) / 1.5
    deltas = jnp.abs(jnp.linspace(min_decay, max_decay, D, dtype=F32))[None, :]
    tl = min(L, 512)
    full = lambda r, c: pl.BlockSpec((r, c), lambda i, j: (0, 0))
    return pl.pallas_call(
        partial(_hyfilt_kernel, shift=0.05), grid=(L // tl, 4),
        in_specs=[pl.BlockSpec((tl, HY_FFN), lambda i, j: (i, 0)), full(HY_FFN, HY_FFN), full(1, HY_FFN),
                  full(HY_FFN, HY_FFN), full(1, HY_FFN), full(HY_FFN, HY_FFN), full(1, HY_FFN),
                  pl.BlockSpec((HY_FFN, D), lambda i, j: (0, j)), full(3, HY_FFN), full(1, D)],
        out_specs=pl.BlockSpec((tl, D), lambda i, j: (i, j)),
        out_shape=jax.ShapeDtypeStruct((L, 4 * D), F32), name="hyfilt",
        scratch_shapes=[pltpu.VMEM((tl, HY_FFN), BF16), pltpu.VMEM((tl, D), F32)],
        compiler_params=_cp(("parallel", "arbitrary")))(
            feats, w1p, b1[None], w2, b2[None], w3, b3[None], w4.astype(BF16), freq, deltas)


def _fft_split(L):
    n = 2 * L
    n2 = 128 if L >= 2048 else n // 32
    return n // n2, n2


def _fft_tables(L):
    n1_, n2_ = _fft_split(L)
    n, h1 = 2 * L, n1_ // 2
    n2 = np.arange(n2_)[:, None, None]
    k1 = np.arange(n1_)[None, :, None]
    n1 = np.arange(h1)[None, None, :]
    g = np.exp(-2j * np.pi * (n2 * k1 / n + n1 * k1 / n1_))
    ma = np.concatenate([np.concatenate([g.real, -g.imag], 2), np.concatenate([g.imag, g.real], 2)], 1)
    kk = np.arange(n2_)
    f = np.exp(-2j * np.pi * np.outer(kk, kk) / n2_)
    mc = np.block([[f.real, -f.imag], [f.imag, f.real]])
    mci = np.block([[f.real, f.imag], [-f.imag, f.real]])
    gi = np.exp(2j * np.pi * (n2 * np.arange(n1_)[None, None, :] / n
                              + np.arange(h1)[None, :, None] * np.arange(n1_)[None, None, :] / n1_)) / n
    mai = np.concatenate([np.concatenate([gi.real, -gi.imag], 2), np.concatenate([gi.imag, gi.real], 2)], 1)
    cast = lambda a: jnp.asarray(a, F32).astype(BF16)
    return cast(ma), cast(ma[:, :, :h1]), cast(mc), cast(mci), cast(mai)


def _kfft_kernel(kf_ref, kb_ref, bias_ref, mar_ref, mc_ref, o_ref, af_ref, ab_ref, *, N1, N2):
    H1, PA = N1 // 2, 2 * N1 + 8

    def stage_a(n2, c):
        for src, dst in ((kf_ref, af_ref), (kb_ref, ab_ref)):
            slab = src[pl.ds(n2, H1, stride=N2), :].astype(BF16)
            dst[pl.ds(pl.multiple_of(n2 * PA, 8), 2 * N1), :] = _dot(mar_ref[n2], slab)
        return c

    lax.fori_loop(0, N2, stage_a, 0, unroll=8)
    bias = bias_ref[...]

    def stage_c(k1, c):
        def rows(ref):
            return jnp.concatenate([ref[pl.ds(k1, N2, stride=PA), :], ref[pl.ds(N1 + k1, N2, stride=PA), :]], 0)
        x = _dot(mc_ref[...], jnp.concatenate([rows(af_ref), rows(ab_ref)], 1).astype(BF16))
        xf, xb = x[:, :LANES], x[:, LANES:]
        o_ref[k1] = jnp.concatenate([xf[:N2] + xb[:N2] + bias, xf[N2:] - xb[N2:]], 0).astype(o_ref.dtype)
        return c

    lax.fori_loop(0, N1, stage_c, 0, unroll=16)


def _hy_kfft(k4, bias, tabs, L):
    N1, N2 = _fft_split(L)
    _, mar, mc, _, _ = tabs
    nb = D // LANES
    return pl.pallas_call(
        partial(_kfft_kernel, N1=N1, N2=N2), grid=(2, nb),
        in_specs=[pl.BlockSpec((L, LANES), lambda o, j: (0, o * 2 * nb + j)),
                  pl.BlockSpec((L, LANES), lambda o, j: (0, o * 2 * nb + nb + j)),
                  pl.BlockSpec((None, 1, LANES), lambda o, j: (o, 0, j)),
                  pl.BlockSpec(mar.shape, lambda o, j: (0, 0, 0)), pl.BlockSpec(mc.shape, lambda o, j: (0, 0))],
        out_specs=pl.BlockSpec((None, N1, 2 * N2, LANES), lambda o, j: (o, 0, 0, j)),
        out_shape=jax.ShapeDtypeStruct((2, N1, 2 * N2, D), BF16),
        scratch_shapes=[pltpu.VMEM((N2 * (2 * N1 + 8), LANES), F32)] * 2, name="hykfft",
        compiler_params=_cp(("parallel", "parallel"), _VMEM_LIMIT))(k4, k4, bias[:, None, :], mar, mc)


def _hyconv_kernel(sig_ref, gate_ref, cws_ref, cbs_ref, cwg_ref, cbg_ref, kh_ref, ma_ref, mc_ref, mci_ref,
                   mai_ref, *rest, L, N1, N2, in_row0, out_row0, sig_conv, aliased):
    if aliased:
        rest = rest[1:]
    o_ref, sig0, sig1, gat0, gat1, a_ref = rest
    H1, PA = N1 // 2, 2 * N1 + 8
    NP, PADR = N2 + 8, 16
    sigs, gats = (sig0, sig1), (gat0, gat1)

    def stage(ref, nat, bb, r0):
        nat[0:PADR, :] = jnp.zeros((PADR, LANES), F32)
        nat[PADR + H1 * NP:PADR + H1 * NP + 8, :] = jnp.zeros((8, LANES), F32)
        for n1 in range(H1):
            nat[PADR + n1 * NP:PADR + n1 * NP + N2, :] = ref[bb, r0 + n1 * N2:r0 + (n1 + 1) * N2, :].astype(F32)

    def slab(nat, n2, w, b):
        u = nat[pl.ds(PADR + n2, H1, stride=NP), :]
        if w is None:
            return u
        prev = jnp.where(n2 == 0, PADR - NP + N2 - 1, PADR + n2 - 1)
        nxt = jnp.where(n2 == N2 - 1, PADR + NP, PADR + n2 + 1)
        return (w[0:1] * nat[pl.ds(prev, H1, stride=NP), :] + w[1:2] * u
                + w[2:3] * nat[pl.ds(nxt, H1, stride=NP), :] + b)

    for bb in range(2):
        stage(sig_ref, sigs[bb], bb, in_row0 if sig_conv else 0)
        stage(gate_ref, gats[bb], bb, in_row0)
    ws, bs = (cws_ref[...], cbs_ref[...]) if sig_conv else (None, None)
    wg, bg = cwg_ref[...], cbg_ref[...]

    def stage_a(n2, c):
        st = jnp.concatenate([slab(sigs[0], n2, ws, bs), slab(sigs[1], n2, ws, bs)], 0).astype(BF16)
        a_ref[pl.ds(pl.multiple_of(n2 * PA, 8), 2 * N1), :] = _dot(ma_ref[n2], st)
        return c

    lax.fori_loop(0, N2, stage_a, 0, unroll=16)

    def stage_c(i, c):
        k1 = 2 * i

        def rows(k):
            return jnp.concatenate([a_ref[pl.ds(k, N2, stride=PA), :], a_ref[pl.ds(N1 + k, N2, stride=PA), :]], 0)

        x = _dot(mc_ref[...], jnp.concatenate([rows(k1), rows(k1 + 1)], 1).astype(BF16))
        kh = jnp.concatenate([kh_ref[k1], kh_ref[k1 + 1]], 1).astype(F32)
        xr, xi, kr, ki = x[:N2], x[N2:], kh[:N2], kh[N2:]
        y = jnp.concatenate([xr * kr - xi * ki, xr * ki + xi * kr], 0)
        cc = _dot(mci_ref[...], y.astype(BF16))
        for j in range(2):
            a_ref[pl.ds(k1 + j, N2, stride=PA), :] = cc[:N2, j * LANES:(j + 1) * LANES]
            a_ref[pl.ds(N1 + k1 + j, N2, stride=PA), :] = cc[N2:, j * LANES:(j + 1) * LANES]
        return c

    lax.fori_loop(0, N1 // 2, stage_c, 0, unroll=16)

    def stage_ai(n2, c):
        st = a_ref[pl.ds(pl.multiple_of(n2 * PA, 8), 2 * N1), :]
        g = jnp.concatenate([slab(gats[0], n2, wg, bg), slab(gats[1], n2, wg, bg)], 0)
        r = _dot(mai_ref[n2], st.astype(BF16)) * g
        sig0[pl.ds(PADR + n2, H1, stride=NP), :] = r[:H1]
        sig1[pl.ds(PADR + n2, H1, stride=NP), :] = r[H1:]
        return c

    lax.fori_loop(0, N2, stage_ai, 0, unroll=16)

    for bb in range(2):
        if out_row0:
            o_ref[bb, 0:out_row0, :] = jnp.zeros((out_row0, LANES), o_ref.dtype)
        for n1 in range(H1):
            o_ref[bb, out_row0 + n1 * N2:out_row0 + (n1 + 1) * N2, :] = (
                sigs[bb][PADR + n1 * NP:PADR + n1 * NP + N2, :].astype(o_ref.dtype))


def _hy_conv(sig, sig_col, gate, gate_col, cw, cb, kh, order, tabs, L, in_row0, out_rows, out_row0, sig_conv,
             alias_into=None):
    N1, N2 = _fft_split(L)
    ma, _, mc, mci, mai = tabs
    H1 = N1 // 2
    B = gate.shape[0]
    nb = D // LANES
    scol = sig_col if sig_conv else 0
    cspec = lambda r, c0: pl.BlockSpec((r, LANES), lambda j, q: (0, c0 + j))
    const = lambda a: pl.BlockSpec(a.shape, lambda j, q: (0,) * a.ndim)
    in_specs = [pl.BlockSpec((2, sig.shape[1], LANES), lambda j, q: (q, 0, sig_col + j)),
                pl.BlockSpec((2, gate.shape[1], LANES), lambda j, q: (q, 0, gate_col + j)),
                cspec(3, scol), cspec(1, scol), cspec(3, gate_col), cspec(1, gate_col),
                pl.BlockSpec((None, N1, 2 * N2, LANES), lambda j, q: (order, 0, 0, j)),
                const(ma), const(mc), const(mci), const(mai)]
    args = [sig, gate, cw, cb, cw, cb, kh, ma, mc, mci, mai]
    aliases = {}
    if alias_into is not None:
        in_specs.append(pl.BlockSpec(memory_space=pl.ANY))
        args.append(alias_into)
        aliases = {len(args) - 1: 0}
        out_total = alias_into.shape[1]
    else:
        out_total = out_row0 + out_rows
    return pl.pallas_call(
        partial(_hyconv_kernel, L=L, N1=N1, N2=N2, in_row0=in_row0, out_row0=out_row0, sig_conv=sig_conv,
                aliased=alias_into is not None),
        grid=(nb, B // 2), in_specs=in_specs,
        out_specs=pl.BlockSpec((2, out_row0 + out_rows, LANES), lambda j, q: (q, 0, j)),
        out_shape=jax.ShapeDtypeStruct((B, out_total, D), BF16),
        scratch_shapes=[pltpu.VMEM((H1 * (N2 + 8) + 24, LANES), F32)] * 4
        + [pltpu.VMEM((N2 * (2 * N1 + 8), LANES), F32)],
        input_output_aliases=aliases, name="hyconv",
        compiler_params=_cp(("parallel", "parallel"), _VMEM_LIMIT))(*args)


def _ssdpre_kernel(x_ref, w_ref, b_ref, o_ref, nat, *, T, CTX):
    W = x_ref.shape[1]
    nat[0:8, :] = jnp.zeros((8, W), F32)
    nat[8 + T:16 + T, :] = jnp.zeros((8, W), F32)
    for i in range(T // ROW_TILE):
        nat[8 + i * ROW_TILE:8 + (i + 1) * ROW_TILE, :] = x_ref[i * ROW_TILE:(i + 1) * ROW_TILE, :].astype(F32)
    w = w_ref[...]
    b = b_ref[...]
    row = lax.broadcasted_iota(jnp.int32, (ROW_TILE, 1), 0)
    for i in range(T // ROW_TILE):
        r0 = 8 + i * ROW_TILE
        xm = nat[r0 - 1:r0 - 1 + ROW_TILE, :]
        xp = nat[r0 + 1:r0 + 1 + ROW_TILE, :]
        if i * ROW_TILE == CTX:
            xm = jnp.where(row == 0, 0.0, xm)
        if (i + 1) * ROW_TILE == CTX:
            xp = jnp.where(row == ROW_TILE - 1, 0.0, xp)
        u = w[0:1] * xm + w[1:2] * nat[r0:r0 + ROW_TILE, :] + w[2:3] * xp + b
        o_ref[i * ROW_TILE:(i + 1) * ROW_TILE, :] = _silu(u).astype(o_ref.dtype)


def _ssd_pre(p3, cw, cb, CTX):
    B, T, _ = p3.shape
    W = 256
    nblk = cw.shape[1] // W
    return pl.pallas_call(
        partial(_ssdpre_kernel, T=T, CTX=CTX), grid=(B, nblk),
        in_specs=[pl.BlockSpec((None, T, W), lambda b, j: (b, 0, _PCOL_XBC // W + j)),
                  pl.BlockSpec((3, W), lambda b, j: (0, j)), pl.BlockSpec((1, W), lambda b, j: (0, j))],
        out_specs=pl.BlockSpec((None, T, W), lambda b, j: (b, 0, j)),
        out_shape=jax.ShapeDtypeStruct((B, T, cw.shape[1]), BF16),
        scratch_shapes=[pltpu.VMEM((T + 16, W), F32)], name="ssdpre",
        compiler_params=_cp(("parallel", "parallel")))(p3, cw, cb)


SSD_CHUNKS_PER_STEP = 2


def _ssdscan_kernel(xbc_ref, dt_ref, dtb_ref, a_ref, tri_ref, *rest, rev):
    if rev:
        o_ref, s_ref = rest
    else:
        z_ref, yr_ref, dsk_ref, ng_ref, o_ref, s_ref, ybuf = rest
    Q = SSD_CHUNK

    @pl.when(pl.program_id(1) == 0)
    def _():
        s_ref[...] = jnp.zeros(s_ref.shape, F32)

    tri = tri_ref[...]
    mask = tri > 0.5
    last = 0 if rev else Q - 1
    col0 = SSD_HEADS if rev else 0
    lane_lo = lax.broadcasted_iota(jnp.int32, (1, LANES), 1) < SSD_HEADDIM

    def one_chunk(r0):
        rows = slice(r0, r0 + Q)
        x = xbc_ref[rows, :]
        xs, Bm, Cm = x[:, :D], x[:, D:D + 2 * LANES], x[:, D + 2 * LANES:]
        dtr = dt_ref[rows, :].astype(F32) + dtb_ref[...]
        dt = jnp.maximum(dtr, 0.0) + jnp.log(1.0 + jnp.exp(-jnp.abs(dtr)))
        cs = _dot(tri, dt * a_ref[...], HIGHEST)
        csT = cs.T
        dtT = dt.T
        for g in range(2):
            Bg = Bm[:, g * LANES:(g + 1) * LANES]
            Cg = Cm[:, g * LANES:(g + 1) * LANES]
            CB = lax.dot_general(Cg, Bg, (((1,), (1,)), ((), ())), preferred_element_type=F32)
            BgT = Bg.astype(F32).T
            Cgf = Cg.astype(F32)
            for pr in range(4):
                hp = g * 4 + pr
                xs_pair = xs[:, hp * LANES:(hp + 1) * LANES]
                Sp = s_ref[hp]
                rhs = jnp.concatenate([xs_pair, Sp.astype(BF16)], 0)
                ys, sts = [], []
                for hh in range(2):
                    col = col0 + 2 * hp + hh
                    csl = jnp.broadcast_to(cs[:, col:col + 1], (Q, Q))
                    csr = csT[col:col + 1, :]
                    dtr_row = dtT[col:col + 1, :]
                    Lm = jnp.exp(jnp.where(mask, csl - csr, -1e30))
                    lhs = jnp.concatenate([CB * Lm * dtr_row, Cgf * jnp.exp(csl)], 1).astype(BF16)
                    ys.append(_dot(lhs, rhs))
                    tot = csT[col:col + 1, last:last + 1]
                    w_row = jnp.exp(tot - csr) * dtr_row
                    st = _dot((BgT * w_row).astype(BF16), xs_pair)
                    sts.append(jnp.exp(tot) * Sp + st)
                y_pair = jnp.where(lane_lo, ys[0], ys[1])
                s_ref[hp] = jnp.where(lane_lo, sts[0], sts[1])
                sl = slice(hp * LANES, (hp + 1) * LANES)
                if rev:
                    o_ref[rows, sl] = y_pair.astype(o_ref.dtype)
                else:
                    ybuf[:, sl] = y_pair + yr_ref[rows, sl].astype(F32) + xs_pair.astype(F32) * dsk_ref[:, sl]
        if not rev:
            y = ybuf[...] * _silu(z_ref[rows, :].astype(F32))
            half = D // 2
            parts = []
            for g in range(2):
                yg = y[:, g * half:(g + 1) * half]
                parts.append(yg * lax.rsqrt(jnp.mean(yg * yg, axis=-1, keepdims=True) + EPS))
            o_ref[rows, :] = (jnp.concatenate(parts, 1) * ng_ref[...]).astype(o_ref.dtype)

    subs = range(SSD_CHUNKS_PER_STEP)
    for sub in (reversed(subs) if rev else subs):
        one_chunk(sub * Q)


def _ssd_scan(xbc, p3, dtb, arow, rev, CTX, fin=None):
    B, T, _ = xbc.shape
    Q = SSD_CHUNK
    R = SSD_CHUNKS_PER_STEP * Q
    assert CTX % R == 0 and T % R == 0
    nst, nc = T // R, CTX // R
    if rev:
        step = lambda s: jnp.where(s < nc, nc - 1 - s, nst - 1 + nc - s)
        tri = jnp.asarray(np.triu(np.ones((Q, Q), np.float32)))
    else:
        step = lambda s: s
        tri = jnp.asarray(np.tril(np.ones((Q, Q), np.float32)))
    row = lambda w: pl.BlockSpec((1, w), lambda b, s: (0, 0))
    in_specs = [pl.BlockSpec((None, R, xbc.shape[2]), lambda b, s: (b, step(s), 0)),
                pl.BlockSpec((None, R, LANES), lambda b, s: (b, step(s), _PCOL_DT // LANES)),
                row(LANES), row(LANES), pl.BlockSpec((Q, Q), lambda b, s: (0, 0))]
    args = [xbc, p3, dtb, arow, tri]
    scratch = [pltpu.VMEM((SSD_HEADS // 2, LANES, LANES), F32)]
    if not rev:
        yrev, dskip, ng = fin
        in_specs += [pl.BlockSpec((None, R, D), lambda b, s: (b, s, _PCOL_Z // D)),
                     pl.BlockSpec((None, R, D), lambda b, s: (b, s, 0)), row(D), row(D)]
        args += [p3, yrev, dskip, ng]
        scratch.append(pltpu.VMEM((Q, D), F32))
    return pl.pallas_call(
        partial(_ssdscan_kernel, rev=rev), grid=(B, nst), in_specs=in_specs,
        out_specs=pl.BlockSpec((None, R, D), lambda b, s: (b, step(s), 0)),
        out_shape=jax.ShapeDtypeStruct((B, T, D), BF16), scratch_shapes=scratch,
        name="ssdrev" if rev else "ssdfwd",
        compiler_params=_cp(("parallel", "arbitrary")))(*args)


def _dapre_kernel(q_ref, k_ref, v_ref, gq_ref, gk_ref, cos_ref, sin_ref, ones_ref, qo_ref, ko_ref, vt_ref):
    ones = ones_ref[...]
    cosf = jnp.concatenate([cos_ref[...]] * (D // LANES), 1)
    sinf = jnp.concatenate([sin_ref[...]] * (D // LANES), 1)
    lane = lax.broadcasted_iota(jnp.int32, (1, D), 1)
    first_half = (lane % 32) < 16

    def norm_rope(x_ref, g_ref, scale):
        x = x_ref[...].astype(F32)
        sq = x * x
        hi = sq.astype(BF16)
        lo = (sq - hi.astype(F32)).astype(BF16)
        parts = []
        for blk in range(D // 256):
            sl = slice(blk * 256, (blk + 1) * 256)
            parts.append(_dot(hi[:, sl], ones) + _dot(lo[:, sl], ones))
        ss = jnp.concatenate(parts, 1)
        y = x * lax.rsqrt(ss * (1.0 / DA_HEAD_DIM) + EPS) * g_ref[...]
        partner = jnp.where(first_half, pltpu.roll(y, D - 16, 1), pltpu.roll(y, 16, 1))
        return ((y * cosf + partner * sinf) * scale).astype(BF16)

    qo_ref[...] = norm_rope(q_ref, gq_ref, LOG2E * DA_HEAD_DIM ** -0.5)
    ko_ref[...] = norm_rope(k_ref, gk_ref, 1.0)
    vt_ref[...] = v_ref[...].astype(F32).T.astype(BF16)


def _rope_tables(L, CTX):
    rows = L // GRID_W
    r = np.repeat(np.arange(rows), GRID_W)
    c = np.tile(np.arange(GRID_W), rows)
    nf = DA_HEAD_DIM // 4
    inv = jnp.asarray(10000.0, F32) ** (-jnp.arange(nf, dtype=F32) / nf)
    ang = jnp.stack([jnp.asarray(r), jnp.asarray(c)], -1).astype(F32)[:, :, None] * inv
    cos, sin = jnp.cos(ang), jnp.sin(ang)
    cos64 = jnp.stack([cos, cos], 2).reshape(L, DA_HEAD_DIM)
    sin64 = jnp.stack([-sin, sin], 2).reshape(L, DA_HEAD_DIM)
    cos_t = jnp.concatenate([jnp.ones((CTX, DA_HEAD_DIM), F32), cos64], 0)
    sin_t = jnp.concatenate([jnp.zeros((CTX, DA_HEAD_DIM), F32), sin64], 0)
    return jnp.tile(cos_t, (1, 2)), jnp.tile(sin_t, (1, 2))


def _da_pre(p2, gq, gk, cos_t, sin_t, tpb):
    rows = p2.shape[0]
    ones = jnp.asarray(np.kron(np.eye(256 // DA_HEAD_DIM), np.ones((DA_HEAD_DIM, DA_HEAD_DIM))), BF16)
    tab = pl.BlockSpec((ROW_TILE, LANES), lambda t: (t % tpb, 0))
    row = pl.BlockSpec((1, D), lambda t: (0, 0))
    out = pl.BlockSpec((ROW_TILE, D), lambda t: (t, 0))
    seg = lambda k: pl.BlockSpec((ROW_TILE, D), lambda t: (t, _PCOL_DA // D + k))
    return pl.pallas_call(
        _dapre_kernel, grid=(rows // ROW_TILE,),
        in_specs=[seg(0), seg(1), seg(2), row, row, tab, tab, pl.BlockSpec((256, 256), lambda t: (0, 0))],
        out_specs=[out, out, pl.BlockSpec((None, D, ROW_TILE), lambda t: (t // tpb, 0, t % tpb))],
        out_shape=[jax.ShapeDtypeStruct((rows, D), BF16)] * 2
        + [jax.ShapeDtypeStruct((rows // ROW_TILE // tpb, D, tpb * ROW_TILE), BF16)], name="dapre",
        compiler_params=_cp(("parallel",)))(p2, p2, p2, gq, gk, cos_t, sin_t, ones)


ATTN_KEY_BLOCK = 1024
ATTN_SAFE_SCORE = 30.0
LOG2E = 1.4426950408889634


def _attn_kernel(safe_ref, lam_ref, *refs, lam_init):
    lv = lam_ref[...]
    lam = (jnp.exp(jnp.sum(lv[0:1] * lv[1:2], keepdims=True)) - jnp.exp(jnp.sum(lv[2:3] * lv[3:4], keepdims=True))
           + lam_init)
    *q_refs, k_ref, vt_ref, g_ref, o_ref = refs
    q = jnp.concatenate([r[...] for r in q_refs], 0)
    tq = q.shape[0]
    lane_lo = lax.broadcasted_iota(jnp.int32, (1, LANES), 1) < DA_HEAD_DIM
    qcat = jnp.concatenate([jnp.where(lane_lo, q, jnp.zeros_like(q)), jnp.where(lane_lo, jnp.zeros_like(q), q)], 0)

    nt = (((1,), (1,)), ((), ()))

    def finish(ot):
        o = ot[:, :tq] - lam * ot[:, tq:]
        y = o * lax.rsqrt(jnp.mean(o * o, axis=0, keepdims=True) + EPS) * g_ref[...] * (1.0 - lam_init)
        o_ref[...] = y.T.astype(o_ref.dtype)

    def attend(shifted):
        nkeys = k_ref.shape[0]
        blocks = [(k0, min(k0 + ATTN_KEY_BLOCK, nkeys)) for k0 in range(0, nkeys, ATTN_KEY_BLOCK)]
        scores = lambda b: lax.dot_general(k_ref[b[0]:b[1], :], qcat, nt, preferred_element_type=F32)
        if shifted:
            m = jnp.full((1, 2 * tq), -jnp.inf, F32)
            for blk in blocks:
                m = jnp.maximum(m, jnp.max(scores(blk), axis=0, keepdims=True))
        l = jnp.zeros((1, 2 * tq), F32)
        acc = jnp.zeros((LANES, 2 * tq), F32)
        st = scores(blocks[0])
        for j, (k0, k1) in enumerate(blocks):
            st_next = scores(blocks[j + 1]) if j + 1 < len(blocks) else None
            e = jnp.exp2(st - m) if shifted else jnp.exp2(st)
            l = l + jnp.sum(e, axis=0, keepdims=True)
            acc = acc + _dot(vt_ref[:, k0:k1], e.astype(BF16))
            st = st_next
        finish(acc / l)

    @pl.when(safe_ref[0] == 1)
    def _():
        attend(False)

    @pl.when(safe_ref[0] != 1)
    def _():
        attend(True)


def _da_attn(safe, lam4, qn, kn, vt, subg_col, lam_init, CTX, context):
    B, T, _ = qn.shape
    tq = ROW_TILE
    if context:
        nq, nkeys, out_rows = 1, CTX, CTX
        q_specs = [pl.BlockSpec((None, tq, LANES), lambda b, h, i, s: (b, 0, h))]
    else:
        per_step = 4 if (T - CTX) % (4 * tq) == 0 else 2
        nq, nkeys, out_rows = (T - CTX) // (per_step * tq), T, T - CTX
        q0 = CTX // tq
        q_specs = [pl.BlockSpec((None, tq, LANES), lambda b, h, i, s, j=j: (b, q0 + per_step * i + j, h))
                   for j in range(per_step)]
    grid_spec = pltpu.PrefetchScalarGridSpec(
        num_scalar_prefetch=1, grid=(B, DA_HEADS, nq),
        in_specs=[pl.BlockSpec((4, DA_HEAD_DIM), lambda b, h, i, s: (0, 0))] + q_specs + [
            pl.BlockSpec((None, nkeys, LANES), lambda b, h, i, s: (b, 0, h)),
            pl.BlockSpec((None, LANES, nkeys), lambda b, h, i, s: (b, h, 0)),
            pl.BlockSpec((LANES, 1), lambda b, h, i, s: (0, 0))],
        out_specs=pl.BlockSpec((None, len(q_specs) * tq, LANES), lambda b, h, i, s: (b, i, h)))
    return pl.pallas_call(
        partial(_attn_kernel, lam_init=lam_init), grid_spec=grid_spec,
        out_shape=jax.ShapeDtypeStruct((B, out_rows, D), BF16), name="daattn",
        compiler_params=_cp(("parallel", "parallel", "parallel"), _VMEM_LIMIT))(
            safe, lam4, *([qn] * len(q_specs)), kn, vt, subg_col)


def _merge_kernel(yh_ref, ys_ref, yd_ref, ydc_ref, g0_ref, g1_ref, g2_ref, h_ref, wb_ref, wo_ref, gate_ref, n2_ref,
                  sh_ref, sc_ref, *rest, route, ctx_every):
    yd = yd_ref[...]
    if ctx_every:
        yd = jnp.where(pl.program_id(0) % ctx_every == 0, ydc_ref[...], yd)
    acc = None
    for i, (y, gl_ref) in enumerate(((yh_ref[...], g0_ref), (ys_ref[...], g1_ref), (yd, g2_ref))):
        term = jax.nn.sigmoid(gl_ref[...].astype(F32)) * _dot(y, wb_ref[i])
        acc = term if acc is None else acc + term
    hn = h_ref[...] + gate_ref[...] * _dot(acc.astype(BF16), wo_ref[...])
    xn = hn * lax.rsqrt(jnp.mean(hn * hn, axis=-1, keepdims=True) + EPS) * n2_ref[...]
    fin = xn * (1.0 + sc_ref[...]) + sh_ref[...]
    if not route:
        hout_ref, fin_ref = rest
    else:
        rw_ref, hout_ref, fin_ref, rg_ref = rest
        rw = rw_ref[...]
        fin_hi, rw_hi = fin.astype(BF16), rw.astype(BF16)
        fin_lo, rw_lo = (fin - fin_hi.astype(F32)).astype(BF16), (rw - rw_hi.astype(F32)).astype(BF16)
        logits = _dot(fin_hi, rw_hi) + _dot(fin_lo, rw_hi) + _dot(fin_hi, rw_lo)
        lane = lax.broadcasted_iota(jnp.int32, logits.shape, 1)
        logits = jnp.where(lane < N_EXPERTS, logits, -jnp.inf)
        m1 = jnp.max(logits, axis=1, keepdims=True)
        i1 = jnp.min(jnp.where(logits == m1, lane, LANES), axis=1, keepdims=True)
        rest_l = jnp.where(lane == i1, -jnp.inf, logits)
        m2 = jnp.max(rest_l, axis=1, keepdims=True)
        i2 = jnp.min(jnp.where(rest_l == m2, lane, LANES), axis=1, keepdims=True)
        e2 = jnp.exp(m2 - m1)
        rg_ref[...] = jnp.where(lane == i1, 1.0 / (1.0 + e2), 0.0) + jnp.where(lane == i2, e2 / (1.0 + e2), 0.0)
    hout_ref[...] = hn
    fin_ref[...] = fin.astype(fin_ref.dtype)


def _merge(yh, ys, yd, ydc, p2, h, wb, wo, mod3, n2g, tpb, latent_only, router_w=None):
    rows = h.shape[0]
    if latent_only:
        nt = rows // ROW_TILE // tpb * (tpb - 1)
        src = lambda t: t // (tpb - 1) * tpb + 1 + t % (tpb - 1)
    else:
        nt = rows // ROW_TILE
        src = lambda t: t
    tile = lambda cb: pl.BlockSpec((ROW_TILE, D), lambda t: (src(t), cb))
    mspec = lambda k: pl.BlockSpec((None, 1, D), lambda t: (_modrow(src(t), tpb) * 6 + k, 0, 0))
    if latent_only:
        yd_specs = [pl.BlockSpec((ROW_TILE, D), lambda t: (t, 0)), pl.BlockSpec((ROW_TILE, D), lambda t: (0, 0))]
    else:
        yd_specs = [pl.BlockSpec((ROW_TILE, D), lambda t: (t // tpb * (tpb - 1) + jnp.maximum(t % tpb - 1, 0), 0)),
                    pl.BlockSpec((ROW_TILE, D), lambda t: (t // tpb, 0))]
    in_specs = [tile(0), tile(0)] + yd_specs + [
                tile(_PCOL_GATE // D), tile(_PCOL_GATE // D + 1),
                tile(_PCOL_GATE // D + 2), tile(0), pl.BlockSpec((3, D, D), lambda t: (0, 0, 0)),
                pl.BlockSpec((D, D), lambda t: (0, 0)), mspec(2), pl.BlockSpec((1, D), lambda t: (0, 0)),
                mspec(3), mspec(4)]
    args = [yh, ys, yd, ydc, p2, p2, p2, h, wb, wo, mod3, n2g, mod3, mod3]
    otile = pl.BlockSpec((ROW_TILE, D), lambda t: (t, 0))
    out_specs = [otile, otile]
    out_shape = [jax.ShapeDtypeStruct((nt * ROW_TILE, D), F32), jax.ShapeDtypeStruct((nt * ROW_TILE, D), BF16)]
    if router_w is not None:
        in_specs.append(pl.BlockSpec((D, LANES), lambda t: (0, 0)))
        args.append(router_w)
        out_specs.append(pl.BlockSpec((ROW_TILE, LANES), lambda t: (t, 0)))
        out_shape.append(jax.ShapeDtypeStruct((nt * ROW_TILE, LANES), F32))
    return pl.pallas_call(
        partial(_merge_kernel, route=router_w is not None, ctx_every=0 if latent_only else tpb), grid=(nt,), in_specs=in_specs, out_specs=out_specs,
        out_shape=out_shape, name="merge", compiler_params=_cp(("parallel",), _VMEM_LIMIT))(*args)


def _ffn_kernel(x_ref, w1_ref, w3_ref, w2_ref, h_ref, gc_ref, gl_ref, o_ref, acc, *, ctx_rows, tiles_per_batch):
    f = pl.program_id(1)
    x = x_ref[...]
    a = (_silu(_dot(x, w1_ref[...])) * _dot(x, w3_ref[...])).astype(BF16)
    part = _dot(a, w2_ref[...])

    @pl.when(f == 0)
    def _():
        acc[...] = part

    @pl.when(f > 0)
    def _():
        acc[...] += part

    @pl.when(f == pl.num_programs(1) - 1)
    def _():
        tm = x_ref.shape[0]
        r = (pl.program_id(0) % tiles_per_batch) * tm + lax.broadcasted_iota(jnp.int32, (tm, 1), 0)
        gate = jnp.where(r < ctx_rows, gc_ref[...], gl_ref[...])
        o_ref[...] = h_ref[...] + gate * acc[...]


def _ffn(fin, w1, w3, w2, h, mod3, T, CTX):
    rows = h.shape[0]
    tpb = 4
    tm = T // tpb
    ff = w1.shape[1]
    tf = 1024
    return pl.pallas_call(
        partial(_ffn_kernel, ctx_rows=CTX, tiles_per_batch=tpb), grid=(rows // tm, ff // tf),
        in_specs=[pl.BlockSpec((tm, D), lambda i, f: (i, 0)), pl.BlockSpec((D, tf), lambda i, f: (0, f)),
                  pl.BlockSpec((D, tf), lambda i, f: (0, f)), pl.BlockSpec((tf, D), lambda i, f: (f, 0)),
                  pl.BlockSpec((tm, D), lambda i, f: (i, 0)),
                  pl.BlockSpec((None, 1, D), lambda i, f: (5, 0, 0)),
                  pl.BlockSpec((None, 1, D), lambda i, f: ((1 + i // tpb) * 6 + 5, 0, 0))],
        out_specs=pl.BlockSpec((tm, D), lambda i, f: (i, 0)),
        out_shape=jax.ShapeDtypeStruct((rows, D), F32), scratch_shapes=[pltpu.VMEM((tm, D), F32)], name="ffn",
        compiler_params=_cp(("parallel", "arbitrary"), _VMEM_LIMIT))(fin, w1, w3, w2, h, mod3, mod3)


MOE_TILE = 1024
MOE_CHUNK = 256


def _slots_kernel(rg_ref, tri_ref, slot_ref, slott_ref, cnt_ref):
    blk = tri_ref.shape[0]
    carry = jnp.zeros((1, LANES), F32)
    for i in range(rg_ref.shape[0] // blk):
        sel = rg_ref[i * blk:(i + 1) * blk, :] > 0.0
        self32 = jnp.where(sel, 1.0, 0.0)
        rank = _dot(tri_ref[...], self32.astype(BF16)) + carry
        slot = jnp.where(sel, rank, -1.0)
        slot_ref[i * blk:(i + 1) * blk, :] = slot
        slott_ref[:, i * blk:(i + 1) * blk] = slot.T
        carry = carry + jnp.sum(self32, axis=0, keepdims=True)
    cnt_ref[...] = jnp.broadcast_to(carry, cnt_ref.shape)


def _moe_slots(rgate, tm):
    rows = rgate.shape[0]
    blk = 256
    tri = jnp.asarray(np.tril(np.ones((blk, blk), np.float32), -1), BF16)
    return pl.pallas_call(
        _slots_kernel, grid=(rows // tm,),
        in_specs=[pl.BlockSpec((tm, LANES), lambda i: (i, 0)), pl.BlockSpec((blk, blk), lambda i: (0, 0))],
        out_specs=[pl.BlockSpec((tm, LANES), lambda i: (i, 0)), pl.BlockSpec((LANES, tm), lambda i: (0, i)),
                   pl.BlockSpec((8, LANES), lambda i: (i, 0))],
        out_shape=[jax.ShapeDtypeStruct((rows, LANES), F32), jax.ShapeDtypeStruct((LANES, rows), F32),
                   jax.ShapeDtypeStruct((rows // tm * 8, LANES), F32)], name="moeslots",
        compiler_params=_cp(("parallel",)))(rgate, tri)


def _moe_kernel(nfull_ref, half_ref, x_ref, slott_ref, slot_ref, rg_ref, w1_ref, w3_ref, w2_ref, h_ref, gl_ref, o_ref,
                xe_ref, ye_ref):
    i, e, f = pl.program_id(0), pl.program_id(1), pl.program_id(2)
    CH = MOE_CHUNK
    nfull = nfull_ref[i * N_EXPERTS + e]
    half = half_ref[i * N_EXPERTS + e]

    @pl.when(jnp.logical_and(e == 0, f == 0))
    def _():
        o_ref[...] = h_ref[...]

    def expert(r0, rows):
        xe = xe_ref[pl.ds(r0, rows), :]
        a = (_silu(_dot(xe, w1_ref[...])) * _dot(xe, w3_ref[...])).astype(BF16)
        return _dot(a, w2_ref[...])

    def gather(r0, rows):
        want = lax.broadcasted_iota(jnp.int32, (rows, 1), 0).astype(F32) + r0.astype(F32)
        pick = jnp.where(slott_ref[pl.ds(e, 1), :] == want, 1.0, 0.0).astype(BF16)
        xe_ref[pl.ds(r0, rows), :] = _dot(pick, x_ref[...]).astype(BF16)
        ye_ref[pl.ds(r0, rows), :] = expert(r0, rows)

    def for_chunks(body):
        def full(c, carry):
            body(pl.multiple_of(c * CH, CH), CH)
            return carry

        lax.fori_loop(0, nfull, full, 0)

        @pl.when(half == 1)
        def _():
            body(pl.multiple_of(nfull * CH, CH), CH // 2)

    @pl.when(f == 0)
    def _():
        for_chunks(gather)

    @pl.when(f == 1)
    def _():
        mine = lax.broadcasted_iota(jnp.int32, (1, LANES), 1) == e
        scol = jnp.sum(jnp.where(mine, slot_ref[...], 0.0), axis=1, keepdims=True)
        scale = jnp.sum(jnp.where(mine, rg_ref[...], 0.0), axis=1, keepdims=True) * gl_ref[...]

        def scatter(r0, rows):
            ye = (ye_ref[pl.ds(r0, rows), :] + expert(r0, rows)).astype(BF16)
            want = lax.broadcasted_iota(jnp.int32, (1, rows), 1).astype(F32) + r0.astype(F32)
            put = jnp.where(scol == want, 1.0, 0.0).astype(BF16)
            o_ref[...] += scale * _dot(put, ye)

        for_chunks(scatter)


def _moe(fin, rgate, w1, w3, w2, h, mod3, SEQ):
    rows = h.shape[0]
    tm = min(SEQ, MOE_TILE)
    ff = w1.shape[2]
    tf = ff // 2
    tiles_per_batch = SEQ // tm
    nt = rows // tm
    slot, slott, cnt = _moe_slots(rgate, tm)
    counts = cnt.reshape(nt, 8, LANES)[:, 0, :N_EXPERTS].astype(jnp.int32).reshape(nt * N_EXPERTS)
    rem = counts % MOE_CHUNK
    nfull = counts // MOE_CHUNK + (rem > MOE_CHUNK // 2).astype(jnp.int32)
    half = jnp.logical_and(rem > 0, rem <= MOE_CHUNK // 2).astype(jnp.int32)
    grid_spec = pltpu.PrefetchScalarGridSpec(
        num_scalar_prefetch=2, grid=(nt, N_EXPERTS, 2),
        in_specs=[pl.BlockSpec((tm, D), lambda i, e, f, n, hf: (i, 0)),
                  pl.BlockSpec((8, tm), lambda i, e, f, n, hf: (0, i)),
                  pl.BlockSpec((tm, LANES), lambda i, e, f, n, hf: (i, 0)),
                  pl.BlockSpec((tm, LANES), lambda i, e, f, n, hf: (i, 0)),
                  pl.BlockSpec((None, D, tf), lambda i, e, f, n, hf: (e, 0, f)),
                  pl.BlockSpec((None, D, tf), lambda i, e, f, n, hf: (e, 0, f)),
                  pl.BlockSpec((None, tf, D), lambda i, e, f, n, hf: (e, f, 0)),
                  pl.BlockSpec((tm, D), lambda i, e, f, n, hf: (i, 0)),
                  pl.BlockSpec((None, 1, D), lambda i, e, f, n, hf: ((1 + i // tiles_per_batch) * 6 + 5, 0, 0))],
        out_specs=pl.BlockSpec((tm, D), lambda i, e, f, n, hf: (i, 0)),
        scratch_shapes=[pltpu.VMEM((tm, D), BF16), pltpu.VMEM((tm, D), F32)])
    return pl.pallas_call(
        _moe_kernel, grid_spec=grid_spec, out_shape=jax.ShapeDtypeStruct((rows, D), F32), name="moe",
        compiler_params=_cp(("parallel", "arbitrary", "arbitrary"), _VMEM_LIMIT))(
            nfull, half, fin, slott, slot, rgate, w1, w3, w2, h, mod3)


def _regroup_cols(w):
    hy, ssd, da, gate = 3072, 2592, 3072, 3072
    parts = [w[..., :hy], w[..., hy + ssd + da:hy + ssd + da + gate], w[..., hy + ssd:hy + ssd + da],
             w[..., hy:hy + ssd]]
    pad = jnp.zeros(w.shape[:-1] + (_PCOLS - hy - ssd - da - gate,), w.dtype)
    return jnp.concatenate(parts + [pad], axis=-1)


def kernel(x, c, ctx, c_ctx, w_mod, b_mod, norm1_g, norm2_g, w_in, hy_conv_w, hy_conv_b, hy_w1, hy_b1, hy_w2, hy_b2,
           hy_w3, hy_b3, hy_w4, hy_freq, hy_bias, ssd_conv_w, ssd_conv_b, ssd_dt_bias, ssd_a_log, ssd_d, ssd_norm_g,
           da_q_norm, da_k_norm, da_lambda, da_subln_g, w_branch, w_out, ffn_w1, ffn_w3, ffn_w2, router_w, moe_w1,
           moe_w3, moe_w2):
    B, SEQ, _ = x.shape
    CTX = ctx.shape[1]
    assert CTX == ROW_TILE and SEQ % ROW_TILE == 0 and B % 2 == 0 and B + 1 <= 8
    T = CTX + SEQ
    tpb = T // ROW_TILE
    depth = w_in.shape[0]
    R = B * T

    h = jnp.concatenate([ctx, x], axis=1).reshape(R, D)
    cvec = jnp.zeros((8, D), F32).at[0].set(c_ctx).at[1:1 + B].set(c)
    cos_t, sin_t = _rope_tables(SEQ, CTX)
    tabs_l, tabs_c = _fft_tables(SEQ), _fft_tables(CTX)
    nb = D // LANES

    for i in range(depth):
        need_ctx = i < depth - 1
        mod = _mod(cvec, w_mod[i], b_mod[i][None])
        mod3 = mod.reshape(8 * 6, 1, D)
        xn = _normmod(h, norm1_g[i][None], mod3, T, CTX)
        p2 = _matmul(xn, _regroup_cols(w_in[i].astype(BF16)), T // 2, 2048, BF16)
        p3 = p2.reshape(B, T, _PCOLS)

        cw, cb = hy_conv_w[i], hy_conv_b[i][None]
        hy_args = (hy_w1[i], hy_b1[i], hy_w2[i], hy_b2[i], hy_w3[i], hy_b3[i], hy_w4[i], hy_freq[i])
        kh = _hy_kfft(_hy_filters(SEQ, *hy_args), hy_bias[i], tabs_l, SEQ)
        z1 = _hy_conv(p3, 0, p3, nb, cw, cb, kh, 0, tabs_l, SEQ, CTX, SEQ, 0, True)
        y_hy = _hy_conv(z1, 0, p3, 2 * nb, cw, cb, kh, 1, tabs_l, SEQ, CTX, SEQ, CTX, False)
        if need_ctx:
            khc = _hy_kfft(_hy_filters(CTX, *hy_args), hy_bias[i], tabs_c, CTX)
            z1c = _hy_conv(p3, 0, p3, nb, cw, cb, khc, 0, tabs_c, CTX, 0, CTX, 0, True)
            y_hy = _hy_conv(z1c, 0, p3, 2 * nb, cw, cb, khc, 1, tabs_c, CTX, 0, CTX, 0, False, alias_into=y_hy)

        xbc = _ssd_pre(p3, ssd_conv_w[i], ssd_conv_b[i][None], CTX)
        dtb = jnp.pad(ssd_dt_bias[i].reshape(1, 2 * SSD_HEADS), ((0, 0), (0, LANES - 2 * SSD_HEADS)))
        arow = jnp.pad(-jnp.exp(ssd_a_log[i].astype(F32)).reshape(1, 2 * SSD_HEADS),
                       ((0, 0), (0, LANES - 2 * SSD_HEADS)))
        y_rev = _ssd_scan(xbc, p3, dtb, arow, True, CTX)
        dskip = jnp.repeat(ssd_d[i].astype(F32), SSD_HEADDIM)[None]
        y_ssd = _ssd_scan(xbc, p3, dtb, arow, False, CTX, fin=(y_rev, dskip, ssd_norm_g[i][None]))

        lam_init = 0.8 - 0.6 * math.exp(-0.3 * (i + 1))
        gq = jnp.tile(da_q_norm[i], D // DA_HEAD_DIM)[None]
        gk = jnp.tile(da_k_norm[i], D // DA_HEAD_DIM)[None]
        qn, kn, vt = _da_pre(p2, gq, gk, cos_t, sin_t, tpb)
        score_bound = 8.0 * jnp.max(jnp.abs(da_q_norm[i])) * jnp.max(jnp.abs(da_k_norm[i]))
        safe = (score_bound <= ATTN_SAFE_SCORE).astype(jnp.int32).reshape(1)
        da_args = (safe, da_lambda[i], qn.reshape(B, T, D), kn.reshape(B, T, D), vt, da_subln_g[i][:, None], lam_init,
                   CTX)
        y_da = _da_attn(*da_args, False)
        y_dac = _da_attn(*da_args, True) if need_ctx else y_da

        wb, wo = w_branch[i].astype(BF16), w_out[i].astype(BF16)
        flat = lambda a: a.reshape(-1, D)
        j = i // 2
        if i % 2 == 0:
            h, fin = _merge(flat(y_hy), flat(y_ssd), flat(y_da), flat(y_dac), p2, h, wb, wo, mod3, norm2_g[i][None],
                            tpb, latent_only=not need_ctx)
            assert need_ctx, "dense FFN layers are expected to carry context rows"
            h = _ffn(fin, ffn_w1[j].astype(BF16), ffn_w3[j].astype(BF16), ffn_w2[j].astype(BF16), h, mod3, T, CTX)
        else:
            assert not need_ctx, "expert layers are expected to be latent only"
            rw = jnp.pad(router_w[j], ((0, 0), (0, LANES - N_EXPERTS)))
            h, fin, rgate = _merge(flat(y_hy), flat(y_ssd), flat(y_da), flat(y_dac), p2, h, wb, wo, mod3,
                                   norm2_g[i][None], tpb, latent_only=True, router_w=rw)
            h = _moe(fin, rgate, moe_w1[j].astype(BF16), moe_w3[j].astype(BF16), moe_w2[j].astype(BF16), h, mod3,
                     SEQ)
    return h.reshape(B, SEQ, D)
```

```python
import math
from functools import partial

import numpy as np
import jax
import jax.numpy as jnp
from jax import lax
from jax.experimental import pallas as pl
from jax.experimental.pallas import tpu as pltpu

F32, BF16 = jnp.float32, jnp.bfloat16
HIGHEST = lax.Precision.HIGHEST

D = 1024
EPS = 1e-6
GRID_W = 64
ROW_TILE = 256
LANES = 128
SSD_CHUNK = 128
SSD_HEADS = 16
SSD_HEADDIM = 64
DA_HEADS = 8
DA_HEAD_DIM = 64
N_EXPERTS = 8
HY_BANDS = 16
HY_EMB = 2 * HY_BANDS + 1
HY_FFN = 64

_PCOL_HY = 0
_PCOL_GATE = 3072
_PCOL_DA = 6144
_PCOL_Z = 9216
_PCOL_XBC = 10240
_PCOL_DT = 11776
_PCOLS = 12288
_VMEM_LIMIT = 56 * 1024 * 1024
_VMEM_LIMIT_FFN = 60 * 1024 * 1024


def _cp(sem, vmem=None):
    return pltpu.CompilerParams(dimension_semantics=sem, vmem_limit_bytes=vmem)


def _dot(a, b, prec=None):
    return jnp.dot(a, b, preferred_element_type=F32, precision=prec)


def _silu(x):
    return x * jax.nn.sigmoid(x)


def _mod_kernel(c_ref, w_ref, b_ref, o_ref):
    o_ref[...] = _dot(_silu(c_ref[...]), w_ref[...], HIGHEST) + b_ref[...]


def _mod(cvec, w, b):
    return pl.pallas_call(
        _mod_kernel, grid=(6,),
        in_specs=[pl.BlockSpec((8, D), lambda j: (0, 0)), pl.BlockSpec((D, D), lambda j: (0, j)),
                  pl.BlockSpec((1, D), lambda j: (0, j))],
        out_specs=pl.BlockSpec((8, D), lambda j: (0, j)),
        out_shape=jax.ShapeDtypeStruct((8, 6 * D), F32), name="mod")(cvec, w, b)


def _modrow(t, tpb):
    return jnp.where(t % tpb == 0, 0, 1 + t // tpb)


def _mod_spec(k, tpb):
    return pl.BlockSpec((None, 1, D), lambda t: (_modrow(t, tpb) * 6 + k, 0, 0))


def _normmod_kernel(h_ref, g_ref, shc_ref, scc_ref, shl_ref, scl_ref, o_ref, *, ctx_rows, tiles_per_batch):
    x = h_ref[...]
    tm = x.shape[0]
    r = (pl.program_id(0) % tiles_per_batch) * tm + lax.broadcasted_iota(jnp.int32, (tm, 1), 0)
    is_ctx = r < ctx_rows
    y = x * lax.rsqrt(jnp.mean(x * x, axis=-1, keepdims=True) + EPS) * g_ref[...]
    o_ref[...] = (y * (1.0 + jnp.where(is_ctx, scc_ref[...], scl_ref[...]))
                  + jnp.where(is_ctx, shc_ref[...], shl_ref[...])).astype(o_ref.dtype)


def _normmod(h, g, mod3, T, CTX):
    rows = h.shape[0]
    tpb = 4
    tm = T // tpb
    ctx = lambda k: pl.BlockSpec((None, 1, D), lambda t: (k, 0, 0))
    lat = lambda k: pl.BlockSpec((None, 1, D), lambda t: ((1 + t // tpb) * 6 + k, 0, 0))
    return pl.pallas_call(
        partial(_normmod_kernel, ctx_rows=CTX, tiles_per_batch=tpb), grid=(rows // tm,),
        in_specs=[pl.BlockSpec((tm, D), lambda t: (t, 0)), pl.BlockSpec((1, D), lambda t: (0, 0)),
                  ctx(0), ctx(1), lat(0), lat(1)],
        out_specs=pl.BlockSpec((tm, D), lambda t: (t, 0)),
        out_shape=jax.ShapeDtypeStruct((rows, D), BF16), name="normmod",
        compiler_params=_cp(("parallel",)))(h, g, mod3, mod3, mod3, mod3)


def _mm_kernel(a_ref, b_ref, o_ref):
    o_ref[...] = _dot(a_ref[...], b_ref[...]).astype(o_ref.dtype)


def _matmul(a, b, tm, tn, out_dtype):
    m, k = a.shape
    n = b.shape[1]
    return pl.pallas_call(
        _mm_kernel, grid=(n // tn, m // tm),
        in_specs=[pl.BlockSpec((tm, k), lambda j, i: (i, 0)), pl.BlockSpec((k, tn), lambda j, i: (0, j))],
        out_specs=pl.BlockSpec((tm, tn), lambda j, i: (i, j)),
        out_shape=jax.ShapeDtypeStruct((m, n), out_dtype), name="inproj",
        compiler_params=_cp(("parallel", "parallel"), _VMEM_LIMIT))(a, b)


def _hyfilt_kernel(f_ref, w1, b1, w2, b2, w3, b3, w4, fr_ref, dl_ref, o_ref, h_ref, win_ref, *, shift):
    @pl.when(pl.program_id(1) == 0)
    def _():
        f = f_ref[...]
        fr = fr_ref[...]
        h = jnp.sin(fr[0:1] * (_dot(f, w1[...], HIGHEST) + b1[...]))
        h = jnp.sin(fr[1:2] * (_dot(h, w2[...], HIGHEST) + b2[...]))
        h = jnp.sin(fr[2:3] * (_dot(h, w3[...], HIGHEST) + b3[...]))
        h_ref[...] = h.astype(BF16)
        win_ref[...] = jnp.exp(-f[:, 0:1] * dl_ref[...]) + shift

    o_ref[...] = _dot(h_ref[...], w4[...]) * win_ref[...]


def _hy_filters(L, w1, b1, w2, b2, w3, b3, w4, freq):
    t = jnp.linspace(0.0, 1.0, L, dtype=F32)[:, None]
    w = 2.0 * math.pi * jnp.arange(L, dtype=F32)[:, None] / L
    f = jnp.linspace(1e-4, HY_BANDS - 1, HY_BANDS, dtype=F32)[None, :]
    feats = jnp.concatenate([t, jnp.cos(f * w), -jnp.sin(f * w)], axis=-1)
    feats = jnp.pad(feats, ((0, 0), (0, HY_FFN - HY_EMB)))
    w1p = jnp.pad(w1, ((0, HY_FFN - HY_EMB), (0, 0)))
    max_decay = math.log(1e-2) / 0.3
    min_decay = math.log(1e-2) / 1.5
    deltas = jnp.abs(jnp.linspace(min_decay, max_decay, D, dtype=F32))[None, :]
    tl = min(L, 512)
    full = lambda r, c: pl.BlockSpec((r, c), lambda i, j: (0, 0))
    return pl.pallas_call(
        partial(_hyfilt_kernel, shift=0.05), grid=(L // tl, 4),
        in_specs=[pl.BlockSpec((tl, HY_FFN), lambda i, j: (i, 0)), full(HY_FFN, HY_FFN), full(1, HY_FFN),
                  full(HY_FFN, HY_FFN), full(1, HY_FFN), full(HY_FFN, HY_FFN), full(1, HY_FFN),
                  pl.BlockSpec((HY_FFN, D), lambda i, j: (0, j)), full(3, HY_FFN), full(1, D)],
        out_specs=pl.BlockSpec((tl, D), lambda i, j: (i, j)),
        out_shape=jax.ShapeDtypeStruct((L, 4 * D), F32), name="hyfilt",
        scratch_shapes=[pltpu.VMEM((tl, HY_FFN), BF16), pltpu.VMEM((tl, D), F32)],
        compiler_params=_cp(("parallel", "arbitrary")))(
            feats, w1p, b1[None], w2, b2[None], w3, b3[None], w4.astype(BF16), freq, deltas)


def _fft_split(L):
    n = 2 * L
    n2 = 128 if L >= 2048 else n // 32
    return n // n2, n2


def _fft_tables(L):
    n1_, n2_ = _fft_split(L)
    n, h1 = 2 * L, n1_ // 2
    n2 = np.arange(n2_)[:, None, None]
    k1 = np.arange(n1_)[None, :, None]
    n1 = np.arange(h1)[None, None, :]
    g = np.exp(-2j * np.pi * (n2 * k1 / n + n1 * k1 / n1_))
    ma = np.concatenate([np.concatenate([g.real, -g.imag], 2), np.concatenate([g.imag, g.real], 2)], 1)
    kk = np.arange(n2_)
    f = np.exp(-2j * np.pi * np.outer(kk, kk) / n2_)
    mc = np.block([[f.real, -f.imag], [f.imag, f.real]])
    mci = np.block([[f.real, f.imag], [-f.imag, f.real]])
    gi = np.exp(2j * np.pi * (n2 * np.arange(n1_)[None, None, :] / n
                              + np.arange(h1)[None, :, None] * np.arange(n1_)[None, None, :] / n1_)) / n
    mai = np.concatenate([np.concatenate([gi.real, -gi.imag], 2), np.concatenate([gi.imag, gi.real], 2)], 1)
    cast = lambda a: jnp.asarray(a, F32).astype(BF16)
    return cast(ma), cast(ma[:, :, :h1]), cast(mc), cast(mci), cast(mai)


def _kfft_kernel(kf_ref, kb_ref, bias_ref, mar_ref, mc_ref, o_ref, af_ref, ab_ref, *, N1, N2):
    H1, PA = N1 // 2, 2 * N1 + 8

    def stage_a(n2, c):
        for src, dst in ((kf_ref, af_ref), (kb_ref, ab_ref)):
            slab = src[pl.ds(n2, H1, stride=N2), :].astype(BF16)
            dst[pl.ds(pl.multiple_of(n2 * PA, 8), 2 * N1), :] = _dot(mar_ref[n2], slab)
        return c

    lax.fori_loop(0, N2, stage_a, 0, unroll=8)
    bias = bias_ref[...]

    def stage_c(k1, c):
        def rows(ref):
            return jnp.concatenate([ref[pl.ds(k1, N2, stride=PA), :], ref[pl.ds(N1 + k1, N2, stride=PA), :]], 0)
        x = _dot(mc_ref[...], jnp.concatenate([rows(af_ref), rows(ab_ref)], 1).astype(BF16))
        xf, xb = x[:, :LANES], x[:, LANES:]
        o_ref[k1] = jnp.concatenate([xf[:N2] + xb[:N2] + bias, xf[N2:] - xb[N2:]], 0).astype(o_ref.dtype)
        return c

    lax.fori_loop(0, N1, stage_c, 0, unroll=16)


def _hy_kfft(k4, bias, tabs, L):
    N1, N2 = _fft_split(L)
    _, mar, mc, _, _ = tabs
    nb = D // LANES
    return pl.pallas_call(
        partial(_kfft_kernel, N1=N1, N2=N2), grid=(2, nb),
        in_specs=[pl.BlockSpec((L, LANES), lambda o, j: (0, o * 2 * nb + j)),
                  pl.BlockSpec((L, LANES), lambda o, j: (0, o * 2 * nb + nb + j)),
                  pl.BlockSpec((None, 1, LANES), lambda o, j: (o, 0, j)),
                  pl.BlockSpec(mar.shape, lambda o, j: (0, 0, 0)), pl.BlockSpec(mc.shape, lambda o, j: (0, 0))],
        out_specs=pl.BlockSpec((None, N1, 2 * N2, LANES), lambda o, j: (o, 0, 0, j)),
        out_shape=jax.ShapeDtypeStruct((2, N1, 2 * N2, D), BF16),
        scratch_shapes=[pltpu.VMEM((N2 * (2 * N1 + 8), LANES), F32)] * 2, name="hykfft",
        compiler_params=_cp(("parallel", "parallel"), _VMEM_LIMIT))(k4, k4, bias[:, None, :], mar, mc)


def _hyconv_kernel(sig_ref, gate_ref, cws_ref, cbs_ref, cwg_ref, cbg_ref, kh_ref, ma_ref, mc_ref, mci_ref,
                   mai_ref, *rest, L, N1, N2, in_row0, out_row0, sig_conv, aliased):
    if aliased:
        rest = rest[1:]
    o_ref, sig0, sig1, gat0, gat1, a_ref = rest
    H1, PA = N1 // 2, 2 * N1 + 8
    NP, PADR = N2 + 8, 16
    sigs, gats = (sig0, sig1), (gat0, gat1)

    def stage(ref, nat, bb, r0):
        nat[0:PADR, :] = jnp.zeros((PADR, LANES), F32)
        nat[PADR + H1 * NP:PADR + H1 * NP + 8, :] = jnp.zeros((8, LANES), F32)
        for n1 in range(H1):
            nat[PADR + n1 * NP:PADR + n1 * NP + N2, :] = ref[bb, r0 + n1 * N2:r0 + (n1 + 1) * N2, :].astype(F32)

    def slab(nat, n2, w, b):
        u = nat[pl.ds(PADR + n2, H1, stride=NP), :]
        if w is None:
            return u
        prev = jnp.where(n2 == 0, PADR - NP + N2 - 1, PADR + n2 - 1)
        nxt = jnp.where(n2 == N2 - 1, PADR + NP, PADR + n2 + 1)
        return (w[0:1] * nat[pl.ds(prev, H1, stride=NP), :] + w[1:2] * u
                + w[2:3] * nat[pl.ds(nxt, H1, stride=NP), :] + b)

    for bb in range(2):
        stage(sig_ref, sigs[bb], bb, in_row0 if sig_conv else 0)
        stage(gate_ref, gats[bb], bb, in_row0)
    ws, bs = (cws_ref[...], cbs_ref[...]) if sig_conv else (None, None)
    wg, bg = cwg_ref[...], cbg_ref[...]

    def stage_a(n2, c):
        st = jnp.concatenate([slab(sigs[0], n2, ws, bs), slab(sigs[1], n2, ws, bs)], 0).astype(BF16)
        a_ref[pl.ds(pl.multiple_of(n2 * PA, 8), 2 * N1), :] = _dot(ma_ref[n2], st)
        return c

    lax.fori_loop(0, N2, stage_a, 0, unroll=16)

    def stage_c(i, c):
        k1 = 2 * i

        def rows(k):
            return jnp.concatenate([a_ref[pl.ds(k, N2, stride=PA), :], a_ref[pl.ds(N1 + k, N2, stride=PA), :]], 0)

        x = _dot(mc_ref[...], jnp.concatenate([rows(k1), rows(k1 + 1)], 1).astype(BF16))
        kh = jnp.concatenate([kh_ref[k1], kh_ref[k1 + 1]], 1).astype(F32)
        xr, xi, kr, ki = x[:N2], x[N2:], kh[:N2], kh[N2:]
        y = jnp.concatenate([xr * kr - xi * ki, xr * ki + xi * kr], 0)
        cc = _dot(mci_ref[...], y.astype(BF16))
        for j in range(2):
            a_ref[pl.ds(k1 + j, N2, stride=PA), :] = cc[:N2, j * LANES:(j + 1) * LANES]
            a_ref[pl.ds(N1 + k1 + j, N2, stride=PA), :] = cc[N2:, j * LANES:(j + 1) * LANES]
        return c

    lax.fori_loop(0, N1 // 2, stage_c, 0, unroll=16)

    def stage_ai(n2, c):
        st = a_ref[pl.ds(pl.multiple_of(n2 * PA, 8), 2 * N1), :]
        g = jnp.concatenate([slab(gats[0], n2, wg, bg), slab(gats[1], n2, wg, bg)], 0)
        r = _dot(mai_ref[n2], st.astype(BF16)) * g
        sig0[pl.ds(PADR + n2, H1, stride=NP), :] = r[:H1]
        sig1[pl.ds(PADR + n2, H1, stride=NP), :] = r[H1:]
        return c

    lax.fori_loop(0, N2, stage_ai, 0, unroll=16)

    for bb in range(2):
        if out_row0:
            o_ref[bb, 0:out_row0, :] = jnp.zeros((out_row0, LANES), o_ref.dtype)
        for n1 in range(H1):
            o_ref[bb, out_row0 + n1 * N2:out_row0 + (n1 + 1) * N2, :] = (
                sigs[bb][PADR + n1 * NP:PADR + n1 * NP + N2, :].astype(o_ref.dtype))


def _hy_conv(sig, sig_col, gate, gate_col, cw, cb, kh, order, tabs, L, in_row0, out_rows, out_row0, sig_conv,
             alias_into=None):
    N1, N2 = _fft_split(L)
    ma, _, mc, mci, mai = tabs
    H1 = N1 // 2
    B = gate.shape[0]
    nb = D // LANES
    scol = sig_col if sig_conv else 0
    cspec = lambda r, c0: pl.BlockSpec((r, LANES), lambda j, q: (0, c0 + j))
    const = lambda a: pl.BlockSpec(a.shape, lambda j, q: (0,) * a.ndim)
    in_specs = [pl.BlockSpec((2, sig.shape[1], LANES), lambda j, q: (q, 0, sig_col + j)),
                pl.BlockSpec((2, gate.shape[1], LANES), lambda j, q: (q, 0, gate_col + j)),
                cspec(3, scol), cspec(1, scol), cspec(3, gate_col), cspec(1, gate_col),
                pl.BlockSpec((None, N1, 2 * N2, LANES), lambda j, q: (order, 0, 0, j)),
                const(ma), const(mc), const(mci), const(mai)]
    args = [sig, gate, cw, cb, cw, cb, kh, ma, mc, mci, mai]
    aliases = {}
    if alias_into is not None:
        in_specs.append(pl.BlockSpec(memory_space=pl.ANY))
        args.append(alias_into)
        aliases = {len(args) - 1: 0}
        out_total = alias_into.shape[1]
    else:
        out_total = out_row0 + out_rows
    return pl.pallas_call(
        partial(_hyconv_kernel, L=L, N1=N1, N2=N2, in_row0=in_row0, out_row0=out_row0, sig_conv=sig_conv,
                aliased=alias_into is not None),
        grid=(nb, B // 2), in_specs=in_specs,
        out_specs=pl.BlockSpec((2, out_row0 + out_rows, LANES), lambda j, q: (q, 0, j)),
        out_shape=jax.ShapeDtypeStruct((B, out_total, D), BF16),
        scratch_shapes=[pltpu.VMEM((H1 * (N2 + 8) + 24, LANES), F32)] * 4
        + [pltpu.VMEM((N2 * (2 * N1 + 8), LANES), F32)],
        input_output_aliases=aliases, name="hyconv",
        compiler_params=_cp(("parallel", "parallel"), _VMEM_LIMIT))(*args)


def _ssdpre_kernel(x_ref, w_ref, b_ref, o_ref, nat, *, T, CTX):
    W = x_ref.shape[1]
    nat[0:8, :] = jnp.zeros((8, W), F32)
    nat[8 + T:16 + T, :] = jnp.zeros((8, W), F32)
    for i in range(T // ROW_TILE):
        nat[8 + i * ROW_TILE:8 + (i + 1) * ROW_TILE, :] = x_ref[i * ROW_TILE:(i + 1) * ROW_TILE, :].astype(F32)
    w = w_ref[...]
    b = b_ref[...]
    row = lax.broadcasted_iota(jnp.int32, (ROW_TILE, 1), 0)
    for i in range(T // ROW_TILE):
        r0 = 8 + i * ROW_TILE
        xm = nat[r0 - 1:r0 - 1 + ROW_TILE, :]
        xp = nat[r0 + 1:r0 + 1 + ROW_TILE, :]
        if i * ROW_TILE == CTX:
            xm = jnp.where(row == 0, 0.0, xm)
        if (i + 1) * ROW_TILE == CTX:
            xp = jnp.where(row == ROW_TILE - 1, 0.0, xp)
        u = w[0:1] * xm + w[1:2] * nat[r0:r0 + ROW_TILE, :] + w[2:3] * xp + b
        o_ref[i * ROW_TILE:(i + 1) * ROW_TILE, :] = _silu(u).astype(o_ref.dtype)


def _ssd_pre(p3, cw, cb, CTX):
    B, T, _ = p3.shape
    W = 256
    nblk = cw.shape[1] // W
    return pl.pallas_call(
        partial(_ssdpre_kernel, T=T, CTX=CTX), grid=(B, nblk),
        in_specs=[pl.BlockSpec((None, T, W), lambda b, j: (b, 0, _PCOL_XBC // W + j)),
                  pl.BlockSpec((3, W), lambda b, j: (0, j)), pl.BlockSpec((1, W), lambda b, j: (0, j))],
        out_specs=pl.BlockSpec((None, T, W), lambda b, j: (b, 0, j)),
        out_shape=jax.ShapeDtypeStruct((B, T, cw.shape[1]), BF16),
        scratch_shapes=[pltpu.VMEM((T + 16, W), F32)], name="ssdpre",
        compiler_params=_cp(("parallel", "parallel")))(p3, cw, cb)


SSD_CHUNKS_PER_STEP = 2


def _ssdscan_kernel(xbc_ref, dt_ref, dtb_ref, a_ref, tri_ref, *rest, rev):
    if rev:
        o_ref, s_ref = rest
    else:
        z_ref, yr_ref, dsk_ref, ng_ref, o_ref, s_ref, ybuf = rest
    Q = SSD_CHUNK

    @pl.when(pl.program_id(1) == 0)
    def _():
        s_ref[...] = jnp.zeros(s_ref.shape, F32)

    tri = tri_ref[...]
    mask = tri > 0.5
    last = 0 if rev else Q - 1
    col0 = SSD_HEADS if rev else 0
    lane_lo = lax.broadcasted_iota(jnp.int32, (1, LANES), 1) < SSD_HEADDIM

    def one_chunk(r0):
        rows = slice(r0, r0 + Q)
        x = xbc_ref[rows, :]
        xs, Bm, Cm = x[:, :D], x[:, D:D + 2 * LANES], x[:, D + 2 * LANES:]
        dtr = dt_ref[rows, :].astype(F32) + dtb_ref[...]
        dt = jnp.maximum(dtr, 0.0) + jnp.log(1.0 + jnp.exp(-jnp.abs(dtr)))
        cs = _dot(tri, dt * a_ref[...], HIGHEST)
        csT = cs.T
        dtT = dt.T
        for g in range(2):
            Bg = Bm[:, g * LANES:(g + 1) * LANES]
            Cg = Cm[:, g * LANES:(g + 1) * LANES]
            CB = lax.dot_general(Cg, Bg, (((1,), (1,)), ((), ())), preferred_element_type=F32)
            BgT = Bg.astype(F32).T
            Cgf = Cg.astype(F32)
            for pr in range(4):
                hp = g * 4 + pr
                xs_pair = xs[:, hp * LANES:(hp + 1) * LANES]
                Sp = s_ref[hp]
                rhs = jnp.concatenate([xs_pair, Sp.astype(BF16)], 0)
                ys, sts = [], []
                for hh in range(2):
                    col = col0 + 2 * hp + hh
                    csl = jnp.broadcast_to(cs[:, col:col + 1], (Q, Q))
                    csr = csT[col:col + 1, :]
                    dtr_row = dtT[col:col + 1, :]
                    Lm = jnp.exp(jnp.where(mask, csl - csr, -1e30))
                    lhs = jnp.concatenate([CB * Lm * dtr_row, Cgf * jnp.exp(csl)], 1).astype(BF16)
                    ys.append(_dot(lhs, rhs))
                    tot = csT[col:col + 1, last:last + 1]
                    w_row = jnp.exp(tot - csr) * dtr_row
                    st = _dot((BgT * w_row).astype(BF16), xs_pair)
                    sts.append(jnp.exp(tot) * Sp + st)
                y_pair = jnp.where(lane_lo, ys[0], ys[1])
                s_ref[hp] = jnp.where(lane_lo, sts[0], sts[1])
                sl = slice(hp * LANES, (hp + 1) * LANES)
                if rev:
                    o_ref[rows, sl] = y_pair.astype(o_ref.dtype)
                else:
                    ybuf[:, sl] = y_pair + yr_ref[rows, sl].astype(F32) + xs_pair.astype(F32) * dsk_ref[:, sl]
        if not rev:
            y = ybuf[...] * _silu(z_ref[rows, :].astype(F32))
            half = D // 2
            parts = []
            for g in range(2):
                yg = y[:, g * half:(g + 1) * half]
                parts.append(yg * lax.rsqrt(jnp.mean(yg * yg, axis=-1, keepdims=True) + EPS))
            o_ref[rows, :] = (jnp.concatenate(parts, 1) * ng_ref[...]).astype(o_ref.dtype)

    subs = range(SSD_CHUNKS_PER_STEP)
    for sub in (reversed(subs) if rev else subs):
        one_chunk(sub * Q)


def _ssd_scan(xbc, p3, dtb, arow, rev, CTX, fin=None):
    B, T, _ = xbc.shape
    Q = SSD_CHUNK
    R = SSD_CHUNKS_PER_STEP * Q
    assert CTX % R == 0 and T % R == 0
    nst, nc = T // R, CTX // R
    if rev:
        step = lambda s: jnp.where(s < nc, nc - 1 - s, nst - 1 + nc - s)
        tri = jnp.asarray(np.triu(np.ones((Q, Q), np.float32)))
    else:
        step = lambda s: s
        tri = jnp.asarray(np.tril(np.ones((Q, Q), np.float32)))
    row = lambda w: pl.BlockSpec((1, w), lambda b, s: (0, 0))
    in_specs = [pl.BlockSpec((None, R, xbc.shape[2]), lambda b, s: (b, step(s), 0)),
                pl.BlockSpec((None, R, LANES), lambda b, s: (b, step(s), _PCOL_DT // LANES)),
                row(LANES), row(LANES), pl.BlockSpec((Q, Q), lambda b, s: (0, 0))]
    args = [xbc, p3, dtb, arow, tri]
    scratch = [pltpu.VMEM((SSD_HEADS // 2, LANES, LANES), F32)]
    if not rev:
        yrev, dskip, ng = fin
        in_specs += [pl.BlockSpec((None, R, D), lambda b, s: (b, s, _PCOL_Z // D)),
                     pl.BlockSpec((None, R, D), lambda b, s: (b, s, 0)), row(D), row(D)]
        args += [p3, yrev, dskip, ng]
        scratch.append(pltpu.VMEM((Q, D), F32))
    return pl.pallas_call(
        partial(_ssdscan_kernel, rev=rev), grid=(B, nst), in_specs=in_specs,
        out_specs=pl.BlockSpec((None, R, D), lambda b, s: (b, step(s), 0)),
        out_shape=jax.ShapeDtypeStruct((B, T, D), BF16), scratch_shapes=scratch,
        name="ssdrev" if rev else "ssdfwd",
        compiler_params=_cp(("parallel", "arbitrary")))(*args)


def _dapre_kernel(q_ref, k_ref, v_ref, gq_ref, gk_ref, cos_ref, sin_ref, ones_ref, qo_ref, ko_ref, vt_ref):
    ones = ones_ref[...]
    cosf = jnp.concatenate([cos_ref[...]] * (D // LANES), 1)
    sinf = jnp.concatenate([sin_ref[...]] * (D // LANES), 1)
    lane = lax.broadcasted_iota(jnp.int32, (1, D), 1)
    first_half = (lane % 32) < 16

    def norm_rope(x_ref, g_ref, scale):
        x = x_ref[...].astype(F32)
        sq = x * x
        hi = sq.astype(BF16)
        lo = (sq - hi.astype(F32)).astype(BF16)
        parts = []
        for blk in range(D // 256):
            sl = slice(blk * 256, (blk + 1) * 256)
            parts.append(_dot(hi[:, sl], ones) + _dot(lo[:, sl], ones))
        ss = jnp.concatenate(parts, 1)
        y = x * lax.rsqrt(ss * (1.0 / DA_HEAD_DIM) + EPS) * g_ref[...]
        partner = jnp.where(first_half, pltpu.roll(y, D - 16, 1), pltpu.roll(y, 16, 1))
        return ((y * cosf + partner * sinf) * scale).astype(BF16)

    qo_ref[...] = norm_rope(q_ref, gq_ref, LOG2E * DA_HEAD_DIM ** -0.5)
    ko_ref[...] = norm_rope(k_ref, gk_ref, 1.0)
    vt_ref[...] = v_ref[...].astype(F32).T.astype(BF16)


def _rope_tables(L, CTX):
    rows = L // GRID_W
    r = np.repeat(np.arange(rows), GRID_W)
    c = np.tile(np.arange(GRID_W), rows)
    nf = DA_HEAD_DIM // 4
    inv = jnp.asarray(10000.0, F32) ** (-jnp.arange(nf, dtype=F32) / nf)
    ang = jnp.stack([jnp.asarray(r), jnp.asarray(c)], -1).astype(F32)[:, :, None] * inv
    cos, sin = jnp.cos(ang), jnp.sin(ang)
    cos64 = jnp.stack([cos, cos], 2).reshape(L, DA_HEAD_DIM)
    sin64 = jnp.stack([-sin, sin], 2).reshape(L, DA_HEAD_DIM)
    cos_t = jnp.concatenate([jnp.ones((CTX, DA_HEAD_DIM), F32), cos64], 0)
    sin_t = jnp.concatenate([jnp.zeros((CTX, DA_HEAD_DIM), F32), sin64], 0)
    return jnp.tile(cos_t, (1, 2)), jnp.tile(sin_t, (1, 2))


def _da_pre(p2, gq, gk, cos_t, sin_t, tpb):
    rows = p2.shape[0]
    ones = jnp.asarray(np.kron(np.eye(256 // DA_HEAD_DIM), np.ones((DA_HEAD_DIM, DA_HEAD_DIM))), BF16)
    tab = pl.BlockSpec((ROW_TILE, LANES), lambda t: (t % tpb, 0))
    row = pl.BlockSpec((1, D), lambda t: (0, 0))
    out = pl.BlockSpec((ROW_TILE, D), lambda t: (t, 0))
    seg = lambda k: pl.BlockSpec((ROW_TILE, D), lambda t: (t, _PCOL_DA // D + k))
    return pl.pallas_call(
        _dapre_kernel, grid=(rows // ROW_TILE,),
        in_specs=[seg(0), seg(1), seg(2), row, row, tab, tab, pl.BlockSpec((256, 256), lambda t: (0, 0))],
        out_specs=[out, out, pl.BlockSpec((None, D, ROW_TILE), lambda t: (t // tpb, 0, t % tpb))],
        out_shape=[jax.ShapeDtypeStruct((rows, D), BF16)] * 2
        + [jax.ShapeDtypeStruct((rows // ROW_TILE // tpb, D, tpb * ROW_TILE), BF16)], name="dapre",
        compiler_params=_cp(("parallel",)))(p2, p2, p2, gq, gk, cos_t, sin_t, ones)


ATTN_KEY_BLOCK = 1024
ATTN_SAFE_SCORE = 30.0
LOG2E = 1.4426950408889634


def _attn_kernel(safe_ref, lam_ref, *refs, lam_init):
    lv = lam_ref[...]
    lam = (jnp.exp(jnp.sum(lv[0:1] * lv[1:2], keepdims=True)) - jnp.exp(jnp.sum(lv[2:3] * lv[3:4], keepdims=True))
           + lam_init)
    *q_refs, k_ref, vt_ref, g_ref, o_ref = refs
    q = jnp.concatenate([r[...] for r in q_refs], 0)
    tq = q.shape[0]
    lane_lo = lax.broadcasted_iota(jnp.int32, (1, LANES), 1) < DA_HEAD_DIM
    qcat = jnp.concatenate([jnp.where(lane_lo, q, jnp.zeros_like(q)), jnp.where(lane_lo, jnp.zeros_like(q), q)], 0)

    nt = (((1,), (1,)), ((), ()))

    def finish(ot):
        o = ot[:, :tq] - lam * ot[:, tq:]
        y = o * lax.rsqrt(jnp.mean(o * o, axis=0, keepdims=True) + EPS) * g_ref[...] * (1.0 - lam_init)
        o_ref[...] = y.T.astype(o_ref.dtype)

    def attend(shifted):
        nkeys = k_ref.shape[0]
        blocks = [(k0, min(k0 + ATTN_KEY_BLOCK, nkeys)) for k0 in range(0, nkeys, ATTN_KEY_BLOCK)]
        scores = lambda b: lax.dot_general(k_ref[b[0]:b[1], :], qcat, nt, preferred_element_type=F32)
        if shifted:
            m = jnp.full((1, 2 * tq), -jnp.inf, F32)
            for blk in blocks:
                m = jnp.maximum(m, jnp.max(scores(blk), axis=0, keepdims=True))
        l = jnp.zeros((1, 2 * tq), F32)
        acc = jnp.zeros((LANES, 2 * tq), F32)
        st = scores(blocks[0])
        for j, (k0, k1) in enumerate(blocks):
            st_next = scores(blocks[j + 1]) if j + 1 < len(blocks) else None
            e = jnp.exp2(st - m) if shifted else jnp.exp2(st)
            l = l + jnp.sum(e, axis=0, keepdims=True)
            acc = acc + _dot(vt_ref[:, k0:k1], e.astype(BF16))
            st = st_next
        finish(acc / l)

    @pl.when(safe_ref[0] == 1)
    def _():
        attend(False)

    @pl.when(safe_ref[0] != 1)
    def _():
        attend(True)


def _da_attn(safe, lam4, qn, kn, vt, subg_col, lam_init, CTX, context):
    B, T, _ = qn.shape
    tq = ROW_TILE
    if context:
        nq, nkeys, out_rows = 1, CTX, CTX
        q_specs = [pl.BlockSpec((None, tq, LANES), lambda b, h, i, s: (b, 0, h))]
    else:
        per_step = 4 if (T - CTX) % (4 * tq) == 0 else 2
        nq, nkeys, out_rows = (T - CTX) // (per_step * tq), T, T - CTX
        q0 = CTX // tq
        q_specs = [pl.BlockSpec((None, tq, LANES), lambda b, h, i, s, j=j: (b, q0 + per_step * i + j, h))
                   for j in range(per_step)]
    grid_spec = pltpu.PrefetchScalarGridSpec(
        num_scalar_prefetch=1, grid=(B, DA_HEADS, nq),
        in_specs=[pl.BlockSpec((4, DA_HEAD_DIM), lambda b, h, i, s: (0, 0))] + q_specs + [
            pl.BlockSpec((None, nkeys, LANES), lambda b, h, i, s: (b, 0, h)),
            pl.BlockSpec((None, LANES, nkeys), lambda b, h, i, s: (b, h, 0)),
            pl.BlockSpec((LANES, 1), lambda b, h, i, s: (0, 0))],
        out_specs=pl.BlockSpec((None, len(q_specs) * tq, LANES), lambda b, h, i, s: (b, i, h)))
    return pl.pallas_call(
        partial(_attn_kernel, lam_init=lam_init), grid_spec=grid_spec,
        out_shape=jax.ShapeDtypeStruct((B, out_rows, D), BF16), name="daattn",
        compiler_params=_cp(("parallel", "parallel", "parallel"), _VMEM_LIMIT))(
            safe, lam4, *([qn] * len(q_specs)), kn, vt, subg_col)


def _merge_kernel(yh_ref, ys_ref, yd_ref, ydc_ref, g0_ref, g1_ref, g2_ref, h_ref, wb_ref, wo_ref, gate_ref, n2_ref,
                  sh_ref, sc_ref, *rest, route, ctx_every):
    yd = yd_ref[...]
    if ctx_every:
        yd = jnp.where(pl.program_id(0) % ctx_every == 0, ydc_ref[...], yd)
    acc = None
    for i, (y, gl_ref) in enumerate(((yh_ref[...], g0_ref), (ys_ref[...], g1_ref), (yd, g2_ref))):
        term = jax.nn.sigmoid(gl_ref[...].astype(F32)) * _dot(y, wb_ref[i])
        acc = term if acc is None else acc + term
    hn = h_ref[...] + gate_ref[...] * _dot(acc.astype(BF16), wo_ref[...])
    xn = hn * lax.rsqrt(jnp.mean(hn * hn, axis=-1, keepdims=True) + EPS) * n2_ref[...]
    fin = xn * (1.0 + sc_ref[...]) + sh_ref[...]
    if not route:
        hout_ref, fin_ref = rest
    else:
        rw_ref, hout_ref, fin_ref, rg_ref = rest
        rw = rw_ref[...]
        fin_hi, rw_hi = fin.astype(BF16), rw.astype(BF16)
        fin_lo, rw_lo = (fin - fin_hi.astype(F32)).astype(BF16), (rw - rw_hi.astype(F32)).astype(BF16)
        logits = _dot(fin_hi, rw_hi) + _dot(fin_lo, rw_hi) + _dot(fin_hi, rw_lo)
        lane = lax.broadcasted_iota(jnp.int32, logits.shape, 1)
        logits = jnp.where(lane < N_EXPERTS, logits, -jnp.inf)
        m1 = jnp.max(logits, axis=1, keepdims=True)
        i1 = jnp.min(jnp.where(logits == m1, lane, LANES), axis=1, keepdims=True)
        rest_l = jnp.where(lane == i1, -jnp.inf, logits)
        m2 = jnp.max(rest_l, axis=1, keepdims=True)
        i2 = jnp.min(jnp.where(rest_l == m2, lane, LANES), axis=1, keepdims=True)
        e2 = jnp.exp(m2 - m1)
        rg_ref[...] = jnp.where(lane == i1, 1.0 / (1.0 + e2), 0.0) + jnp.where(lane == i2, e2 / (1.0 + e2), 0.0)
    hout_ref[...] = hn
    fin_ref[...] = fin.astype(fin_ref.dtype)


def _merge(yh, ys, yd, ydc, p2, h, wb, wo, mod3, n2g, tpb, latent_only, router_w=None):
    rows = h.shape[0]
    if latent_only:
        nt = rows // ROW_TILE // tpb * (tpb - 1)
        src = lambda t: t // (tpb - 1) * tpb + 1 + t % (tpb - 1)
    else:
        nt = rows // ROW_TILE
        src = lambda t: t
    tile = lambda cb: pl.BlockSpec((ROW_TILE, D), lambda t: (src(t), cb))
    mspec = lambda k: pl.BlockSpec((None, 1, D), lambda t: (_modrow(src(t), tpb) * 6 + k, 0, 0))
    if latent_only:
        yd_specs = [pl.BlockSpec((ROW_TILE, D), lambda t: (t, 0)), pl.BlockSpec((ROW_TILE, D), lambda t: (0, 0))]
    else:
        yd_specs = [pl.BlockSpec((ROW_TILE, D), lambda t: (t // tpb * (tpb - 1) + jnp.maximum(t % tpb - 1, 0), 0)),
                    pl.BlockSpec((ROW_TILE, D), lambda t: (t // tpb, 0))]
    in_specs = [tile(0), tile(0)] + yd_specs + [
                tile(_PCOL_GATE // D), tile(_PCOL_GATE // D + 1),
                tile(_PCOL_GATE // D + 2), tile(0), pl.BlockSpec((3, D, D), lambda t: (0, 0, 0)),
                pl.BlockSpec((D, D), lambda t: (0, 0)), mspec(2), pl.BlockSpec((1, D), lambda t: (0, 0)),
                mspec(3), mspec(4)]
    args = [yh, ys, yd, ydc, p2, p2, p2, h, wb, wo, mod3, n2g, mod3, mod3]
    otile = pl.BlockSpec((ROW_TILE, D), lambda t: (t, 0))
    out_specs = [otile, otile]
    out_shape = [jax.ShapeDtypeStruct((nt * ROW_TILE, D), F32), jax.ShapeDtypeStruct((nt * ROW_TILE, D), BF16)]
    if router_w is not None:
        in_specs.append(pl.BlockSpec((D, LANES), lambda t: (0, 0)))
        args.append(router_w)
        out_specs.append(pl.BlockSpec((ROW_TILE, LANES), lambda t: (t, 0)))
        out_shape.append(jax.ShapeDtypeStruct((nt * ROW_TILE, LANES), F32))
    return pl.pallas_call(
        partial(_merge_kernel, route=router_w is not None, ctx_every=0 if latent_only else tpb), grid=(nt,), in_specs=in_specs, out_specs=out_specs,
        out_shape=out_shape, name="merge", compiler_params=_cp(("parallel",), _VMEM_LIMIT))(*args)


def _ffn_kernel(x_ref, w1_ref, w3_ref, w2_ref, h_ref, gc_ref, gl_ref, ng_ref, shc_ref, scc_ref, shl_ref, scl_ref,
                o_ref, xn_ref, acc, *, ctx_rows, tiles_per_batch):
    f = pl.program_id(1)
    x = x_ref[...]
    a = (_silu(_dot(x, w1_ref[...])) * _dot(x, w3_ref[...])).astype(BF16)
    part = _dot(a, w2_ref[...])

    @pl.when(f == 0)
    def _():
        acc[...] = part

    @pl.when(f > 0)
    def _():
        acc[...] += part

    @pl.when(f == pl.num_programs(1) - 1)
    def _():
        tm = x_ref.shape[0]
        r = (pl.program_id(0) % tiles_per_batch) * tm + lax.broadcasted_iota(jnp.int32, (tm, 1), 0)
        is_ctx = r < ctx_rows
        hn = h_ref[...] + jnp.where(is_ctx, gc_ref[...], gl_ref[...]) * acc[...]
        o_ref[...] = hn
        y = hn * lax.rsqrt(jnp.mean(hn * hn, axis=-1, keepdims=True) + EPS) * ng_ref[...]
        xn_ref[...] = (y * (1.0 + jnp.where(is_ctx, scc_ref[...], scl_ref[...]))
                       + jnp.where(is_ctx, shc_ref[...], shl_ref[...])).astype(xn_ref.dtype)


def _ffn(fin, w1, w3, w2, h, mod3, T, CTX, next_g, next_mod3):
    rows = h.shape[0]
    tpb = 4
    tm = T // tpb
    ff = w1.shape[1]
    tf = 1024
    ctx = lambda k: pl.BlockSpec((None, 1, D), lambda i, f: (k, 0, 0))
    lat = lambda k: pl.BlockSpec((None, 1, D), lambda i, f: ((1 + i // tpb) * 6 + k, 0, 0))
    tile = pl.BlockSpec((tm, D), lambda i, f: (i, 0))
    return pl.pallas_call(
        partial(_ffn_kernel, ctx_rows=CTX, tiles_per_batch=tpb), grid=(rows // tm, ff // tf),
        in_specs=[pl.BlockSpec((tm, D), lambda i, f: (i, 0)), pl.BlockSpec((D, tf), lambda i, f: (0, f)),
                  pl.BlockSpec((D, tf), lambda i, f: (0, f)), pl.BlockSpec((tf, D), lambda i, f: (f, 0)),
                  pl.BlockSpec((tm, D), lambda i, f: (i, 0)),
                  ctx(5), lat(5), pl.BlockSpec((1, D), lambda i, f: (0, 0)), ctx(0), ctx(1), lat(0), lat(1)],
        out_specs=[tile, tile],
        out_shape=[jax.ShapeDtypeStruct((rows, D), F32), jax.ShapeDtypeStruct((rows, D), BF16)],
        scratch_shapes=[pltpu.VMEM((tm, D), F32)], name="ffn",
        compiler_params=_cp(("parallel", "arbitrary"), _VMEM_LIMIT_FFN))(
            fin, w1, w3, w2, h, mod3, mod3, next_g, next_mod3, next_mod3, next_mod3, next_mod3)


MOE_TILE = 1024
MOE_CHUNK = 256


def _slots_kernel(rg_ref, tri_ref, slot_ref, slott_ref, cnt_ref):
    blk = tri_ref.shape[0]
    carry = jnp.zeros((1, LANES), F32)
    for i in range(rg_ref.shape[0] // blk):
        sel = rg_ref[i * blk:(i + 1) * blk, :] > 0.0
        self32 = jnp.where(sel, 1.0, 0.0)
        rank = _dot(tri_ref[...], self32.astype(BF16)) + carry
        slot = jnp.where(sel, rank, -1.0)
        slot_ref[i * blk:(i + 1) * blk, :] = slot
        slott_ref[:, i * blk:(i + 1) * blk] = slot.T
        carry = carry + jnp.sum(self32, axis=0, keepdims=True)
    cnt_ref[...] = jnp.broadcast_to(carry, cnt_ref.shape)


def _moe_slots(rgate, tm):
    rows = rgate.shape[0]
    blk = 256
    tri = jnp.asarray(np.tril(np.ones((blk, blk), np.float32), -1), BF16)
    return pl.pallas_call(
        _slots_kernel, grid=(rows // tm,),
        in_specs=[pl.BlockSpec((tm, LANES), lambda i: (i, 0)), pl.BlockSpec((blk, blk), lambda i: (0, 0))],
        out_specs=[pl.BlockSpec((tm, LANES), lambda i: (i, 0)), pl.BlockSpec((LANES, tm), lambda i: (0, i)),
                   pl.BlockSpec((8, LANES), lambda i: (i, 0))],
        out_shape=[jax.ShapeDtypeStruct((rows, LANES), F32), jax.ShapeDtypeStruct((LANES, rows), F32),
                   jax.ShapeDtypeStruct((rows // tm * 8, LANES), F32)], name="moeslots",
        compiler_params=_cp(("parallel",)))(rgate, tri)


def _moe_kernel(nfull_ref, half_ref, x_ref, slott_ref, slot_ref, rg_ref, w1_ref, w3_ref, w2_ref, h_ref, gl_ref, o_ref,
                xe_ref, ye_ref):
    i, e, f = pl.program_id(0), pl.program_id(1), pl.program_id(2)
    CH = MOE_CHUNK
    nfull = nfull_ref[i * N_EXPERTS + e]
    half = half_ref[i * N_EXPERTS + e]

    @pl.when(jnp.logical_and(e == 0, f == 0))
    def _():
        o_ref[...] = h_ref[...]

    def expert(r0, rows):
        xe = xe_ref[pl.ds(r0, rows), :]
        a = (_silu(_dot(xe, w1_ref[...])) * _dot(xe, w3_ref[...])).astype(BF16)
        return _dot(a, w2_ref[...])

    def gather(r0, rows):
        want = lax.broadcasted_iota(jnp.int32, (rows, 1), 0).astype(F32) + r0.astype(F32)
        pick = jnp.where(slott_ref[pl.ds(e, 1), :] == want, 1.0, 0.0).astype(BF16)
        xe_ref[pl.ds(r0, rows), :] = _dot(pick, x_ref[...]).astype(BF16)
        ye_ref[pl.ds(r0, rows), :] = expert(r0, rows)

    def for_chunks(body):
        def full(c, carry):
            body(pl.multiple_of(c * CH, CH), CH)
            return carry

        lax.fori_loop(0, nfull, full, 0)

        @pl.when(half == 1)
        def _():
            body(pl.multiple_of(nfull * CH, CH), CH // 2)

    @pl.when(f == 0)
    def _():
        for_chunks(gather)

    @pl.when(f == 1)
    def _():
        mine = lax.broadcasted_iota(jnp.int32, (1, LANES), 1) == e
        scol = jnp.sum(jnp.where(mine, slot_ref[...], 0.0), axis=1, keepdims=True)
        scale = jnp.sum(jnp.where(mine, rg_ref[...], 0.0), axis=1, keepdims=True) * gl_ref[...]

        def scatter(r0, rows):
            ye = (ye_ref[pl.ds(r0, rows), :] + expert(r0, rows)).astype(BF16)
            want = lax.broadcasted_iota(jnp.int32, (1, rows), 1).astype(F32) + r0.astype(F32)
            put = jnp.where(scol == want, 1.0, 0.0).astype(BF16)
            o_ref[...] += scale * _dot(put, ye)

        for_chunks(scatter)


def _moe(fin, rgate, w1, w3, w2, h, mod3, SEQ):
    rows = h.shape[0]
    tm = min(SEQ, MOE_TILE)
    ff = w1.shape[2]
    tf = ff // 2
    tiles_per_batch = SEQ // tm
    nt = rows // tm
    slot, slott, cnt = _moe_slots(rgate, tm)
    counts = cnt.reshape(nt, 8, LANES)[:, 0, :N_EXPERTS].astype(jnp.int32).reshape(nt * N_EXPERTS)
    rem = counts % MOE_CHUNK
    nfull = counts // MOE_CHUNK + (rem > MOE_CHUNK // 2).astype(jnp.int32)
    half = jnp.logical_and(rem > 0, rem <= MOE_CHUNK // 2).astype(jnp.int32)
    grid_spec = pltpu.PrefetchScalarGridSpec(
        num_scalar_prefetch=2, grid=(nt, N_EXPERTS, 2),
        in_specs=[pl.BlockSpec((tm, D), lambda i, e, f, n, hf: (i, 0)),
                  pl.BlockSpec((8, tm), lambda i, e, f, n, hf: (0, i)),
                  pl.BlockSpec((tm, LANES), lambda i, e, f, n, hf: (i, 0)),
                  pl.BlockSpec((tm, LANES), lambda i, e, f, n, hf: (i, 0)),
                  pl.BlockSpec((None, D, tf), lambda i, e, f, n, hf: (e, 0, f)),
                  pl.BlockSpec((None, D, tf), lambda i, e, f, n, hf: (e, 0, f)),
                  pl.BlockSpec((None, tf, D), lambda i, e, f, n, hf: (e, f, 0)),
                  pl.BlockSpec((tm, D), lambda i, e, f, n, hf: (i, 0)),
                  pl.BlockSpec((None, 1, D), lambda i, e, f, n, hf: ((1 + i // tiles_per_batch) * 6 + 5, 0, 0))],
        out_specs=pl.BlockSpec((tm, D), lambda i, e, f, n, hf: (i, 0)),
        scratch_shapes=[pltpu.VMEM((tm, D), BF16), pltpu.VMEM((tm, D), F32)])
    return pl.pallas_call(
        _moe_kernel, grid_spec=grid_spec, out_shape=jax.ShapeDtypeStruct((rows, D), F32), name="moe",
        compiler_params=_cp(("parallel", "arbitrary", "arbitrary"), _VMEM_LIMIT))(
            nfull, half, fin, slott, slot, rgate, w1, w3, w2, h, mod3)


def _regroup_cols(w):
    hy, ssd, da, gate = 3072, 2592, 3072, 3072
    parts = [w[..., :hy], w[..., hy + ssd + da:hy + ssd + da + gate], w[..., hy + ssd:hy + ssd + da],
             w[..., hy:hy + ssd]]
    pad = jnp.zeros(w.shape[:-1] + (_PCOLS - hy - ssd - da - gate,), w.dtype)
    return jnp.concatenate(parts + [pad], axis=-1)


def kernel(x, c, ctx, c_ctx, w_mod, b_mod, norm1_g, norm2_g, w_in, hy_conv_w, hy_conv_b, hy_w1, hy_b1, hy_w2, hy_b2,
           hy_w3, hy_b3, hy_w4, hy_freq, hy_bias, ssd_conv_w, ssd_conv_b, ssd_dt_bias, ssd_a_log, ssd_d, ssd_norm_g,
           da_q_norm, da_k_norm, da_lambda, da_subln_g, w_branch, w_out, ffn_w1, ffn_w3, ffn_w2, router_w, moe_w1,
           moe_w3, moe_w2):
    B, SEQ, _ = x.shape
    CTX = ctx.shape[1]
    assert CTX == ROW_TILE and SEQ % ROW_TILE == 0 and B % 2 == 0 and B + 1 <= 8
    T = CTX + SEQ
    tpb = T // ROW_TILE
    depth = w_in.shape[0]
    R = B * T

    h = jnp.concatenate([ctx, x], axis=1).reshape(R, D)
    cvec = jnp.zeros((8, D), F32).at[0].set(c_ctx).at[1:1 + B].set(c)
    cos_t, sin_t = _rope_tables(SEQ, CTX)
    tabs_l, tabs_c = _fft_tables(SEQ), _fft_tables(CTX)
    nb = D // LANES
    mods = [_mod(cvec, w_mod[i], b_mod[i][None]).reshape(8 * 6, 1, D) for i in range(depth)]
    xn = None

    for i in range(depth):
        need_ctx = i < depth - 1
        mod3 = mods[i]
        if xn is None:
            xn = _normmod(h, norm1_g[i][None], mod3, T, CTX)
        p2 = _matmul(xn, _regroup_cols(w_in[i].astype(BF16)), T // 2, 2048, BF16)
        p3 = p2.reshape(B, T, _PCOLS)

        cw, cb = hy_conv_w[i], hy_conv_b[i][None]
        hy_args = (hy_w1[i], hy_b1[i], hy_w2[i], hy_b2[i], hy_w3[i], hy_b3[i], hy_w4[i], hy_freq[i])
        kh = _hy_kfft(_hy_filters(SEQ, *hy_args), hy_bias[i], tabs_l, SEQ)
        z1 = _hy_conv(p3, 0, p3, nb, cw, cb, kh, 0, tabs_l, SEQ, CTX, SEQ, 0, True)
        y_hy = _hy_conv(z1, 0, p3, 2 * nb, cw, cb, kh, 1, tabs_l, SEQ, CTX, SEQ, CTX, False)
        if need_ctx:
            khc = _hy_kfft(_hy_filters(CTX, *hy_args), hy_bias[i], tabs_c, CTX)
            z1c = _hy_conv(p3, 0, p3, nb, cw, cb, khc, 0, tabs_c, CTX, 0, CTX, 0, True)
            y_hy = _hy_conv(z1c, 0, p3, 2 * nb, cw, cb, khc, 1, tabs_c, CTX, 0, CTX, 0, False, alias_into=y_hy)

        xbc = _ssd_pre(p3, ssd_conv_w[i], ssd_conv_b[i][None], CTX)
        dtb = jnp.pad(ssd_dt_bias[i].reshape(1, 2 * SSD_HEADS), ((0, 0), (0, LANES - 2 * SSD_HEADS)))
        arow = jnp.pad(-jnp.exp(ssd_a_log[i].astype(F32)).reshape(1, 2 * SSD_HEADS),
                       ((0, 0), (0, LANES - 2 * SSD_HEADS)))
        y_rev = _ssd_scan(xbc, p3, dtb, arow, True, CTX)
        dskip = jnp.repeat(ssd_d[i].astype(F32), SSD_HEADDIM)[None]
        y_ssd = _ssd_scan(xbc, p3, dtb, arow, False, CTX, fin=(y_rev, dskip, ssd_norm_g[i][None]))

        lam_init = 0.8 - 0.6 * math.exp(-0.3 * (i + 1))
        gq = jnp.tile(da_q_norm[i], D // DA_HEAD_DIM)[None]
        gk = jnp.tile(da_k_norm[i], D // DA_HEAD_DIM)[None]
        qn, kn, vt = _da_pre(p2, gq, gk, cos_t, sin_t, tpb)
        score_bound = 8.0 * jnp.max(jnp.abs(da_q_norm[i])) * jnp.max(jnp.abs(da_k_norm[i]))
        safe = (score_bound <= ATTN_SAFE_SCORE).astype(jnp.int32).reshape(1)
        da_args = (safe, da_lambda[i], qn.reshape(B, T, D), kn.reshape(B, T, D), vt, da_subln_g[i][:, None], lam_init,
                   CTX)
        y_da = _da_attn(*da_args, False)
        y_dac = _da_attn(*da_args, True) if need_ctx else y_da

        wb, wo = w_branch[i].astype(BF16), w_out[i].astype(BF16)
        flat = lambda a: a.reshape(-1, D)
        j = i // 2
        if i % 2 == 0:
            h, fin = _merge(flat(y_hy), flat(y_ssd), flat(y_da), flat(y_dac), p2, h, wb, wo, mod3, norm2_g[i][None],
                            tpb, latent_only=not need_ctx)
            assert need_ctx, "dense FFN layers are expected to carry context rows"
            h, xn = _ffn(fin, ffn_w1[j].astype(BF16), ffn_w3[j].astype(BF16), ffn_w2[j].astype(BF16), h, mod3, T, CTX,
                         norm1_g[i + 1][None], mods[i + 1])
        else:
            assert not need_ctx, "expert layers are expected to be latent only"
            rw = jnp.pad(router_w[j], ((0, 0), (0, LANES - N_EXPERTS)))
            h, fin, rgate = _merge(flat(y_hy), flat(y_ssd), flat(y_da), flat(y_dac), p2, h, wb, wo, mod3,
                                   norm2_g[i][None], tpb, latent_only=True, router_w=rw)
            h = _moe(fin, rgate, moe_w1[j].astype(BF16), moe_w3[j].astype(BF16), moe_w2[j].astype(BF16), h, mod3,
                     SEQ)
            xn = None
    return h.reshape(B, SEQ, D)
```
